```python
import math
import jax, jax.numpy as jnp
from jax import lax
import numpy as np

D_MODEL = 4096
BATCH = 4
SEQ = 2048
DEPTH = 1

CHUNK = 64
DN_HEADS = 16
DN_DK = 128
DN_DV = 128
DN_CONV = 4
DN_KEY = DN_HEADS * DN_DK
DN_VAL = DN_HEADS * DN_DV
DN_QKV = 2 * DN_KEY + DN_VAL
GLA_HEADS = 8
GLA_DK = 128
GLA_DV = 256
GLA_KEY = GLA_HEADS * GLA_DK
GLA_VAL = GLA_HEADS * GLA_DV
GLA_RANK = 16
GLA_TAU = 16
GLA_BLOCK = 16
D_FF = 11008
FFN_CONV = 3
EPS = 1e-6
IN_SIZES = (DN_QKV, DN_VAL, DN_HEADS, DN_HEADS, GLA_KEY, GLA_KEY, GLA_VAL, GLA_VAL, GLA_RANK, D_MODEL, D_MODEL)
IN_COLS = DN_QKV + DN_VAL + 2 * DN_HEADS + 2 * GLA_KEY + 2 * GLA_VAL + GLA_RANK + 2 * D_MODEL

kernel_name = 'hybrid_gdn_gla_convglu_block'


def rms_norm(x, w):
    xf = x.astype(jnp.float32)
    y = xf * lax.rsqrt(jnp.mean(xf * xf, axis=-1, keepdims=True) + EPS)
    return (y * w.astype(jnp.float32)).astype(x.dtype)


def gated_rms_norm(o, w, z):
    of = o.astype(jnp.float32)
    y = of * lax.rsqrt(jnp.mean(of * of, axis=-1, keepdims=True) + EPS)
    return y * w.astype(jnp.float32) * jax.nn.silu(z.astype(jnp.float32))


def l2_norm(x):
    return x * lax.rsqrt(jnp.sum(x * x, axis=-1, keepdims=True) + EPS)


def causal_dwconv(x, w):
    K = w.shape[0]
    T = x.shape[1]
    xp = jnp.pad(x, ((0, 0), (K - 1, 0), (0, 0)))
    y = xp[:, 0:T, :] * w[0]
    for i in range(1, K):
        y = y + xp[:, i:i + T, :] * w[i]
    return y


def gated_deltanet_chunked(q, k, v, beta, g):
    Bsz, T, H, dk = q.shape
    dv = v.shape[-1]
    N = T // CHUNK

    def chunks(t):
        return t.reshape(Bsz, N, CHUNK, H, t.shape[-1]).transpose(1, 0, 3, 2, 4)

    q = chunks(q) * (DN_DK ** -0.5)
    k = chunks(k)
    v = chunks(v)
    beta = chunks(beta[..., None])[..., 0]
    g = chunks(g[..., None])[..., 0]
    b = jnp.cumsum(g, axis=-1)
    tril = jnp.tril(jnp.ones((CHUNK, CHUNK), bool))
    decay = jnp.exp(jnp.where(tril, b[..., :, None] - b[..., None, :], -jnp.inf))
    k_beta = k * beta[..., None]
    a = jnp.einsum('nbhid,nbhjd->nbhij', k_beta, k) * decay
    eye = jnp.broadcast_to(jnp.eye(CHUNK, dtype=jnp.float32), a.shape)
    t_inv = lax.linalg.triangular_solve(a, eye, left_side=True, lower=True, unit_diagonal=True)
    w = t_inv @ (k_beta * jnp.exp(b)[..., None])
    u = t_inv @ (v * beta[..., None])
    att = jnp.einsum('nbhid,nbhjd->nbhij', q, k) * decay

    def step(S, c):
        qc, kc, uc, wc, attc, bc = c
        v_new = uc - wc @ S
        o = (qc * jnp.exp(bc)[..., None]) @ S + attc @ v_new
        b_last = bc[..., -1:]
        S = jnp.exp(b_last)[..., None] * S + jnp.einsum('bhcd,bhce->bhde', kc * jnp.exp(b_last - bc)[..., None], v_new)
        return S, o

    S0 = jnp.zeros((Bsz, H, dk, dv), jnp.float32)
    _, o = lax.scan(step, S0, (q, k, u, w, att, b))
    return o.transpose(1, 0, 3, 2, 4).reshape(Bsz, T, H, dv)


def gla_chunked(q, k, v, gk):
    Bsz, T, H, dk = q.shape
    dv = v.shape[-1]
    N = T // GLA_BLOCK

    def blocks(t):
        return t.reshape(Bsz, N, GLA_BLOCK, H, t.shape[-1]).transpose(1, 0, 3, 2, 4)

    causal = jnp.tril(jnp.ones((GLA_BLOCK, GLA_BLOCK), bool))[:, :, None]

    def step(S, blk):
        qb, kb, vb, gb = blk
        b = jnp.cumsum(gb, axis=2)
        diff = jnp.where(causal, b[:, :, :, None, :] - b[:, :, None, :, :], -jnp.inf)
        att = jnp.sum(qb[:, :, :, None, :] * kb[:, :, None, :, :] * jnp.exp(diff), axis=-1)
        o = jnp.einsum('bhid,bhde->bhie', qb * jnp.exp(b), S) + jnp.einsum('bhij,bhje->bhie', att, vb)
        b_last = b[:, :, -1:, :]
        S = jnp.exp(b_last[:, :, 0, :, None]) * S + jnp.einsum('bhjd,bhje->bhde', kb * jnp.exp(b_last - b), vb)
        return S, o

    S0 = jnp.zeros((Bsz, H, dk, dv), jnp.float32)
    _, o = lax.scan(step, S0, (blocks(q * (GLA_DK ** -0.5)), blocks(k), blocks(v), blocks(gk)))
    return o.transpose(1, 0, 3, 2, 4).reshape(Bsz, T, H, dv)


def mixer_sublayer(x, norm_w, w_in, dn_conv_w, dn_a_log, dn_dt_bias, dn_norm_w,
                   gla_w_alpha2, gla_b_alpha, gla_norm_w, w_branch_dn, w_branch_gla, w_out):
    Bsz, T, _ = x.shape
    f32 = jnp.float32
    h = rms_norm(x, norm_w)
    proj = h @ w_in
    split_at = list(np.cumsum(IN_SIZES)[:-1])
    (dn_qkv, dn_z, dn_b, dn_a, gl_q, gl_k, gl_v, gl_r, gl_lr, gate_dn, gate_gla) = jnp.split(proj, split_at, axis=-1)

    qkv = jax.nn.silu(causal_dwconv(dn_qkv, dn_conv_w)).astype(f32)
    dq, dk, dv = jnp.split(qkv, [DN_KEY, 2 * DN_KEY], axis=-1)
    dq = l2_norm(dq.reshape(Bsz, T, DN_HEADS, DN_DK))
    dk = l2_norm(dk.reshape(Bsz, T, DN_HEADS, DN_DK))
    dv = dv.reshape(Bsz, T, DN_HEADS, DN_DV)
    beta = jax.nn.sigmoid(dn_b.astype(f32))
    g = -jnp.exp(dn_a_log.astype(f32)) * jax.nn.softplus(dn_a.astype(f32) + dn_dt_bias.astype(f32))
    o_dn = gated_deltanet_chunked(dq, dk, dv, beta, g)
    o_dn = gated_rms_norm(o_dn, dn_norm_w, dn_z.reshape(Bsz, T, DN_HEADS, DN_DV))
    o_dn = o_dn.reshape(Bsz, T, DN_VAL).astype(x.dtype)

    gq = gl_q.astype(f32).reshape(Bsz, T, GLA_HEADS, GLA_DK)
    gk_ = gl_k.astype(f32).reshape(Bsz, T, GLA_HEADS, GLA_DK)
    gv = gl_v.astype(f32).reshape(Bsz, T, GLA_HEADS, GLA_DV)
    log_alpha = jax.nn.log_sigmoid((gl_lr @ gla_w_alpha2 + gla_b_alpha).astype(f32)) / GLA_TAU
    log_alpha = log_alpha.reshape(Bsz, T, GLA_HEADS, GLA_DK)
    o_gla = gla_chunked(gq, gk_, gv, log_alpha)
    o_gla = gated_rms_norm(o_gla, gla_norm_w, gl_r.reshape(Bsz, T, GLA_HEADS, GLA_DV))
    o_gla = o_gla.reshape(Bsz, T, GLA_VAL).astype(x.dtype)

    merged = jax.nn.sigmoid(gate_dn) * (o_dn @ w_branch_dn) + jax.nn.sigmoid(gate_gla) * (o_gla @ w_branch_gla)
    return x + merged @ w_out


def ffn_sublayer(x, norm_w, w_ffn_in, ffn_conv_w, ffn_conv_b, w_ffn_out):
    h = rms_norm(x, norm_w)
    gate, up = jnp.split(h @ w_ffn_in, 2, axis=-1)
    gate = causal_dwconv(gate, ffn_conv_w) + ffn_conv_b
    return x + (jax.nn.silu(gate) * up) @ w_ffn_out


def setup_inputs(seed: int = 0) -> dict:
    key = jax.random.key(seed)
    ks = jax.random.split(key, 20)
    f32 = jnp.float32
    L = DEPTH

    def nrm(k, shape, scale):
        return jax.random.normal(k, shape, f32) * scale

    x = jax.random.normal(ks[0], (BATCH, SEQ, D_MODEL), f32)
    norm_mix_w = 1.0 + nrm(ks[1], (L, D_MODEL), 0.01)
    w_in = nrm(ks[2], (L, D_MODEL, IN_COLS), D_MODEL ** -0.5)
    dn_conv_w = nrm(ks[3], (L, DN_CONV, DN_QKV), DN_CONV ** -0.5)
    dn_a_log = jnp.log(jax.random.uniform(ks[4], (L, DN_HEADS), f32, 1.0, 16.0))
    dt = jnp.exp(jax.random.uniform(ks[5], (L, DN_HEADS), f32, math.log(1e-3), math.log(1e-1)))
    dn_dt_bias = dt + jnp.log(-jnp.expm1(-dt))
    dn_norm_w = 1.0 + nrm(ks[6], (L, DN_DV), 0.01)
    gla_w_alpha2 = nrm(ks[7], (L, GLA_RANK, GLA_KEY), GLA_RANK ** -0.5)
    gla_b_alpha = nrm(ks[8], (L, GLA_KEY), 0.01)
    gla_norm_w = 1.0 + nrm(ks[9], (L, GLA_DV), 0.01)
    w_branch_dn = nrm(ks[10], (L, DN_VAL, D_MODEL), DN_VAL ** -0.5)
    w_branch_gla = nrm(ks[11], (L, GLA_VAL, D_MODEL), GLA_VAL ** -0.5)
    w_out = nrm(ks[12], (L, D_MODEL, D_MODEL), D_MODEL ** -0.5)
    norm_ffn_w = 1.0 + nrm(ks[13], (L, D_MODEL), 0.01)
    w_ffn_in = nrm(ks[14], (L, D_MODEL, 2 * D_FF), D_MODEL ** -0.5)
    ffn_conv_w = nrm(ks[15], (L, FFN_CONV, D_FF), FFN_CONV ** -0.5)
    ffn_conv_b = nrm(ks[16], (L, D_FF), 0.01)
    w_ffn_out = nrm(ks[17], (L, D_FF, D_MODEL), D_FF ** -0.5)
    norm_final_w = 1.0 + nrm(ks[18], (D_MODEL,), 0.01)
    return {'x': x, 'norm_mix_w': norm_mix_w, 'w_in': w_in, 'dn_conv_w': dn_conv_w,
            'dn_a_log': dn_a_log, 'dn_dt_bias': dn_dt_bias, 'dn_norm_w': dn_norm_w,
            'gla_w_alpha2': gla_w_alpha2, 'gla_b_alpha': gla_b_alpha, 'gla_norm_w': gla_norm_w,
            'w_branch_dn': w_branch_dn, 'w_branch_gla': w_branch_gla, 'w_out': w_out,
            'norm_ffn_w': norm_ffn_w, 'w_ffn_in': w_ffn_in, 'ffn_conv_w': ffn_conv_w,
            'ffn_conv_b': ffn_conv_b, 'w_ffn_out': w_ffn_out, 'norm_final_w': norm_final_w}


def reference(x, norm_mix_w, w_in, dn_conv_w, dn_a_log, dn_dt_bias, dn_norm_w,
              gla_w_alpha2, gla_b_alpha, gla_norm_w, w_branch_dn, w_branch_gla, w_out,
              norm_ffn_w, w_ffn_in, ffn_conv_w, ffn_conv_b, w_ffn_out, norm_final_w):
    for l in range(DEPTH):
        x = mixer_sublayer(x, norm_mix_w[l], w_in[l], dn_conv_w[l], dn_a_log[l], dn_dt_bias[l], dn_norm_w[l],
                           gla_w_alpha2[l], gla_b_alpha[l], gla_norm_w[l], w_branch_dn[l], w_branch_gla[l], w_out[l])
        x = ffn_sublayer(x, norm_ffn_w[l], w_ffn_in[l], ffn_conv_w[l], ffn_conv_b[l], w_ffn_out[l])
    return rms_norm(x, norm_final_w)
```

```python
import functools

import numpy as np
import jax
import jax.numpy as jnp
from jax import lax
from jax.experimental import pallas as pl
from jax.experimental.pallas import tpu as pltpu

F32 = jnp.float32
BF16 = jnp.bfloat16
HI = lax.Precision.HIGHEST

EPS = 1e-6
CHUNK = 64
DN_HEADS = 16
DN_D = 128
DN_CONV = 4
GLA_HEADS = 8
GLA_DK = 128
GLA_DV = 256
GLA_RANK = 16
GLA_TAU = 16.0
FFN_CONV = 3
LANES = 128
HALO = 8

C_DQ, C_DK, C_DV, C_DZ = 0, 2048, 4096, 6144
C_GQ, C_GK, C_GV, C_GR = 8192, 9216, 10240, 12288
C_GATE_DN, C_GATE_GLA = 14336, 18432
N_MAIN = 22528
S_B, S_A, S_LR = 0, 16, 32

VMEM_LIMIT = 56 * 1024 * 1024


def _cparams(sem):
    return pltpu.CompilerParams(dimension_semantics=sem, vmem_limit_bytes=VMEM_LIMIT)


def _silu(x):
    return x * jax.nn.sigmoid(x)


def _softplus(x):
    return jnp.maximum(x, 0.0) + jnp.log1p(jnp.exp(-jnp.abs(x)))


def _dot(a, b, precision=None):
    return jnp.dot(a, b, precision=precision, preferred_element_type=F32)


def _dot_nt(a, b, precision=None):
    return lax.dot_general(a, b, (((1,), (1,)), ((), ())), precision=precision, preferred_element_type=F32)


def _dot_tn(a, b, precision=None):
    return lax.dot_general(a, b, (((0,), (0,)), ((), ())), precision=precision, preferred_element_type=F32)


def _rmsnorm_kernel(x_ref, w_ref, o_ref):
    x = x_ref[...]
    ms = jnp.mean(x * x, axis=-1, keepdims=True)
    o_ref[...] = (x * lax.rsqrt(ms + EPS) * w_ref[...]).astype(o_ref.dtype)


def _rmsnorm(x, w, out_dtype, tm=256):
    M, D = x.shape
    tm = min(tm, M)
    return pl.pallas_call(
        _rmsnorm_kernel,
        grid=(M // tm,),
        in_specs=[pl.BlockSpec((tm, D), lambda i: (i, 0)), pl.BlockSpec((1, D), lambda i: (0, 0))],
        out_specs=pl.BlockSpec((tm, D), lambda i: (i, 0)),
        out_shape=jax.ShapeDtypeStruct((M, D), out_dtype),
        compiler_params=_cparams(("parallel",)),
        name="rmsnorm",
    )(x, w.reshape(1, D))


def _mm_kernel(a_ref, b_ref, o_ref):
    o_ref[...] = _dot(a_ref[...], b_ref[...]).astype(o_ref.dtype)


def _matmul(a, b, out_dtype, tm, tn, name):
    M, K = a.shape
    N = b.shape[1]
    tm = min(tm, M)
    tn = min(tn, N)
    return pl.pallas_call(
        _mm_kernel,
        grid=(M // tm, N // tn),
        in_specs=[pl.BlockSpec((tm, K), lambda i, j: (i, 0)), pl.BlockSpec((K, tn), lambda i, j: (0, j))],
        out_specs=pl.BlockSpec((tm, tn), lambda i, j: (i, j)),
        out_shape=jax.ShapeDtypeStruct((M, N), out_dtype),
        compiler_params=_cparams(("parallel", "arbitrary")),
        name=name,
    )(a, b)


def _conv_silu(x, cw, row):
    K = cw.shape[0]
    y = x * cw[K - 1:K, :]
    for s in range(1, K):
        xs = jnp.where(row >= s, pltpu.roll(x, s, 0), 0.0)
        y = y + xs * cw[K - 1 - s:K - s, :]
    return _silu(y)


def _l2norm(x):
    return x * lax.rsqrt(jnp.sum(x * x, axis=-1, keepdims=True) + EPS)


def _dn_kernel(q_ref, k_ref, v_ref, z_ref, small_ref, cwq_ref, cwk_ref, cwv_ref, alog_ref, dtb_ref, nw_ref,
               o_ref, qs_s, k_s, v_s, beta_s, g_s, *, seq):
    h = pl.program_id(1)
    C = CHUNK
    row = lax.broadcasted_iota(jnp.int32, (seq, DN_D), 0)

    qs_s[...] = _l2norm(_conv_silu(q_ref[...], cwq_ref[...], row)) * (DN_D ** -0.5)
    k_s[...] = _l2norm(_conv_silu(k_ref[...], cwk_ref[...], row))
    v_s[...] = _conv_silu(v_ref[...], cwv_ref[...], row)
    small = small_ref[...]
    sel_r = lax.broadcasted_iota(jnp.int32, (LANES, LANES), 0)
    sel_b = (sel_r == h + S_B).astype(F32)
    sel_a = (sel_r == h + S_A).astype(F32)
    beta_s[...] = jax.nn.sigmoid(_dot(small, sel_b, HI))
    g_s[...] = -jnp.exp(alog_ref[0]) * _softplus(_dot(small, sel_a, HI) + dtb_ref[0])

    ri = lax.broadcasted_iota(jnp.int32, (C, C), 0)
    ci = lax.broadcasted_iota(jnp.int32, (C, C), 1)
    tril = ri >= ci
    strict = ri > ci
    L = tril.astype(F32)
    SL = strict.astype(F32)
    eye = (ri == ci).astype(F32)
    nw = nw_ref[...]

    def body(n, S):
        sl = pl.ds(pl.multiple_of(n * C, C), C)
        qc, kc, vc, bt, gc = qs_s[sl, :], k_s[sl, :], v_s[sl, :], beta_s[sl, :], g_s[sl, :]
        bc = _dot(L, gc, HI)
        dm = _dot(L, gc[:, :C] * SL, HI)
        decay = jnp.where(tril, jnp.exp(dm), 0.0)
        eb = jnp.exp(bc)
        blast = bc[C - 1:C, :]
        kb = kc * bt
        a = jnp.where(strict, _dot_nt(kb, kc, HI) * decay, 0.0)
        p = eye - a
        x = _dot(a, a, HI)
        for _ in range(4):
            p = p + _dot(p, x, HI)
            x = _dot(x, x, HI)
        p = p + _dot(p, x, HI)
        wu = _dot(p, jnp.concatenate([kb * eb, vc * bt], axis=1), HI)
        w, u = wu[:, :DN_D], wu[:, DN_D:]
        att = _dot_nt(qc, kc, HI) * decay
        v_new = u - _dot(w, S, HI)
        o = _dot(qc * eb, S, HI) + _dot(att, v_new, HI)
        kd = kc * jnp.exp(blast - bc)
        S = jnp.exp(blast) * S + _dot_tn(kd, v_new, HI)
        y = o * lax.rsqrt(jnp.mean(o * o, axis=-1, keepdims=True) + EPS) * nw * _silu(z_ref[sl, :])
        o_ref[sl, :] = y.astype(o_ref.dtype)
        return S

    lax.fori_loop(0, seq // C, body, jnp.zeros((DN_D, DN_D), F32))


def _deltanet(proj, small, conv_w, a_log, dt_bias, norm_w, batch, seq):
    M = proj.shape[0]
    H = DN_HEADS
    blk = lambda c0: pl.BlockSpec((seq, DN_D), lambda b, h: (b, c0 // DN_D + h))
    cw = lambda c0: pl.BlockSpec((DN_CONV, DN_D), lambda b, h: (0, c0 // DN_D + h))
    hp = pl.BlockSpec((1, 1, LANES), lambda b, h: (h, 0, 0))
    alog = jnp.broadcast_to(a_log.reshape(H, 1, 1), (H, 1, LANES))
    dtb = jnp.broadcast_to(dt_bias.reshape(H, 1, 1), (H, 1, LANES))
    return pl.pallas_call(
        functools.partial(_dn_kernel, seq=seq),
        grid=(batch, H),
        in_specs=[blk(C_DQ), blk(C_DK), blk(C_DV), blk(C_DZ),
                  pl.BlockSpec((seq, LANES), lambda b, h: (b, 0)),
                  cw(0), cw(2048), cw(4096), hp, hp,
                  pl.BlockSpec((1, DN_D), lambda b, h: (0, 0))],
        out_specs=pl.BlockSpec((seq, DN_D), lambda b, h: (b, h)),
        out_shape=jax.ShapeDtypeStruct((M, H * DN_D), BF16),
        scratch_shapes=[pltpu.VMEM((seq, DN_D), F32)] * 5,
        compiler_params=_cparams(("parallel", "parallel")),
        name="deltanet",
    )(proj, proj, proj, proj, small, conv_w, conv_w, conv_w, alog, dtb, norm_w.reshape(1, DN_D))


GLA_LEVELS = (32, 16, 8, 4, 2, 1)


def _gla_constants():
    C = CHUNK
    i = np.arange(C)[:, None]
    m = np.arange(C)[None, :]
    blocks = [(m <= i), (m > i)]
    upper_rows, masks = [], []
    for s in GLA_LEVELS:
        p = (i // (2 * s)) * (2 * s)
        m0 = p + s - 1
        up = (i - p) >= s
        blocks.append(np.where(up, (m > m0) & (m <= i), (m > i) & (m <= m0)))
        upper_rows.append(np.broadcast_to(up, (C, LANES)))
        j = m
        pj = (j // (2 * s)) * (2 * s)
        masks.append(up & ((j - pj) < s) & (p == pj))
    masks.append(i == m)
    mcat = np.concatenate(blocks, axis=0).astype(np.float32)
    upper = np.stack(upper_rows).astype(np.float32)
    mask = np.stack(masks).astype(np.float32)
    return mcat, upper, mask


def _gla_kernel(q_ref, k_ref, v_ref, r_ref, small_ref, wa_ref, ba_ref, nw_ref, mcat_ref, upper_ref, mask_ref,
                o_ref, gk_s, st_s, *, seq):
    C = CHUNK
    x_alpha = _dot(small_ref[...], wa_ref[...], HI) + ba_ref[...]
    gk_s[...] = -_softplus(-x_alpha) / GLA_TAU
    st_s[...] = jnp.zeros_like(st_s)
    mcat = mcat_ref[...]
    nw = nw_ref[...]

    def body(n, carry):
        sl = pl.ds(pl.multiple_of(n * C, C), C)
        qc = q_ref[sl, :] * (GLA_DK ** -0.5)
        kc, vc = k_ref[sl, :], v_ref[sl, :]
        ex = jnp.exp(_dot(mcat, gk_s[sl, :], HI))
        eb = ex[0:C, :]
        erest = ex[C:2 * C, :]
        eblast = eb[C - 1:C, :]
        att = mask_ref[len(GLA_LEVELS)] * _dot_nt(qc, kc, HI)
        for l in range(len(GLA_LEVELS)):
            zl = jnp.where(upper_ref[l] > 0.5, qc, kc) * ex[(2 + l) * C:(3 + l) * C, :]
            att = att + mask_ref[l] * _dot_nt(zl, zl, HI)
        st = st_s[...]
        o = _dot_nt(qc * eb, st, HI) + _dot(att, vc, HI)
        st_s[...] = st * eblast + _dot_tn(vc, kc * erest, HI)
        y = o * lax.rsqrt(jnp.mean(o * o, axis=-1, keepdims=True) + EPS) * nw * _silu(r_ref[sl, :])
        o_ref[sl, :] = y.astype(o_ref.dtype)
        return carry

    lax.fori_loop(0, seq // C, body, 0)


def _gla(proj, small, w_alpha2, b_alpha, norm_w, batch, seq):
    M = proj.shape[0]
    H = GLA_HEADS
    mcat, upper, mask = _gla_constants()
    wa = jnp.zeros((LANES, H * GLA_DK), F32).at[S_LR:S_LR + GLA_RANK, :].set(w_alpha2)
    full = lambda a: pl.BlockSpec(a.shape, lambda b, h: (0,) * a.ndim)
    return pl.pallas_call(
        functools.partial(_gla_kernel, seq=seq),
        grid=(batch, H),
        in_specs=[pl.BlockSpec((seq, GLA_DK), lambda b, h: (b, C_GQ // GLA_DK + h)),
                  pl.BlockSpec((seq, GLA_DK), lambda b, h: (b, C_GK // GLA_DK + h)),
                  pl.BlockSpec((seq, GLA_DV), lambda b, h: (b, C_GV // GLA_DV + h)),
                  pl.BlockSpec((seq, GLA_DV), lambda b, h: (b, C_GR // GLA_DV + h)),
                  pl.BlockSpec((seq, LANES), lambda b, h: (b, 0)),
                  pl.BlockSpec((LANES, GLA_DK), lambda b, h: (0, h)),
                  pl.BlockSpec((1, GLA_DK), lambda b, h: (0, h)),
                  pl.BlockSpec((1, GLA_DV), lambda b, h: (0, 0)),
                  full(mcat), full(upper), full(mask)],
        out_specs=pl.BlockSpec((seq, GLA_DV), lambda b, h: (b, h)),
        out_shape=jax.ShapeDtypeStruct((M, H * GLA_DV), BF16),
        scratch_shapes=[pltpu.VMEM((seq, GLA_DK), F32), pltpu.VMEM((GLA_DV, GLA_DK), F32)],
        compiler_params=_cparams(("parallel", "parallel")),
        name="gla",
    )(proj, proj, proj, proj, small, wa, b_alpha.reshape(1, -1), norm_w.reshape(1, GLA_DV),
      jnp.asarray(mcat), jnp.asarray(upper), jnp.asarray(mask))


def _merge_kernel(odn_ref, ogla_ref, pa_ref, pb_ref, gd_ref, gg_ref, o_ref):
    ya = _dot(odn_ref[...], pa_ref[...])
    yb = _dot(ogla_ref[...], pb_ref[...])
    o_ref[...] = (jax.nn.sigmoid(gd_ref[...]) * ya + jax.nn.sigmoid(gg_ref[...]) * yb).astype(o_ref.dtype)


def _merge(o_dn, o_gla, pa, pb, proj, tm=1024, tn=512):
    M, K = o_dn.shape
    N = pa.shape[1]
    tm = min(tm, M)
    return pl.pallas_call(
        _merge_kernel,
        grid=(M // tm, N // tn),
        in_specs=[pl.BlockSpec((tm, K), lambda i, j: (i, 0)),
                  pl.BlockSpec((tm, K), lambda i, j: (i, 0)),
                  pl.BlockSpec((K, tn), lambda i, j: (0, j)),
                  pl.BlockSpec((K, tn), lambda i, j: (0, j)),
                  pl.BlockSpec((tm, tn), lambda i, j: (i, C_GATE_DN // tn + j)),
                  pl.BlockSpec((tm, tn), lambda i, j: (i, C_GATE_GLA // tn + j))],
        out_specs=pl.BlockSpec((tm, tn), lambda i, j: (i, j)),
        out_shape=jax.ShapeDtypeStruct((M, N), BF16),
        compiler_params=_cparams(("parallel", "arbitrary")),
        name="merge",
    )(o_dn, o_gla, pa, pb, proj, proj)


def _mm_res_kernel(a_ref, b_ref, r_ref, o_ref):
    o_ref[...] = r_ref[...] + _dot(a_ref[...], b_ref[...])


def _matmul_residual(a, b, res, tm=1024, tn=512, name="outproj"):
    M, K = a.shape
    N = b.shape[1]
    tm = min(tm, M)
    return pl.pallas_call(
        _mm_res_kernel,
        grid=(M // tm, N // tn),
        in_specs=[pl.BlockSpec((tm, K), lambda i, j: (i, 0)),
                  pl.BlockSpec((K, tn), lambda i, j: (0, j)),
                  pl.BlockSpec((tm, tn), lambda i, j: (i, j))],
        out_specs=pl.BlockSpec((tm, tn), lambda i, j: (i, j)),
        out_shape=jax.ShapeDtypeStruct((M, N), F32),
        compiler_params=_cparams(("parallel", "arbitrary")),
        name=name,
    )(a, b, res)


def _ffn_in_kernel(h_ref, hprev_ref, wg_ref, wu_ref, cw_ref, cb_ref, o_ref, *, tiles_per_seq):
    i = pl.program_id(0)
    tm = h_ref.shape[0]
    h = h_ref[...]
    gate = _dot(h, wg_ref[...])
    up = _dot(h, wu_ref[...])
    first = (i % tiles_per_seq) == 0
    halo = jnp.where(first, 0.0, _dot(hprev_ref[...], wg_ref[...]))
    ext = jnp.concatenate([halo, gate], axis=0)
    cw = cw_ref[...]
    y = gate * cw[FFN_CONV - 1:FFN_CONV, :] + cb_ref[...]
    for s in range(1, FFN_CONV):
        y = y + pltpu.roll(ext, s, 0)[HALO:, :] * cw[FFN_CONV - 1 - s:FFN_CONV - s, :]
    o_ref[...] = (_silu(y) * up).astype(o_ref.dtype)


def _ffn_in(h, wg, wu, conv_w, conv_b, seq, tm=1024, tn=256):
    M, K = h.shape
    N = wg.shape[1]
    tm = min(tm, seq)
    tps = seq // tm
    hb = tm // HALO
    return pl.pallas_call(
        functools.partial(_ffn_in_kernel, tiles_per_seq=tps),
        grid=(M // tm, N // tn),
        in_specs=[pl.BlockSpec((tm, K), lambda i, j: (i, 0)),
                  pl.BlockSpec((HALO, K), lambda i, j: (jnp.maximum(i * hb - 1, 0), 0)),
                  pl.BlockSpec((K, tn), lambda i, j: (0, j)),
                  pl.BlockSpec((K, tn), lambda i, j: (0, j)),
                  pl.BlockSpec((FFN_CONV, tn), lambda i, j: (0, j)),
                  pl.BlockSpec((1, tn), lambda i, j: (0, j))],
        out_specs=pl.BlockSpec((tm, tn), lambda i, j: (i, j)),
        out_shape=jax.ShapeDtypeStruct((M, N), BF16),
        compiler_params=_cparams(("parallel", "arbitrary")),
        name="ffn_in",
    )(h, h, wg, wu, conv_w, conv_b.reshape(1, N))


def _ffn_out_kernel(a_ref, b_ref, r_ref, o_ref, acc_ref):
    kk = pl.program_id(2)

    @pl.when(kk == 0)
    def _():
        acc_ref[...] = r_ref[...]

    acc_ref[...] += _dot(a_ref[...], b_ref[...])

    @pl.when(kk == pl.num_programs(2) - 1)
    def _():
        o_ref[...] = acc_ref[...]


def _ffn_out(a, b, res, tm=1024, tn=512, ksplit=2):
    M, K = a.shape
    N = b.shape[1]
    tm = min(tm, M)
    tk = K // ksplit
    return pl.pallas_call(
        _ffn_out_kernel,
        grid=(M // tm, N // tn, ksplit),
        in_specs=[pl.BlockSpec((tm, tk), lambda i, j, k: (i, k)),
                  pl.BlockSpec((tk, tn), lambda i, j, k: (k, j)),
                  pl.BlockSpec((tm, tn), lambda i, j, k: (i, j))],
        out_specs=pl.BlockSpec((tm, tn), lambda i, j, k: (i, j)),
        out_shape=jax.ShapeDtypeStruct((M, N), F32),
        scratch_shapes=[pltpu.VMEM((tm, tn), F32)],
        compiler_params=_cparams(("parallel", "arbitrary", "arbitrary")),
        name="ffn_out",
    )(a, b, res)


def _layer(x2d, batch, seq, norm_mix_w, w_in, dn_conv_w, dn_a_log, dn_dt_bias, dn_norm_w,
           gla_w_alpha2, gla_b_alpha, gla_norm_w, w_branch_dn, w_branch_gla, w_out,
           norm_ffn_w, w_ffn_in, ffn_conv_w, ffn_conv_b, w_ffn_out):
    D = x2d.shape[1]
    d_ff = w_ffn_out.shape[0]
    w_main = jnp.concatenate([w_in[:, 0:8192].astype(BF16), w_in[:, 8224:14368].astype(BF16),
                              w_in[:, 14384:].astype(BF16)], axis=1)
    w_small = jnp.concatenate([w_in[:, 8192:8224].astype(BF16), w_in[:, 14368:14384].astype(BF16),
                               jnp.zeros((D, LANES - 48), BF16)], axis=1)

    h = _rmsnorm(x2d, norm_mix_w, BF16)
    proj = _matmul(h, w_main, F32, 1024, 512, "inproj")
    small = _matmul(h, w_small, F32, 1024, LANES, "inproj_small")
    o_dn = _deltanet(proj, small, dn_conv_w, dn_a_log, dn_dt_bias, dn_norm_w, batch, seq)
    o_gla = _gla(proj, small, gla_w_alpha2, gla_b_alpha, gla_norm_w, batch, seq)
    merged = _merge(o_dn, o_gla, w_branch_dn.astype(BF16), w_branch_gla.astype(BF16), proj)
    x1 = _matmul_residual(merged, w_out.astype(BF16), x2d)

    h2 = _rmsnorm(x1, norm_ffn_w, BF16)
    act = _ffn_in(h2, w_ffn_in[:, :d_ff].astype(BF16), w_ffn_in[:, d_ff:].astype(BF16),
                  ffn_conv_w, ffn_conv_b, seq)
    return _ffn_out(act, w_ffn_out.astype(BF16), x1)


def kernel(x, norm_mix_w, w_in, dn_conv_w, dn_a_log, dn_dt_bias, dn_norm_w, gla_w_alpha2, gla_b_alpha, gla_norm_w, w_branch_dn, w_branch_gla, w_out, norm_ffn_w, w_ffn_in, ffn_conv_w, ffn_conv_b, w_ffn_out, norm_final_w):
    batch, seq, D = x.shape
    x2d = x.reshape(batch * seq, D)
    for l in range(norm_mix_w.shape[0]):
        x2d = _layer(x2d, batch, seq, norm_mix_w[l], w_in[l], dn_conv_w[l], dn_a_log[l], dn_dt_bias[l],
                     dn_norm_w[l], gla_w_alpha2[l], gla_b_alpha[l], gla_norm_w[l], w_branch_dn[l],
                     w_branch_gla[l], w_out[l], norm_ffn_w[l], w_ffn_in[l], ffn_conv_w[l], ffn_conv_b[l],
                     w_ffn_out[l])
    return _rmsnorm(x2d, norm_final_w, F32).reshape(batch, seq, D)
```

```python
import functools

import numpy as np
import jax
import jax.numpy as jnp
from jax import lax
from jax.experimental import pallas as pl
from jax.experimental.pallas import tpu as pltpu

F32 = jnp.float32
BF16 = jnp.bfloat16
HI = lax.Precision.HIGHEST

EPS = 1e-6
CHUNK = 64
DN_HEADS = 16
DN_D = 128
DN_CONV = 4
GLA_HEADS = 8
GLA_DK = 128
GLA_DV = 256
GLA_RANK = 16
GLA_TAU = 16.0
FFN_CONV = 3
LANES = 128
HALO = 8

C_DQ, C_DK, C_DV, C_DZ = 0, 2048, 4096, 6144
C_GQ, C_GK, C_GV, C_GR = 8192, 9216, 10240, 12288
C_GATE_DN, C_GATE_GLA = 14336, 18432
N_MAIN = 22528
S_B, S_A, S_LR = 0, 16, 32

VMEM_LIMIT = 56 * 1024 * 1024


def _cparams(sem):
    return pltpu.CompilerParams(dimension_semantics=sem, vmem_limit_bytes=VMEM_LIMIT)


def _silu(x):
    return x * jax.nn.sigmoid(x)


def _softplus(x):
    return jnp.maximum(x, 0.0) + jnp.log1p(jnp.exp(-jnp.abs(x)))


def _dot(a, b, precision=None):
    return jnp.dot(a, b, precision=precision, preferred_element_type=F32)


def _dot_nt(a, b, precision=None):
    return lax.dot_general(a, b, (((1,), (1,)), ((), ())), precision=precision, preferred_element_type=F32)


def _dot_tn(a, b, precision=None):
    return lax.dot_general(a, b, (((0,), (0,)), ((), ())), precision=precision, preferred_element_type=F32)


def _rmsnorm_kernel(x_ref, w_ref, o_ref):
    x = x_ref[...]
    ms = jnp.mean(x * x, axis=-1, keepdims=True)
    o_ref[...] = (x * lax.rsqrt(ms + EPS) * w_ref[...]).astype(o_ref.dtype)


def _rmsnorm(x, w, out_dtype, tm=256):
    M, D = x.shape
    tm = min(tm, M)
    return pl.pallas_call(
        _rmsnorm_kernel,
        grid=(M // tm,),
        in_specs=[pl.BlockSpec((tm, D), lambda i: (i, 0)), pl.BlockSpec((1, D), lambda i: (0, 0))],
        out_specs=pl.BlockSpec((tm, D), lambda i: (i, 0)),
        out_shape=jax.ShapeDtypeStruct((M, D), out_dtype),
        compiler_params=_cparams(("parallel",)),
        name="rmsnorm",
    )(x, w.reshape(1, D))


def _mm_kernel(a_ref, b_ref, o_ref):
    o_ref[...] = _dot(a_ref[...], b_ref[...]).astype(o_ref.dtype)


def _matmul(a, b, out_dtype, tm, tn, name):
    M, K = a.shape
    N = b.shape[1]
    tm = min(tm, M)
    tn = min(tn, N)
    return pl.pallas_call(
        _mm_kernel,
        grid=(M // tm, N // tn),
        in_specs=[pl.BlockSpec((tm, K), lambda i, j: (i, 0)), pl.BlockSpec((K, tn), lambda i, j: (0, j))],
        out_specs=pl.BlockSpec((tm, tn), lambda i, j: (i, j)),
        out_shape=jax.ShapeDtypeStruct((M, N), out_dtype),
        compiler_params=_cparams(("parallel", "arbitrary")),
        name=name,
    )(a, b)


def _conv_silu(x, cw, row):
    K = cw.shape[0]
    y = x * cw[K - 1:K, :]
    for s in range(1, K):
        xs = jnp.where(row >= s, pltpu.roll(x, s, 0), 0.0)
        y = y + xs * cw[K - 1 - s:K - s, :]
    return _silu(y)


def _l2norm(x):
    return x * lax.rsqrt(jnp.sum(x * x, axis=-1, keepdims=True) + EPS)


def _split3(x):
    hi = x.astype(BF16)
    r = x - hi.astype(F32)
    mid = r.astype(BF16)
    lo = (r - mid.astype(F32)).astype(BF16)
    return hi, mid, lo


def _sum3(e, width):
    return (e[:, :width] + e[:, width:2 * width]) + e[:, 2 * width:3 * width]


DN_UNROLL = 8


def _dn_kernel(q_ref, k_ref, v_ref, z_ref, small_ref, cwq_ref, cwk_ref, cwv_ref, alog_ref, dtb_ref, nw_ref,
               o_ref, gates_s, qs_s, k_s, v_s, beta_s, g_s, lhs_s, add_s, ebl_s, *, seq):
    h = pl.program_id(1)
    C = CHUNK
    D = DN_D
    n_chunks = seq // C

    @pl.when(h == 0)
    def _():
        small = small_ref[...]
        lane = lax.broadcasted_iota(jnp.int32, small.shape, 1)
        g_all = -jnp.exp(alog_ref[...]) * _softplus(small + dtb_ref[...])
        gates_s[...] = jnp.where(lane < S_A, jax.nn.sigmoid(small), g_all)

    row = lax.broadcasted_iota(jnp.int32, (seq, D), 0)
    qs_s[...] = _l2norm(_conv_silu(q_ref[...], cwq_ref[...], row)) * (D ** -0.5)
    k_s[...] = _l2norm(_conv_silu(k_ref[...], cwk_ref[...], row))
    v_s[...] = _conv_silu(v_ref[...], cwv_ref[...], row)
    sr = lax.broadcasted_iota(jnp.int32, (3 * LANES, 2 * LANES), 0) % LANES
    sc = lax.broadcasted_iota(jnp.int32, (3 * LANES, 2 * LANES), 1)
    sel = (sr == jnp.where(sc < LANES, h + S_B, h + S_A)).astype(BF16)
    rep = _dot(jnp.concatenate(_split3(gates_s[...]), axis=1), sel)
    beta_s[...] = rep[:, :LANES]
    g_s[...] = rep[:, LANES:]

    ri = lax.broadcasted_iota(jnp.int32, (C, C), 0)
    ci = lax.broadcasted_iota(jnp.int32, (C, C), 1)
    tril = ri >= ci
    strict = ri > ci
    l_bf = tril.astype(BF16)
    eye = (ri == ci).astype(F32)
    rl = lax.broadcasted_iota(jnp.int32, (C, LANES), 0)
    cl = lax.broadcasted_iota(jnp.int32, (C, LANES), 1)
    sl_pad = (rl > cl).astype(F32)

    def body_a(g, carry):
        U = range(DN_UNROLL)
        ns = [g * DN_UNROLL + u for u in U]
        sls = [pl.ds(pl.multiple_of(n * C, C), C) for n in ns]
        qc = [qs_s[s, :] for s in sls]
        kc = [k_s[s, :] for s in sls]
        vc = [v_s[s, :] for s in sls]
        bt = [beta_s[s, :] for s in sls]
        gc = [g_s[s, :] for s in sls]
        e = [_dot(l_bf, jnp.concatenate(_split3(gc[u]) + _split3(gc[u] * sl_pad), axis=1)) for u in U]
        kb = [kc[u] * bt[u] for u in U]
        gram = [_dot_nt(jnp.concatenate([kb[u], qc[u]], axis=0).astype(BF16), kc[u].astype(BF16)) for u in U]
        bc = [_sum3(e[u][:, :3 * LANES], LANES) for u in U]
        dm = [_sum3(e[u][:, 3 * LANES:], LANES)[:, :C] for u in U]
        decay = [jnp.where(tril, jnp.exp(dm[u]), 0.0) for u in U]
        eb = [jnp.exp(bc[u]) for u in U]
        a = [jnp.where(strict, gram[u][:C] * decay[u], 0.0) for u in U]
        att = [(gram[u][C:] * decay[u]).astype(BF16) for u in U]
        p = [eye - a[u] for u in U]
        a_bf = [a[u].astype(BF16) for u in U]
        x = [_dot(a_bf[u], a_bf[u]) for u in U]
        for _ in range(4):
            px = [_dot(jnp.concatenate([p[u], x[u]], axis=0).astype(BF16), x[u].astype(BF16)) for u in U]
            p = [p[u] + px[u][:C] for u in U]
            x = [px[u][C:] for u in U]
        px = [_dot(p[u].astype(BF16), x[u].astype(BF16)) for u in U]
        p = [(p[u] + px[u]).astype(BF16) for u in U]
        rhs = [jnp.concatenate([kb[u] * eb[u], vc[u] * bt[u]], axis=1).astype(BF16) for u in U]
        wu = [_dot(p[u], rhs[u]).astype(BF16) for u in U]
        kd = [(kc[u] * jnp.exp(bc[u][C - 1:C, :] - bc[u])).astype(BF16) for u in U]
        aw = [_dot(att[u], wu[u]) for u in U]
        mn = [_dot_tn(kd[u], wu[u]) for u in U]
        for u in U:
            n = ns[u]
            lhs_s[n, 0:D, :] = mn[u][:, :D].astype(BF16)
            lhs_s[n, D:D + C, :] = (qc[u] * eb[u] - aw[u][:, :D]).astype(BF16)
            add_s[n, 0:D, :] = mn[u][:, D:]
            add_s[n, D:D + C, :] = aw[u][:, D:]
            ebl_s[n] = jnp.broadcast_to(eb[u][C - 1:C, :], (8, LANES))
        return carry

    lax.fori_loop(0, n_chunks // DN_UNROLL, body_a, 0)

    nw = nw_ref[...]

    def finish(n, o):
        sl = pl.ds(pl.multiple_of(n * C, C), C)
        y = o * lax.rsqrt(jnp.mean(o * o, axis=-1, keepdims=True) + EPS) * nw * _silu(z_ref[sl, :])
        o_ref[sl, :] = y.astype(o_ref.dtype)

    def body_b(n, carry):
        S, o_prev = carry
        finish(jnp.maximum(n - 1, 0), o_prev)
        ps = _dot(lhs_s[n], S.astype(BF16))
        add = add_s[n]
        o = ps[D:] + add[D:]
        S = ebl_s[n][0:1, :] * S + (add[:D] - ps[:D])
        return S, o

    _, o_last = lax.fori_loop(0, n_chunks, body_b, (jnp.zeros((D, D), F32), jnp.zeros((C, D), F32)))
    finish(n_chunks - 1, o_last)


def _deltanet(proj, small, conv_w, a_log, dt_bias, norm_w, batch, seq):
    M = proj.shape[0]
    H = DN_HEADS
    n_chunks = seq // CHUNK
    blk = lambda c0: pl.BlockSpec((seq, DN_D), lambda b, h: (b, c0 // DN_D + h))
    cw = lambda c0: pl.BlockSpec((DN_CONV, DN_D), lambda b, h: (0, c0 // DN_D + h))
    row = pl.BlockSpec((1, LANES), lambda b, h: (0, 0))
    alog = jnp.zeros((1, LANES), F32).at[0, S_A:S_A + H].set(a_log)
    dtb = jnp.zeros((1, LANES), F32).at[0, S_A:S_A + H].set(dt_bias)
    seq_f32 = pltpu.VMEM((seq, DN_D), F32)
    return pl.pallas_call(
        functools.partial(_dn_kernel, seq=seq),
        grid=(batch, H),
        in_specs=[blk(C_DQ), blk(C_DK), blk(C_DV), blk(C_DZ),
                  pl.BlockSpec((seq, LANES), lambda b, h: (b, 0)),
                  cw(0), cw(2048), cw(4096), row, row, row],
        out_specs=pl.BlockSpec((seq, DN_D), lambda b, h: (b, h)),
        out_shape=jax.ShapeDtypeStruct((M, H * DN_D), BF16),
        scratch_shapes=[seq_f32] * 6 + [pltpu.VMEM((n_chunks, DN_D + CHUNK, DN_D), BF16),
                                        pltpu.VMEM((n_chunks, DN_D + CHUNK, DN_D), F32),
                                        pltpu.VMEM((n_chunks, 8, LANES), F32)],
        compiler_params=_cparams(("parallel", "arbitrary")),
        name="deltanet",
    )(proj, proj, proj, proj, small, conv_w, conv_w, conv_w, alog, dtb, norm_w.reshape(1, DN_D))


GLA_LEVELS = (32, 16, 8, 4, 2, 1)
GLA_UNROLL = 4


def _gla_constants():
    C = CHUNK
    i = np.arange(C)[:, None]
    m = np.arange(C)[None, :]
    blocks = [(m <= i), (m > i)]
    upper_rows, masks = [], []
    for s in GLA_LEVELS:
        p = (i // (2 * s)) * (2 * s)
        m0 = p + s - 1
        up = (i - p) >= s
        blocks.append(np.where(up, (m > m0) & (m <= i), (m > i) & (m <= m0)))
        upper_rows.append(np.broadcast_to(up, (C, LANES)))
        j = m
        pj = (j // (2 * s)) * (2 * s)
        masks.append(up & ((j - pj) < s) & (p == pj))
    masks.append(i == m)
    mcat = np.concatenate(blocks, axis=0).astype(np.float32)
    upper = np.stack(upper_rows).astype(np.float32)
    mask = np.stack(masks).astype(np.float32)
    return mcat, upper, mask


def _gla_kernel(q_ref, k_ref, v_ref, r_ref, small_ref, wa_ref, ba_ref, nw_ref, mcat_ref, upper_ref, mask_ref,
                o_ref, gk_s, st_s, *, seq):
    C = CHUNK
    x_alpha = _dot(small_ref[...], wa_ref[...], HI) + ba_ref[...]
    gk_s[...] = -_softplus(-x_alpha) / GLA_TAU
    st_s[...] = jnp.zeros_like(st_s)
    mcat = mcat_ref[...]
    nw = nw_ref[...]
    n_lvl = len(GLA_LEVELS)

    def body(g, carry):
        U = range(GLA_UNROLL)
        sls = [pl.ds(pl.multiple_of((g * GLA_UNROLL + u) * C, C), C) for u in U]
        qc = [q_ref[s, :] * (GLA_DK ** -0.5) for s in sls]
        kc = [k_ref[s, :] for s in sls]
        v_bf = [v_ref[s, :].astype(BF16) for s in sls]
        e = [_dot(mcat, jnp.concatenate(_split3(gk_s[s, :]), axis=1)) for s in sls]
        ex = [jnp.exp(_sum3(e[u], GLA_DK)) for u in U]
        k_bf = [kc[u].astype(BF16) for u in U]
        att = [mask_ref[n_lvl] * _dot_nt(qc[u].astype(BF16), k_bf[u]) for u in U]
        for l in range(n_lvl):
            up = upper_ref[l] > 0.5
            zl = [(jnp.where(up, qc[u], kc[u]) * ex[u][(2 + l) * C:(3 + l) * C, :]).astype(BF16) for u in U]
            gl = [_dot_nt(zl[u], zl[u]) for u in U]
            att = [att[u] + mask_ref[l] * gl[u] for u in U]
        ds = [_dot_tn(v_bf[u], (kc[u] * ex[u][C:2 * C, :]).astype(BF16)) for u in U]
        ov = [_dot(att[u].astype(BF16), v_bf[u]) for u in U]
        st = st_s[...]
        for u in U:
            eb = ex[u][0:C, :]
            o = _dot_nt((qc[u] * eb).astype(BF16), st.astype(BF16)) + ov[u]
            st = st * eb[C - 1:C, :] + ds[u]
            y = o * lax.rsqrt(jnp.mean(o * o, axis=-1, keepdims=True) + EPS) * nw * _silu(r_ref[sls[u], :])
            o_ref[sls[u], :] = y.astype(o_ref.dtype)
        st_s[...] = st
        return carry

    lax.fori_loop(0, seq // (C * GLA_UNROLL), body, 0)


def _gla(proj, small, w_alpha2, b_alpha, norm_w, batch, seq):
    M = proj.shape[0]
    H = GLA_HEADS
    mcat, upper, mask = _gla_constants()
    wa = jnp.zeros((LANES, H * GLA_DK), F32).at[S_LR:S_LR + GLA_RANK, :].set(w_alpha2)
    full = lambda a: pl.BlockSpec(a.shape, lambda b, h: (0,) * a.ndim)
    return pl.pallas_call(
        functools.partial(_gla_kernel, seq=seq),
        grid=(batch, H),
        in_specs=[pl.BlockSpec((seq, GLA_DK), lambda b, h: (b, C_GQ // GLA_DK + h)),
                  pl.BlockSpec((seq, GLA_DK), lambda b, h: (b, C_GK // GLA_DK + h)),
                  pl.BlockSpec((seq, GLA_DV), lambda b, h: (b, C_GV // GLA_DV + h)),
                  pl.BlockSpec((seq, GLA_DV), lambda b, h: (b, C_GR // GLA_DV + h)),
                  pl.BlockSpec((seq, LANES), lambda b, h: (b, 0)),
                  pl.BlockSpec((LANES, GLA_DK), lambda b, h: (0, h)),
                  pl.BlockSpec((1, GLA_DK), lambda b, h: (0, h)),
                  pl.BlockSpec((1, GLA_DV), lambda b, h: (0, 0)),
                  full(mcat), full(upper), full(mask)],
        out_specs=pl.BlockSpec((seq, GLA_DV), lambda b, h: (b, h)),
        out_shape=jax.ShapeDtypeStruct((M, H * GLA_DV), BF16),
        scratch_shapes=[pltpu.VMEM((seq, GLA_DK), F32), pltpu.VMEM((GLA_DV, GLA_DK), F32)],
        compiler_params=_cparams(("parallel", "parallel")),
        name="gla",
    )(proj, proj, proj, proj, small, wa, b_alpha.reshape(1, -1), norm_w.reshape(1, GLA_DV),
      jnp.asarray(mcat, BF16), jnp.asarray(upper), jnp.asarray(mask))


def _merge_kernel(odn_ref, ogla_ref, pa_ref, pb_ref, gd_ref, gg_ref, o_ref):
    ya = _dot(odn_ref[...], pa_ref[...])
    yb = _dot(ogla_ref[...], pb_ref[...])
    o_ref[...] = (jax.nn.sigmoid(gd_ref[...]) * ya + jax.nn.sigmoid(gg_ref[...]) * yb).astype(o_ref.dtype)


def _merge(o_dn, o_gla, pa, pb, proj, tm=1024, tn=512):
    M, K = o_dn.shape
    N = pa.shape[1]
    tm = min(tm, M)
    return pl.pallas_call(
        _merge_kernel,
        grid=(M // tm, N // tn),
        in_specs=[pl.BlockSpec((tm, K), lambda i, j: (i, 0)),
                  pl.BlockSpec((tm, K), lambda i, j: (i, 0)),
                  pl.BlockSpec((K, tn), lambda i, j: (0, j)),
                  pl.BlockSpec((K, tn), lambda i, j: (0, j)),
                  pl.BlockSpec((tm, tn), lambda i, j: (i, C_GATE_DN // tn + j)),
                  pl.BlockSpec((tm, tn), lambda i, j: (i, C_GATE_GLA // tn + j))],
        out_specs=pl.BlockSpec((tm, tn), lambda i, j: (i, j)),
        out_shape=jax.ShapeDtypeStruct((M, N), BF16),
        compiler_params=_cparams(("parallel", "arbitrary")),
        name="merge",
    )(o_dn, o_gla, pa, pb, proj, proj)


def _mm_res_kernel(a_ref, b_ref, r_ref, o_ref):
    o_ref[...] = r_ref[...] + _dot(a_ref[...], b_ref[...])


def _matmul_residual(a, b, res, tm=1024, tn=512, name="outproj"):
    M, K = a.shape
    N = b.shape[1]
    tm = min(tm, M)
    return pl.pallas_call(
        _mm_res_kernel,
        grid=(M // tm, N // tn),
        in_specs=[pl.BlockSpec((tm, K), lambda i, j: (i, 0)),
                  pl.BlockSpec((K, tn), lambda i, j: (0, j)),
                  pl.BlockSpec((tm, tn), lambda i, j: (i, j))],
        out_specs=pl.BlockSpec((tm, tn), lambda i, j: (i, j)),
        out_shape=jax.ShapeDtypeStruct((M, N), F32),
        compiler_params=_cparams(("parallel", "arbitrary")),
        name=name,
    )(a, b, res)


def _ffn_in_kernel(h_ref, hprev_ref, wg_ref, wu_ref, cw_ref, cb_ref, o_ref, *, tiles_per_seq):
    i = pl.program_id(0)
    tm = h_ref.shape[0]
    h = h_ref[...]
    gate = _dot(h, wg_ref[...])
    up = _dot(h, wu_ref[...])
    first = (i % tiles_per_seq) == 0
    halo = jnp.where(first, 0.0, _dot(hprev_ref[...], wg_ref[...]))
    ext = jnp.concatenate([halo, gate], axis=0)
    cw = cw_ref[...]
    y = gate * cw[FFN_CONV - 1:FFN_CONV, :] + cb_ref[...]
    for s in range(1, FFN_CONV):
        y = y + pltpu.roll(ext, s, 0)[HALO:, :] * cw[FFN_CONV - 1 - s:FFN_CONV - s, :]
    o_ref[...] = (_silu(y) * up).astype(o_ref.dtype)


def _ffn_in(h, wg, wu, conv_w, conv_b, seq, tm=1024, tn=256):
    M, K = h.shape
    N = wg.shape[1]
    tm = min(tm, seq)
    tps = seq // tm
    hb = tm // HALO
    return pl.pallas_call(
        functools.partial(_ffn_in_kernel, tiles_per_seq=tps),
        grid=(M // tm, N // tn),
        in_specs=[pl.BlockSpec((tm, K), lambda i, j: (i, 0)),
                  pl.BlockSpec((HALO, K), lambda i, j: (jnp.maximum(i * hb - 1, 0), 0)),
                  pl.BlockSpec((K, tn), lambda i, j: (0, j)),
                  pl.BlockSpec((K, tn), lambda i, j: (0, j)),
                  pl.BlockSpec((FFN_CONV, tn), lambda i, j: (0, j)),
                  pl.BlockSpec((1, tn), lambda i, j: (0, j))],
        out_specs=pl.BlockSpec((tm, tn), lambda i, j: (i, j)),
        out_shape=jax.ShapeDtypeStruct((M, N), BF16),
        compiler_params=_cparams(("parallel", "arbitrary")),
        name="ffn_in",
    )(h, h, wg, wu, conv_w, conv_b.reshape(1, N))


def _ffn_out_kernel(a_ref, b_ref, r_ref, o_ref, acc_ref):
    kk = pl.program_id(2)

    @pl.when(kk == 0)
    def _():
        acc_ref[...] = r_ref[...]

    acc_ref[...] += _dot(a_ref[...], b_ref[...])

    @pl.when(kk == pl.num_programs(2) - 1)
    def _():
        o_ref[...] = acc_ref[...]


def _ffn_out(a, b, res, tm=1024, tn=512, ksplit=2):
    M, K = a.shape
    N = b.shape[1]
    tm = min(tm, M)
    tk = K // ksplit
    return pl.pallas_call(
        _ffn_out_kernel,
        grid=(M // tm, N // tn, ksplit),
        in_specs=[pl.BlockSpec((tm, tk), lambda i, j, k: (i, k)),
                  pl.BlockSpec((tk, tn), lambda i, j, k: (k, j)),
                  pl.BlockSpec((tm, tn), lambda i, j, k: (i, j))],
        out_specs=pl.BlockSpec((tm, tn), lambda i, j, k: (i, j)),
        out_shape=jax.ShapeDtypeStruct((M, N), F32),
        scratch_shapes=[pltpu.VMEM((tm, tn), F32)],
        compiler_params=_cparams(("parallel", "arbitrary", "arbitrary")),
        name="ffn_out",
    )(a, b, res)


def _layer(x2d, batch, seq, norm_mix_w, w_in, dn_conv_w, dn_a_log, dn_dt_bias, dn_norm_w,
           gla_w_alpha2, gla_b_alpha, gla_norm_w, w_branch_dn, w_branch_gla, w_out,
           norm_ffn_w, w_ffn_in, ffn_conv_w, ffn_conv_b, w_ffn_out):
    D = x2d.shape[1]
    d_ff = w_ffn_out.shape[0]
    w_main = jnp.concatenate([w_in[:, 0:8192].astype(BF16), w_in[:, 8224:14368].astype(BF16),
                              w_in[:, 14384:].astype(BF16)], axis=1)
    w_small = jnp.concatenate([w_in[:, 8192:8224].astype(BF16), w_in[:, 14368:14384].astype(BF16),
                               jnp.zeros((D, LANES - 48), BF16)], axis=1)

    h = _rmsnorm(x2d, norm_mix_w, BF16)
    proj = _matmul(h, w_main, F32, 1024, 512, "inproj")
    small = _matmul(h, w_small, F32, 1024, LANES, "inproj_small")
    o_dn = _deltanet(proj, small, dn_conv_w, dn_a_log, dn_dt_bias, dn_norm_w, batch, seq)
    o_gla = _gla(proj, small, gla_w_alpha2, gla_b_alpha, gla_norm_w, batch, seq)
    merged = _merge(o_dn, o_gla, w_branch_dn.astype(BF16), w_branch_gla.astype(BF16), proj)
    x1 = _matmul_residual(merged, w_out.astype(BF16), x2d)

    h2 = _rmsnorm(x1, norm_ffn_w, BF16)
    act = _ffn_in(h2, w_ffn_in[:, :d_ff].astype(BF16), w_ffn_in[:, d_ff:].astype(BF16),
                  ffn_conv_w, ffn_conv_b, seq)
    return _ffn_out(act, w_ffn_out.astype(BF16), x1)


def kernel(x, norm_mix_w, w_in, dn_conv_w, dn_a_log, dn_dt_bias, dn_norm_w, gla_w_alpha2, gla_b_alpha, gla_norm_w, w_branch_dn, w_branch_gla, w_out, norm_ffn_w, w_ffn_in, ffn_conv_w, ffn_conv_b, w_ffn_out, norm_final_w):
    batch, seq, D = x.shape
    x2d = x.reshape(batch * seq, D)
    for l in range(norm_mix_w.shape[0]):
        x2d = _layer(x2d, batch, seq, norm_mix_w[l], w_in[l], dn_conv_w[l], dn_a_log[l], dn_dt_bias[l],
                     dn_norm_w[l], gla_w_alpha2[l], gla_b_alpha[l], gla_norm_w[l], w_branch_dn[l],
                     w_branch_gla[l], w_out[l], norm_ffn_w[l], w_ffn_in[l], ffn_conv_w[l], ffn_conv_b[l],
                     w_ffn_out[l])
    return _rmsnorm(x2d, norm_final_w, F32).reshape(batch, seq, D)
```

```python
import functools

import numpy as np
import jax
import jax.numpy as jnp
from jax import lax
from jax.experimental import pallas as pl
from jax.experimental.pallas import tpu as pltpu

F32 = jnp.float32
BF16 = jnp.bfloat16
HI = lax.Precision.HIGHEST

EPS = 1e-6
CHUNK = 64
DN_HEADS = 16
DN_D = 128
DN_CONV = 4
GLA_HEADS = 8
GLA_DK = 128
GLA_DV = 256
GLA_RANK = 16
GLA_TAU = 16.0
FFN_CONV = 3
LANES = 128
HALO = 8

C_DQ, C_DK, C_DV, C_DZ = 0, 2048, 4096, 6144
C_DGATE = 8192
C_GQ, C_GK, C_GV, C_GR = 8224, 9248, 10272, 12320
C_LR = 14368
C_GATE_DN, C_GATE_GLA = 14384, 18480
S_B, S_A = 0, 16
S_LR = C_LR % LANES
GLA_SHIFT = C_GQ % LANES
GATE_SHIFT = C_GATE_DN % LANES
assert C_GK % LANES == C_GV % LANES == C_GR % LANES == GLA_SHIFT and C_GATE_GLA % LANES == GATE_SHIFT

VMEM_LIMIT = 56 * 1024 * 1024


def _cparams(sem):
    return pltpu.CompilerParams(dimension_semantics=sem, vmem_limit_bytes=VMEM_LIMIT)


def _silu(x):
    return x * jax.nn.sigmoid(x)


def _softplus(x):
    return jnp.maximum(x, 0.0) + jnp.log1p(jnp.exp(-jnp.abs(x)))


def _dot(a, b, precision=None):
    return jnp.dot(a, b, precision=precision, preferred_element_type=F32)


def _dot_nt(a, b, precision=None):
    return lax.dot_general(a, b, (((1,), (1,)), ((), ())), precision=precision, preferred_element_type=F32)


def _dot_tn(a, b, precision=None):
    return lax.dot_general(a, b, (((0,), (0,)), ((), ())), precision=precision, preferred_element_type=F32)


def _rmsnorm_kernel(x_ref, w_ref, o_ref):
    x = x_ref[...]
    ms = jnp.mean(x * x, axis=-1, keepdims=True)
    o_ref[...] = (x * lax.rsqrt(ms + EPS) * w_ref[...]).astype(o_ref.dtype)


def _rmsnorm(x, w, out_dtype, tm=256):
    M, D = x.shape
    tm = min(tm, M)
    return pl.pallas_call(
        _rmsnorm_kernel,
        grid=(M // tm,),
        in_specs=[pl.BlockSpec((tm, D), lambda i: (i, 0)), pl.BlockSpec((1, D), lambda i: (0, 0))],
        out_specs=pl.BlockSpec((tm, D), lambda i: (i, 0)),
        out_shape=jax.ShapeDtypeStruct((M, D), out_dtype),
        compiler_params=_cparams(("parallel",)),
        name="rmsnorm",
    )(x, w.reshape(1, D))


def _inproj_kernel(a_ref, w_ref, o_ref):
    o_ref[...] = _dot(a_ref[...], w_ref[...].astype(BF16))


def _inproj(a, w, tm=2048, tn=512):
    M, K = a.shape
    N = w.shape[1]
    tm = min(tm, M)
    return pl.pallas_call(
        _inproj_kernel,
        grid=(M // tm, pl.cdiv(N, tn)),
        in_specs=[pl.BlockSpec((tm, K), lambda i, j: (i, 0), pipeline_mode=pl.Buffered(1)),
                  pl.BlockSpec((K, tn), lambda i, j: (0, j))],
        out_specs=pl.BlockSpec((tm, tn), lambda i, j: (i, j)),
        out_shape=jax.ShapeDtypeStruct((M, N), F32),
        compiler_params=_cparams(("parallel", "arbitrary")),
        name="inproj",
    )(a, w)


def _lane_shift(tiles, shift):
    lane = lax.broadcasted_iota(jnp.int32, tiles[0].shape, 1)
    rolled = [pltpu.roll(t, LANES - shift, 1) for t in tiles]
    return [jnp.where(lane < LANES - shift, rolled[i], rolled[i + 1]) for i in range(len(tiles) - 1)]


def _conv_silu(x, cw, row):
    K = cw.shape[0]
    y = x * cw[K - 1:K, :]
    for s in range(1, K):
        xs = jnp.where(row >= s, pltpu.roll(x, s, 0), 0.0)
        y = y + xs * cw[K - 1 - s:K - s, :]
    return _silu(y)


def _l2norm(x):
    return x * lax.rsqrt(jnp.sum(x * x, axis=-1, keepdims=True) + EPS)


def _split3(x):
    hi = x.astype(BF16)
    r = x - hi.astype(F32)
    mid = r.astype(BF16)
    lo = (r - mid.astype(F32)).astype(BF16)
    return hi, mid, lo


def _sum3(e, width):
    return (e[:, :width] + e[:, width:2 * width]) + e[:, 2 * width:3 * width]


DN_UNROLL = 8


def _dn_kernel(q_ref, k_ref, v_ref, z_ref, small_ref, cwq_ref, cwk_ref, cwv_ref, alog_ref, dtb_ref, nw_ref,
               o_ref, gates_s, qs_s, k_s, v_s, beta_s, g_s, lhs_s, add_s, ebl_s, *, seq):
    h = pl.program_id(1)
    C = CHUNK
    D = DN_D
    n_chunks = seq // C

    @pl.when(h == 0)
    def _():
        small = small_ref[...]
        lane = lax.broadcasted_iota(jnp.int32, small.shape, 1)
        g_all = -jnp.exp(alog_ref[...]) * _softplus(small + dtb_ref[...])
        gates_s[...] = jnp.where(lane < S_A, jax.nn.sigmoid(small), g_all)

    row = lax.broadcasted_iota(jnp.int32, (seq, D), 0)
    qs_s[...] = _l2norm(_conv_silu(q_ref[...], cwq_ref[...], row)) * (D ** -0.5)
    k_s[...] = _l2norm(_conv_silu(k_ref[...], cwk_ref[...], row))
    v_s[...] = _conv_silu(v_ref[...], cwv_ref[...], row)
    sr = lax.broadcasted_iota(jnp.int32, (3 * LANES, 2 * LANES), 0) % LANES
    sc = lax.broadcasted_iota(jnp.int32, (3 * LANES, 2 * LANES), 1)
    sel = (sr == jnp.where(sc < LANES, h + S_B, h + S_A)).astype(BF16)
    rep = _dot(jnp.concatenate(_split3(gates_s[...]), axis=1), sel)
    beta_s[...] = rep[:, :LANES]
    g_s[...] = rep[:, LANES:]

    ri = lax.broadcasted_iota(jnp.int32, (C, C), 0)
    ci = lax.broadcasted_iota(jnp.int32, (C, C), 1)
    tril = ri >= ci
    strict = ri > ci
    l_bf = tril.astype(BF16)
    eye = (ri == ci).astype(F32)
    rl = lax.broadcasted_iota(jnp.int32, (C, LANES), 0)
    cl = lax.broadcasted_iota(jnp.int32, (C, LANES), 1)
    sl_pad = (rl > cl).astype(F32)

    def body_a(g, carry):
        U = range(DN_UNROLL)
        ns = [g * DN_UNROLL + u for u in U]
        sls = [pl.ds(pl.multiple_of(n * C, C), C) for n in ns]
        qc = [qs_s[s, :] for s in sls]
        kc = [k_s[s, :] for s in sls]
        vc = [v_s[s, :] for s in sls]
        bt = [beta_s[s, :] for s in sls]
        gc = [g_s[s, :] for s in sls]
        e = [_dot(l_bf, jnp.concatenate(_split3(gc[u]) + _split3(gc[u] * sl_pad), axis=1)) for u in U]
        kb = [kc[u] * bt[u] for u in U]
        gram = [_dot_nt(jnp.concatenate([kb[u], qc[u]], axis=0).astype(BF16), kc[u].astype(BF16)) for u in U]
        bc = [_sum3(e[u][:, :3 * LANES], LANES) for u in U]
        dm = [_sum3(e[u][:, 3 * LANES:], LANES)[:, :C] for u in U]
        decay = [jnp.where(tril, jnp.exp(dm[u]), 0.0) for u in U]
        eb = [jnp.exp(bc[u]) for u in U]
        a = [jnp.where(strict, gram[u][:C] * decay[u], 0.0) for u in U]
        att = [(gram[u][C:] * decay[u]).astype(BF16) for u in U]
        p = [eye - a[u] for u in U]
        a_bf = [a[u].astype(BF16) for u in U]
        x = [_dot(a_bf[u], a_bf[u]) for u in U]
        for _ in range(4):
            px = [_dot(jnp.concatenate([p[u], x[u]], axis=0).astype(BF16), x[u].astype(BF16)) for u in U]
            p = [p[u] + px[u][:C] for u in U]
            x = [px[u][C:] for u in U]
        px = [_dot(p[u].astype(BF16), x[u].astype(BF16)) for u in U]
        p = [(p[u] + px[u]).astype(BF16) for u in U]
        rhs = [jnp.concatenate([kb[u] * eb[u], vc[u] * bt[u]], axis=1).astype(BF16) for u in U]
        wu = [_dot(p[u], rhs[u]).astype(BF16) for u in U]
        kd = [(kc[u] * jnp.exp(bc[u][C - 1:C, :] - bc[u])).astype(BF16) for u in U]
        aw = [_dot(att[u], wu[u]) for u in U]
        mn = [_dot_tn(kd[u], wu[u]) for u in U]
        for u in U:
            n = ns[u]
            lhs_s[n, 0:D, :] = mn[u][:, :D].astype(BF16)
            lhs_s[n, D:D + C, :] = (qc[u] * eb[u] - aw[u][:, :D]).astype(BF16)
            add_s[n, 0:D, :] = mn[u][:, D:]
            add_s[n, D:D + C, :] = aw[u][:, D:]
            ebl_s[n] = jnp.broadcast_to(eb[u][C - 1:C, :], (8, LANES))
        return carry

    lax.fori_loop(0, n_chunks // DN_UNROLL, body_a, 0)

    nw = nw_ref[...]

    def finish(n, o):
        sl = pl.ds(pl.multiple_of(n * C, C), C)
        y = o * lax.rsqrt(jnp.mean(o * o, axis=-1, keepdims=True) + EPS) * nw * _silu(z_ref[sl, :])
        o_ref[sl, :] = y.astype(o_ref.dtype)

    def body_b(n, carry):
        S, o_prev = carry
        finish(jnp.maximum(n - 1, 0), o_prev)
        ps = _dot(lhs_s[n], S.astype(BF16))
        add = add_s[n]
        o = ps[D:] + add[D:]
        S = ebl_s[n][0:1, :] * S + (add[:D] - ps[:D])
        return S, o

    _, o_last = lax.fori_loop(0, n_chunks, body_b, (jnp.zeros((D, D), F32), jnp.zeros((C, D), F32)))
    finish(n_chunks - 1, o_last)


def _deltanet(proj, conv_w, a_log, dt_bias, norm_w, batch, seq):
    M = proj.shape[0]
    H = DN_HEADS
    n_chunks = seq // CHUNK
    assert seq % (CHUNK * DN_UNROLL) == 0
    blk = lambda c0: pl.BlockSpec((seq, DN_D), lambda b, h: (b, c0 // DN_D + h))
    cw = lambda c0: pl.BlockSpec((DN_CONV, DN_D), lambda b, h: (0, c0 // DN_D + h))
    row = pl.BlockSpec((1, LANES), lambda b, h: (0, 0))
    alog = jnp.zeros((1, LANES), F32).at[0, S_A:S_A + H].set(a_log)
    dtb = jnp.zeros((1, LANES), F32).at[0, S_A:S_A + H].set(dt_bias)
    seq_f32 = pltpu.VMEM((seq, DN_D), F32)
    return pl.pallas_call(
        functools.partial(_dn_kernel, seq=seq),
        grid=(batch, H),
        in_specs=[blk(C_DQ), blk(C_DK), blk(C_DV), blk(C_DZ),
                  pl.BlockSpec((seq, LANES), lambda b, h: (b, C_DGATE // LANES)),
                  cw(0), cw(2048), cw(4096), row, row, row],
        out_specs=pl.BlockSpec((seq, DN_D), lambda b, h: (b, h)),
        out_shape=jax.ShapeDtypeStruct((M, H * DN_D), BF16),
        scratch_shapes=[seq_f32] * 6 + [pltpu.VMEM((n_chunks, DN_D + CHUNK, DN_D), BF16),
                                        pltpu.VMEM((n_chunks, DN_D + CHUNK, DN_D), F32),
                                        pltpu.VMEM((n_chunks, 8, LANES), F32)],
        compiler_params=_cparams(("parallel", "arbitrary")),
        name="deltanet",
    )(proj, proj, proj, proj, proj, conv_w, conv_w, conv_w, alog, dtb, norm_w.reshape(1, DN_D))


GLA_LEVELS = (32, 16, 8, 4, 2, 1)
GLA_UNROLL = 4


def _gla_constants():
    C = CHUNK
    i = np.arange(C)[:, None]
    m = np.arange(C)[None, :]
    blocks = [(m <= i), (m > i)]
    upper_rows, masks = [], []
    for s in GLA_LEVELS:
        p = (i // (2 * s)) * (2 * s)
        m0 = p + s - 1
        up = (i - p) >= s
        blocks.append(np.where(up, (m > m0) & (m <= i), (m > i) & (m <= m0)))
        upper_rows.append(np.broadcast_to(up, (C, LANES)))
        j = m
        pj = (j // (2 * s)) * (2 * s)
        masks.append(up & ((j - pj) < s) & (p == pj))
    masks.append(i == m)
    mcat = np.concatenate(blocks, axis=0).astype(np.float32)
    upper = np.stack(upper_rows).astype(np.float32)
    mask = np.stack(masks).astype(np.float32)
    return mcat, upper, mask


N_QK_TILES = GLA_DK // LANES + 1
N_VR_TILES = GLA_DV // LANES + 1


def _gla_kernel(*refs, seq):
    q_t, refs = refs[:N_QK_TILES], refs[N_QK_TILES:]
    k_t, refs = refs[:N_QK_TILES], refs[N_QK_TILES:]
    v_t, refs = refs[:N_VR_TILES], refs[N_VR_TILES:]
    r_t, refs = refs[:N_VR_TILES], refs[N_VR_TILES:]
    lr_ref, wa_ref, ba_ref, nw_ref, mcat_ref, upper_ref, mask_ref, o_ref, gk_s, st_s = refs
    C = CHUNK

    def rows(tile_refs, s):
        return jnp.concatenate(_lane_shift([t[s, :] for t in tile_refs], GLA_SHIFT), axis=1)

    x_alpha = _dot(lr_ref[...], wa_ref[...], HI) + ba_ref[...]
    gk_s[...] = -_softplus(-x_alpha) / GLA_TAU
    st_s[...] = jnp.zeros_like(st_s)
    mcat = mcat_ref[...]
    nw = nw_ref[...]
    n_lvl = len(GLA_LEVELS)

    def body(g, carry):
        U = range(GLA_UNROLL)
        sls = [pl.ds(pl.multiple_of((g * GLA_UNROLL + u) * C, C), C) for u in U]
        qc = [rows(q_t, s) * (GLA_DK ** -0.5) for s in sls]
        kc = [rows(k_t, s) for s in sls]
        v_bf = [rows(v_t, s).astype(BF16) for s in sls]
        e = [_dot(mcat, jnp.concatenate(_split3(gk_s[s, :]), axis=1)) for s in sls]
        ex = [jnp.exp(_sum3(e[u], GLA_DK)) for u in U]
        k_bf = [kc[u].astype(BF16) for u in U]
        att = [mask_ref[n_lvl] * _dot_nt(qc[u].astype(BF16), k_bf[u]) for u in U]
        for l in range(n_lvl):
            up = upper_ref[l] > 0.5
            zl = [(jnp.where(up, qc[u], kc[u]) * ex[u][(2 + l) * C:(3 + l) * C, :]).astype(BF16) for u in U]
            gl = [_dot_nt(zl[u], zl[u]) for u in U]
            att = [att[u] + mask_ref[l] * gl[u] for u in U]
        ds = [_dot_tn(v_bf[u], (kc[u] * ex[u][C:2 * C, :]).astype(BF16)) for u in U]
        ov = [_dot(att[u].astype(BF16), v_bf[u]) for u in U]
        st = st_s[...]
        for u in U:
            eb = ex[u][0:C, :]
            o = _dot_nt((qc[u] * eb).astype(BF16), st.astype(BF16)) + ov[u]
            st = st * eb[C - 1:C, :] + ds[u]
            y = o * lax.rsqrt(jnp.mean(o * o, axis=-1, keepdims=True) + EPS) * nw * _silu(rows(r_t, sls[u]))
            o_ref[sls[u], :] = y.astype(o_ref.dtype)
        st_s[...] = st
        return carry

    lax.fori_loop(0, seq // (C * GLA_UNROLL), body, 0)


def _gla(proj, w_alpha2, b_alpha, norm_w, batch, seq):
    M = proj.shape[0]
    H = GLA_HEADS
    assert seq % (CHUNK * GLA_UNROLL) == 0
    mcat, upper, mask = _gla_constants()
    wa = jnp.zeros((LANES, H * GLA_DK), F32).at[S_LR:S_LR + GLA_RANK, :].set(w_alpha2)
    full = lambda a: pl.BlockSpec(a.shape, lambda b, h: (0,) * a.ndim)

    def tiles(c0, width, n):
        return [pl.BlockSpec((seq, LANES), lambda b, h, t=t: (b, c0 // LANES + h * (width // LANES) + t))
                for t in range(n)]

    n_in = 2 * N_QK_TILES + 2 * N_VR_TILES
    return pl.pallas_call(
        functools.partial(_gla_kernel, seq=seq),
        grid=(batch, H),
        in_specs=[*tiles(C_GQ, GLA_DK, N_QK_TILES), *tiles(C_GK, GLA_DK, N_QK_TILES),
                  *tiles(C_GV, GLA_DV, N_VR_TILES), *tiles(C_GR, GLA_DV, N_VR_TILES),
                  pl.BlockSpec((seq, LANES), lambda b, h: (b, C_LR // LANES)),
                  pl.BlockSpec((LANES, GLA_DK), lambda b, h: (0, h)),
                  pl.BlockSpec((1, GLA_DK), lambda b, h: (0, h)),
                  pl.BlockSpec((1, GLA_DV), lambda b, h: (0, 0)),
                  full(mcat), full(upper), full(mask)],
        out_specs=pl.BlockSpec((seq, GLA_DV), lambda b, h: (b, h)),
        out_shape=jax.ShapeDtypeStruct((M, H * GLA_DV), BF16),
        scratch_shapes=[pltpu.VMEM((seq, GLA_DK), F32), pltpu.VMEM((GLA_DV, GLA_DK), F32)],
        compiler_params=_cparams(("parallel", "parallel")),
        name="gla",
    )(*([proj] * (n_in + 1)), wa, b_alpha.reshape(1, -1), norm_w.reshape(1, GLA_DV),
      jnp.asarray(mcat, BF16), jnp.asarray(upper), jnp.asarray(mask))


def _merge_kernel(odn_ref, ogla_ref, pa_ref, pb_ref, gd_ref, gdn_ref, gg_ref, ggn_ref, o_ref):
    ya = _dot(odn_ref[...], pa_ref[...].astype(BF16))
    yb = _dot(ogla_ref[...], pb_ref[...].astype(BF16))

    def gate(main_ref, next_ref):
        n = main_ref.shape[1] // LANES
        tiles = [main_ref[:, t * LANES:(t + 1) * LANES] for t in range(n)] + [next_ref[...]]
        return jax.nn.sigmoid(jnp.concatenate(_lane_shift(tiles, GATE_SHIFT), axis=1))

    o_ref[...] = (gate(gd_ref, gdn_ref) * ya + gate(gg_ref, ggn_ref) * yb).astype(o_ref.dtype)


def _merge(o_dn, o_gla, pa, pb, proj, tm=1024, tn=512):
    M, K = o_dn.shape
    N = pa.shape[1]
    tm = min(tm, M)
    lt = tn // LANES
    main = lambda c0: pl.BlockSpec((tm, tn), lambda i, j: (i, c0 // tn + j))
    nxt = lambda c0: pl.BlockSpec((tm, LANES), lambda i, j: (i, c0 // LANES + (j + 1) * lt))
    assert (C_GATE_DN - GATE_SHIFT) % tn == 0 and (C_GATE_GLA - GATE_SHIFT) % tn == 0
    return pl.pallas_call(
        _merge_kernel,
        grid=(M // tm, N // tn),
        in_specs=[pl.BlockSpec((tm, K), lambda i, j: (i, 0)),
                  pl.BlockSpec((tm, K), lambda i, j: (i, 0)),
                  pl.BlockSpec((K, tn), lambda i, j: (0, j)),
                  pl.BlockSpec((K, tn), lambda i, j: (0, j)),
                  main(C_GATE_DN), nxt(C_GATE_DN), main(C_GATE_GLA), nxt(C_GATE_GLA)],
        out_specs=pl.BlockSpec((tm, tn), lambda i, j: (i, j)),
        out_shape=jax.ShapeDtypeStruct((M, N), BF16),
        compiler_params=_cparams(("parallel", "arbitrary")),
        name="merge",
    )(o_dn, o_gla, pa, pb, proj, proj, proj, proj)


def _mm_res_kernel(a_ref, b_ref, r_ref, o_ref):
    o_ref[...] = r_ref[...] + _dot(a_ref[...], b_ref[...].astype(BF16))


def _matmul_residual(a, b, res, tm=1024, tn=512, name="outproj"):
    M, K = a.shape
    N = b.shape[1]
    tm = min(tm, M)
    return pl.pallas_call(
        _mm_res_kernel,
        grid=(M // tm, N // tn),
        in_specs=[pl.BlockSpec((tm, K), lambda i, j: (i, 0)),
                  pl.BlockSpec((K, tn), lambda i, j: (0, j)),
                  pl.BlockSpec((tm, tn), lambda i, j: (i, j))],
        out_specs=pl.BlockSpec((tm, tn), lambda i, j: (i, j)),
        out_shape=jax.ShapeDtypeStruct((M, N), F32),
        compiler_params=_cparams(("parallel", "arbitrary")),
        name=name,
    )(a, b, res)


def _ffn_in_kernel(h_ref, wg_ref, wu_ref, cw_ref, cb_ref, o_ref):
    h = h_ref[...]
    gate = _dot(h, wg_ref[...].astype(BF16))
    up = _dot(h, wu_ref[...].astype(BF16))
    row = lax.broadcasted_iota(jnp.int32, gate.shape, 0)
    cw = cw_ref[...]
    y = gate * cw[FFN_CONV - 1:FFN_CONV, :] + cb_ref[...]
    for s in range(1, FFN_CONV):
        y = y + jnp.where(row >= s, pltpu.roll(gate, s, 0), 0.0) * cw[FFN_CONV - 1 - s:FFN_CONV - s, :]
    o_ref[...] = (_silu(y) * up).astype(o_ref.dtype)


def _ffn_in(h, w_in, conv_w, conv_b, seq, tn=256):
    M, K = h.shape
    N = w_in.shape[1] // 2
    nb = N // tn
    return pl.pallas_call(
        _ffn_in_kernel,
        grid=(M // seq, nb),
        in_specs=[pl.BlockSpec((seq, K), lambda i, j: (i, 0), pipeline_mode=pl.Buffered(1)),
                  pl.BlockSpec((K, tn), lambda i, j: (0, j)),
                  pl.BlockSpec((K, tn), lambda i, j: (0, nb + j)),
                  pl.BlockSpec((FFN_CONV, tn), lambda i, j: (0, j)),
                  pl.BlockSpec((1, tn), lambda i, j: (0, j))],
        out_specs=pl.BlockSpec((seq, tn), lambda i, j: (i, j)),
        out_shape=jax.ShapeDtypeStruct((M, N), BF16),
        compiler_params=_cparams(("parallel", "arbitrary")),
        name="ffn_in",
    )(h, w_in, w_in, conv_w, conv_b.reshape(1, N))


def _ffn_out_kernel(a_ref, b_ref, r_ref, o_ref, acc_ref):
    kk = pl.program_id(2)

    @pl.when(kk == 0)
    def _():
        acc_ref[...] = r_ref[...]

    acc_ref[...] += _dot(a_ref[...], b_ref[...])

    @pl.when(kk == pl.num_programs(2) - 1)
    def _():
        o_ref[...] = acc_ref[...]


def _ffn_out(a, b, res, tm=1024, tn=512, ksplit=2):
    M, K = a.shape
    N = b.shape[1]
    tm = min(tm, M)
    tk = K // ksplit
    return pl.pallas_call(
        _ffn_out_kernel,
        grid=(M // tm, N // tn, ksplit),
        in_specs=[pl.BlockSpec((tm, tk), lambda i, j, k: (i, k)),
                  pl.BlockSpec((tk, tn), lambda i, j, k: (k, j)),
                  pl.BlockSpec((tm, tn), lambda i, j, k: (i, j))],
        out_specs=pl.BlockSpec((tm, tn), lambda i, j, k: (i, j)),
        out_shape=jax.ShapeDtypeStruct((M, N), F32),
        scratch_shapes=[pltpu.VMEM((tm, tn), F32)],
        compiler_params=_cparams(("parallel", "arbitrary", "arbitrary")),
        name="ffn_out",
    )(a, b, res)


def _layer(x2d, batch, seq, norm_mix_w, w_in, dn_conv_w, dn_a_log, dn_dt_bias, dn_norm_w,
           gla_w_alpha2, gla_b_alpha, gla_norm_w, w_branch_dn, w_branch_gla, w_out,
           norm_ffn_w, w_ffn_in, ffn_conv_w, ffn_conv_b, w_ffn_out):
    h = _rmsnorm(x2d, norm_mix_w, BF16)
    proj = _inproj(h, w_in)
    o_dn = _deltanet(proj, dn_conv_w, dn_a_log, dn_dt_bias, dn_norm_w, batch, seq)
    o_gla = _gla(proj, gla_w_alpha2, gla_b_alpha, gla_norm_w, batch, seq)
    merged = _merge(o_dn, o_gla, w_branch_dn, w_branch_gla, proj)
    x1 = _matmul_residual(merged, w_out, x2d)

    h2 = _rmsnorm(x1, norm_ffn_w, BF16)
    act = _ffn_in(h2, w_ffn_in, ffn_conv_w, ffn_conv_b, seq)
    return _ffn_out(act, w_ffn_out.astype(BF16), x1)


def kernel(x, norm_mix_w, w_in, dn_conv_w, dn_a_log, dn_dt_bias, dn_norm_w, gla_w_alpha2, gla_b_alpha, gla_norm_w, w_branch_dn, w_branch_gla, w_out, norm_ffn_w, w_ffn_in, ffn_conv_w, ffn_conv_b, w_ffn_out, norm_final_w):
    batch, seq, D = x.shape
    x2d = x.reshape(batch * seq, D)
    for l in range(norm_mix_w.shape[0]):
        x2d = _layer(x2d, batch, seq, norm_mix_w[l], w_in[l], dn_conv_w[l], dn_a_log[l], dn_dt_bias[l],
                     dn_norm_w[l], gla_w_alpha2[l], gla_b_alpha[l], gla_norm_w[l], w_branch_dn[l],
                     w_branch_gla[l], w_out[l], norm_ffn_w[l], w_ffn_in[l], ffn_conv_w[l], ffn_conv_b[l],
                     w_ffn_out[l])
    return _rmsnorm(x2d, norm_final_w, F32).reshape(batch, seq, D)
```

```python
import functools

import numpy as np
import jax
import jax.numpy as jnp
from jax import lax
from jax.experimental import pallas as pl
from jax.experimental.pallas import tpu as pltpu

F32 = jnp.float32
BF16 = jnp.bfloat16
HI = lax.Precision.HIGHEST

EPS = 1e-6
CHUNK = 64
DN_HEADS = 16
DN_D = 128
DN_CONV = 4
GLA_HEADS = 8
GLA_DK = 128
GLA_DV = 256
GLA_RANK = 16
GLA_TAU = 16.0
FFN_CONV = 3
LANES = 128
HALO = 8

C_DQ, C_DK, C_DV, C_DZ = 0, 2048, 4096, 6144
C_DGATE = 8192
C_GQ, C_GK, C_GV, C_GR = 8224, 9248, 10272, 12320
C_LR = 14368
C_GATE_DN, C_GATE_GLA = 14384, 18480
S_B, S_A = 0, 16
S_LR = C_LR % LANES
GLA_SHIFT = C_GQ % LANES
GATE_SHIFT = C_GATE_DN % LANES
assert C_GK % LANES == C_GV % LANES == C_GR % LANES == GLA_SHIFT and C_GATE_GLA % LANES == GATE_SHIFT

VMEM_LIMIT = 56 * 1024 * 1024


def _cparams(sem):
    return pltpu.CompilerParams(dimension_semantics=sem, vmem_limit_bytes=VMEM_LIMIT)


def _silu(x):
    return x * jax.nn.sigmoid(x)


def _softplus(x):
    return jnp.maximum(x, 0.0) + jnp.log1p(jnp.exp(-jnp.abs(x)))


def _dot(a, b, precision=None):
    return jnp.dot(a, b, precision=precision, preferred_element_type=F32)


def _dot_nt(a, b, precision=None):
    return lax.dot_general(a, b, (((1,), (1,)), ((), ())), precision=precision, preferred_element_type=F32)


def _dot_tn(a, b, precision=None):
    return lax.dot_general(a, b, (((0,), (0,)), ((), ())), precision=precision, preferred_element_type=F32)


def _rmsnorm_kernel(x_ref, w_ref, o_ref):
    x = x_ref[...]
    ms = jnp.mean(x * x, axis=-1, keepdims=True)
    o_ref[...] = (x * lax.rsqrt(ms + EPS) * w_ref[...]).astype(o_ref.dtype)


def _rmsnorm(x, w, out_dtype, tm=256):
    M, D = x.shape
    tm = min(tm, M)
    return pl.pallas_call(
        _rmsnorm_kernel,
        grid=(M // tm,),
        in_specs=[pl.BlockSpec((tm, D), lambda i: (i, 0)), pl.BlockSpec((1, D), lambda i: (0, 0))],
        out_specs=pl.BlockSpec((tm, D), lambda i: (i, 0)),
        out_shape=jax.ShapeDtypeStruct((M, D), out_dtype),
        compiler_params=_cparams(("parallel",)),
        name="rmsnorm",
    )(x, w.reshape(1, D))


def _inproj_kernel(a_ref, wt_ref, o_ref):
    o_ref[...] = _dot_nt(a_ref[...], wt_ref[...].astype(BF16))


def _inproj(a, wt, tm=2048, tn=512):
    M, K = a.shape
    N = wt.shape[0]
    tm = min(tm, M)
    return pl.pallas_call(
        _inproj_kernel,
        grid=(M // tm, pl.cdiv(N, tn)),
        in_specs=[pl.BlockSpec((tm, K), lambda i, j: (i, 0), pipeline_mode=pl.Buffered(1)),
                  pl.BlockSpec((tn, K), lambda i, j: (j, 0))],
        out_specs=pl.BlockSpec((tm, tn), lambda i, j: (i, j)),
        out_shape=jax.ShapeDtypeStruct((M, N), F32),
        compiler_params=_cparams(("parallel", "arbitrary")),
        name="inproj",
    )(a, wt)


def _lane_shift(tiles, shift):
    lane = lax.broadcasted_iota(jnp.int32, tiles[0].shape, 1)
    rolled = [pltpu.roll(t, LANES - shift, 1) for t in tiles]
    return [jnp.where(lane < LANES - shift, rolled[i], rolled[i + 1]) for i in range(len(tiles) - 1)]


def _conv_silu(x, cw, row):
    K = cw.shape[0]
    y = x * cw[K - 1:K, :]
    for s in range(1, K):
        xs = jnp.where(row >= s, pltpu.roll(x, s, 0), 0.0)
        y = y + xs * cw[K - 1 - s:K - s, :]
    return _silu(y)


def _l2norm(x):
    return x * lax.rsqrt(jnp.sum(x * x, axis=-1, keepdims=True) + EPS)


def _split3(x):
    hi = x.astype(BF16)
    r = x - hi.astype(F32)
    mid = r.astype(BF16)
    lo = (r - mid.astype(F32)).astype(BF16)
    return hi, mid, lo


def _split2(x):
    hi = x.astype(BF16)
    return hi, (x - hi.astype(F32)).astype(BF16)


DN_UNROLL = 8


def _dn_kernel(q_ref, k_ref, v_ref, z_ref, small_ref, cwq_ref, cwk_ref, cwv_ref, alog_ref, dtb_ref, nw_ref,
               o_ref, gates_s, qs_s, k_s, v_s, beta_s, bc_s, lhs_s, add_s, ebl_s, *, seq):
    h = pl.program_id(1)
    C = CHUNK
    D = DN_D
    U = range(DN_UNROLL)
    n_groups = seq // (C * DN_UNROLL)
    row = lax.broadcasted_iota(jnp.int32, (seq, D), 0)

    @pl.when(h == 0)
    def _():
        small = small_ref[...]
        lane = lax.broadcasted_iota(jnp.int32, small.shape, 1)
        x = -jnp.exp(alog_ref[...]) * _softplus(small + dtb_ref[...])
        in_chunk = row % C
        s = 1
        while s < C:
            x = x + jnp.where(in_chunk >= s, pltpu.roll(x, s, 0), 0.0)
            s *= 2
        gates_s[...] = jnp.where(lane < S_A, jax.nn.sigmoid(small), x)

    qs_s[...] = _l2norm(_conv_silu(q_ref[...], cwq_ref[...], row)) * (D ** -0.5)
    k_s[...] = _l2norm(_conv_silu(k_ref[...], cwk_ref[...], row))
    v_s[...] = _conv_silu(v_ref[...], cwv_ref[...], row)
    sr = lax.broadcasted_iota(jnp.int32, (3 * LANES, 2 * LANES), 0) % LANES
    sc = lax.broadcasted_iota(jnp.int32, (3 * LANES, 2 * LANES), 1)
    sel = (sr == jnp.where(sc < LANES, h + S_B, h + S_A)).astype(BF16)
    rep = _dot(jnp.concatenate(_split3(gates_s[...]), axis=1), sel)
    beta_s[...] = rep[:, :LANES]
    bc_s[...] = rep[:, LANES:]

    ri = lax.broadcasted_iota(jnp.int32, (C, C), 0)
    ci = lax.broadcasted_iota(jnp.int32, (C, C), 1)
    tril = ri >= ci
    strict = ri > ci
    eye = (ri == ci).astype(F32)
    nw = nw_ref[...]

    def pass_a(g):
        ns = [g * DN_UNROLL + u for u in U]
        sls = [pl.ds(pl.multiple_of(n * C, C), C) for n in ns]
        qc = [qs_s[s, :] for s in sls]
        kc = [k_s[s, :] for s in sls]
        vc = [v_s[s, :] for s in sls]
        bt = [beta_s[s, :] for s in sls]
        bc = [bc_s[s, :] for s in sls]
        kb = [kc[u] * bt[u] for u in U]
        gram = [_dot_nt(jnp.concatenate([kb[u], qc[u]], axis=0).astype(BF16), kc[u].astype(BF16)) for u in U]
        yield
        dm = [bc[u][:, :C] - jnp.transpose(bc[u])[:C, :] for u in U]
        decay = [jnp.where(tril, jnp.exp(jnp.minimum(dm[u], 0.0)), 0.0) for u in U]
        eb = [jnp.exp(bc[u]) for u in U]
        a = [jnp.where(strict, gram[u][:C] * decay[u], 0.0) for u in U]
        att = [(gram[u][C:] * decay[u]).astype(BF16) for u in U]
        p = [eye - a[u] for u in U]
        a_bf = [a[u].astype(BF16) for u in U]
        x = [_dot(a_bf[u], a_bf[u]) for u in U]
        yield
        for _ in range(4):
            px = [_dot(jnp.concatenate([p[u], x[u]], axis=0).astype(BF16), x[u].astype(BF16)) for u in U]
            p = [p[u] + px[u][:C] for u in U]
            x = [px[u][C:] for u in U]
            yield
        px = [_dot(p[u].astype(BF16), x[u].astype(BF16)) for u in U]
        p = [(p[u] + px[u]).astype(BF16) for u in U]
        yield
        rhs = [jnp.concatenate([kb[u] * eb[u], vc[u] * bt[u]], axis=1).astype(BF16) for u in U]
        wu = [_dot(p[u], rhs[u]).astype(BF16) for u in U]
        yield
        kd = [(kc[u] * jnp.exp(bc[u][C - 1:C, :] - bc[u])).astype(BF16) for u in U]
        aw = [_dot(att[u], wu[u]) for u in U]
        yield
        mn = [_dot_tn(kd[u], wu[u]) for u in U]
        for u in U:
            n = ns[u]
            lhs_s[n, 0:D, :] = mn[u][:, :D].astype(BF16)
            lhs_s[n, D:D + C, :] = (qc[u] * eb[u] - aw[u][:, :D]).astype(BF16)
            add_s[n, 0:D, :] = mn[u][:, D:]
            add_s[n, D:D + C, :] = aw[u][:, D:]
            ebl_s[n] = jnp.broadcast_to(eb[u][C - 1:C, :], (8, LANES))
        yield

    def finish(n, o):
        sl = pl.ds(pl.multiple_of(n * C, C), C)
        y = o * lax.rsqrt(jnp.mean(o * o, axis=-1, keepdims=True) + EPS) * nw * _silu(z_ref[sl, :])
        o_ref[sl, :] = y.astype(o_ref.dtype)

    def step_b(n, carry):
        S, o_prev = carry
        finish(jnp.maximum(n - 1, 0), o_prev)
        ps = _dot(lhs_s[n], S.astype(BF16))
        add = add_s[n]
        o = ps[D:] + add[D:]
        S = ebl_s[n][0:1, :] * S + (add[:D] - ps[:D])
        return S, o

    def interleave(g_a, g_b, carry):
        stages = pass_a(g_a) if g_a is not None else iter(())
        steps = [g_b * DN_UNROLL + u for u in U] if g_b is not None else []
        for _ in stages:
            if steps:
                carry = step_b(steps.pop(0), carry)
        for n in steps:
            carry = step_b(n, carry)
        return carry

    carry = (jnp.zeros((D, D), F32), jnp.zeros((C, D), F32))
    carry = interleave(0, None, carry)
    carry = lax.fori_loop(1, n_groups, lambda g, c: interleave(g, g - 1, c), carry)
    _, o_last = interleave(None, n_groups - 1, carry)
    finish(seq // C - 1, o_last)


def _deltanet(proj, conv_w, a_log, dt_bias, norm_w, batch, seq):
    M = proj.shape[0]
    H = DN_HEADS
    n_chunks = seq // CHUNK
    assert seq % (CHUNK * DN_UNROLL) == 0
    blk = lambda c0: pl.BlockSpec((seq, DN_D), lambda b, h: (b, c0 // DN_D + h))
    cw = lambda c0: pl.BlockSpec((DN_CONV, DN_D), lambda b, h: (0, c0 // DN_D + h))
    row = pl.BlockSpec((1, LANES), lambda b, h: (0, 0))
    alog = jnp.zeros((1, LANES), F32).at[0, S_A:S_A + H].set(a_log)
    dtb = jnp.zeros((1, LANES), F32).at[0, S_A:S_A + H].set(dt_bias)
    seq_f32 = pltpu.VMEM((seq, DN_D), F32)
    return pl.pallas_call(
        functools.partial(_dn_kernel, seq=seq),
        grid=(batch, H),
        in_specs=[blk(C_DQ), blk(C_DK), blk(C_DV), blk(C_DZ),
                  pl.BlockSpec((seq, LANES), lambda b, h: (b, C_DGATE // LANES)),
                  cw(0), cw(2048), cw(4096), row, row, row],
        out_specs=pl.BlockSpec((seq, DN_D), lambda b, h: (b, h)),
        out_shape=jax.ShapeDtypeStruct((M, H * DN_D), BF16),
        scratch_shapes=[seq_f32] * 6 + [pltpu.VMEM((n_chunks, DN_D + CHUNK, DN_D), BF16),
                                        pltpu.VMEM((n_chunks, DN_D + CHUNK, DN_D), F32),
                                        pltpu.VMEM((n_chunks, 8, LANES), F32)],
        compiler_params=_cparams(("parallel", "arbitrary")),
        name="deltanet",
    )(proj, proj, proj, proj, proj, conv_w, conv_w, conv_w, alog, dtb, norm_w.reshape(1, DN_D))


GLA_LEVELS = (32, 16, 8, 4, 2, 1)
GLA_UNROLL = 4


def _gla_constants():
    C = CHUNK
    i = np.arange(C)[:, None]
    m = np.arange(C)[None, :]
    blocks = [(m <= i), (m > i)]
    upper_rows, masks = [], []
    for s in GLA_LEVELS:
        p = (i // (2 * s)) * (2 * s)
        m0 = p + s - 1
        up = (i - p) >= s
        blocks.append(np.where(up, (m > m0) & (m <= i), (m > i) & (m <= m0)))
        upper_rows.append(np.broadcast_to(up, (C, LANES)))
        j = m
        pj = (j // (2 * s)) * (2 * s)
        masks.append(up & ((j - pj) < s) & (p == pj))
    masks.append(i == m)
    mcat = np.concatenate(blocks, axis=0).astype(np.float32)
    upper = np.stack(upper_rows).astype(np.float32)
    mask = np.stack(masks).astype(np.float32)
    return mcat, upper, mask


N_QK_TILES = GLA_DK // LANES + 1
N_VR_TILES = GLA_DV // LANES + 1


def _gla_kernel(*refs, seq):
    q_t, refs = refs[:N_QK_TILES], refs[N_QK_TILES:]
    k_t, refs = refs[:N_QK_TILES], refs[N_QK_TILES:]
    v_t, refs = refs[:N_VR_TILES], refs[N_VR_TILES:]
    r_t, refs = refs[:N_VR_TILES], refs[N_VR_TILES:]
    lr_ref, wa_ref, ba_ref, nw_ref, mcat_ref, upper_ref, mask_ref, o_ref, gk_s, st_s = refs
    C = CHUNK

    def rows(tile_refs, s):
        return jnp.concatenate(_lane_shift([t[s, :] for t in tile_refs], GLA_SHIFT), axis=1)

    x_alpha = _dot(lr_ref[...], wa_ref[...], HI) + ba_ref[...]
    gk_s[...] = -_softplus(-x_alpha) / GLA_TAU
    st_s[...] = jnp.zeros_like(st_s)
    mcat = mcat_ref[...]
    nw = nw_ref[...]
    n_lvl = len(GLA_LEVELS)

    def body(g, carry):
        U = range(GLA_UNROLL)
        sls = [pl.ds(pl.multiple_of((g * GLA_UNROLL + u) * C, C), C) for u in U]
        qc = [rows(q_t, s) * (GLA_DK ** -0.5) for s in sls]
        kc = [rows(k_t, s) for s in sls]
        v_bf = [rows(v_t, s).astype(BF16) for s in sls]
        e = [_dot(mcat, jnp.concatenate(_split2(gk_s[s, :]), axis=1)) for s in sls]
        ex = [jnp.exp(e[u][:, :GLA_DK] + e[u][:, GLA_DK:]) for u in U]
        k_bf = [kc[u].astype(BF16) for u in U]
        att = [mask_ref[n_lvl] * _dot_nt(qc[u].astype(BF16), k_bf[u]) for u in U]
        for l in range(n_lvl):
            up = upper_ref[l] > 0.5
            zl = [(jnp.where(up, qc[u], kc[u]) * ex[u][(2 + l) * C:(3 + l) * C, :]).astype(BF16) for u in U]
            gl = [_dot_nt(zl[u], zl[u]) for u in U]
            att = [att[u] + mask_ref[l] * gl[u] for u in U]
        ds = [_dot_tn(v_bf[u], (kc[u] * ex[u][C:2 * C, :]).astype(BF16)) for u in U]
        ov = [_dot(att[u].astype(BF16), v_bf[u]) for u in U]
        st = st_s[...]
        for u in U:
            eb = ex[u][0:C, :]
            o = _dot_nt((qc[u] * eb).astype(BF16), st.astype(BF16)) + ov[u]
            st = st * eb[C - 1:C, :] + ds[u]
            y = o * lax.rsqrt(jnp.mean(o * o, axis=-1, keepdims=True) + EPS) * nw * _silu(rows(r_t, sls[u]))
            o_ref[sls[u], :] = y.astype(o_ref.dtype)
        st_s[...] = st
        return carry

    lax.fori_loop(0, seq // (C * GLA_UNROLL), body, 0)


def _gla(proj, w_alpha2, b_alpha, norm_w, batch, seq):
    M = proj.shape[0]
    H = GLA_HEADS
    assert seq % (CHUNK * GLA_UNROLL) == 0
    mcat, upper, mask = _gla_constants()
    wa = jnp.zeros((LANES, H * GLA_DK), F32).at[S_LR:S_LR + GLA_RANK, :].set(w_alpha2)
    full = lambda a: pl.BlockSpec(a.shape, lambda b, h: (0,) * a.ndim)

    def tiles(c0, width, n):
        return [pl.BlockSpec((seq, LANES), lambda b, h, t=t: (b, c0 // LANES + h * (width // LANES) + t))
                for t in range(n)]

    n_in = 2 * N_QK_TILES + 2 * N_VR_TILES
    return pl.pallas_call(
        functools.partial(_gla_kernel, seq=seq),
        grid=(batch, H),
        in_specs=[*tiles(C_GQ, GLA_DK, N_QK_TILES), *tiles(C_GK, GLA_DK, N_QK_TILES),
                  *tiles(C_GV, GLA_DV, N_VR_TILES), *tiles(C_GR, GLA_DV, N_VR_TILES),
                  pl.BlockSpec((seq, LANES), lambda b, h: (b, C_LR // LANES)),
                  pl.BlockSpec((LANES, GLA_DK), lambda b, h: (0, h)),
                  pl.BlockSpec((1, GLA_DK), lambda b, h: (0, h)),
                  pl.BlockSpec((1, GLA_DV), lambda b, h: (0, 0)),
                  full(mcat), full(upper), full(mask)],
        out_specs=pl.BlockSpec((seq, GLA_DV), lambda b, h: (b, h)),
        out_shape=jax.ShapeDtypeStruct((M, H * GLA_DV), BF16),
        scratch_shapes=[pltpu.VMEM((seq, GLA_DK), F32), pltpu.VMEM((GLA_DV, GLA_DK), F32)],
        compiler_params=_cparams(("parallel", "parallel")),
        name="gla",
    )(*([proj] * (n_in + 1)), wa, b_alpha.reshape(1, -1), norm_w.reshape(1, GLA_DV),
      jnp.asarray(mcat, BF16), jnp.asarray(upper), jnp.asarray(mask))


def _merge_kernel(odn_ref, ogla_ref, pa_ref, pb_ref, gd_ref, gdn_ref, gg_ref, ggn_ref, o_ref):
    ya = _dot(odn_ref[...], pa_ref[...].astype(BF16))
    yb = _dot(ogla_ref[...], pb_ref[...].astype(BF16))

    def gate(main_ref, next_ref):
        n = main_ref.shape[1] // LANES
        tiles = [main_ref[:, t * LANES:(t + 1) * LANES] for t in range(n)] + [next_ref[...]]
        return jax.nn.sigmoid(jnp.concatenate(_lane_shift(tiles, GATE_SHIFT), axis=1))

    o_ref[...] = (gate(gd_ref, gdn_ref) * ya + gate(gg_ref, ggn_ref) * yb).astype(o_ref.dtype)


def _merge(o_dn, o_gla, pa, pb, proj, tm=1024, tn=512):
    M, K = o_dn.shape
    N = pa.shape[1]
    tm = min(tm, M)
    lt = tn // LANES
    main = lambda c0: pl.BlockSpec((tm, tn), lambda i, j: (i, c0 // tn + j))
    nxt = lambda c0: pl.BlockSpec((tm, LANES), lambda i, j: (i, c0 // LANES + (j + 1) * lt))
    assert (C_GATE_DN - GATE_SHIFT) % tn == 0 and (C_GATE_GLA - GATE_SHIFT) % tn == 0
    return pl.pallas_call(
        _merge_kernel,
        grid=(M // tm, N // tn),
        in_specs=[pl.BlockSpec((tm, K), lambda i, j: (i, 0)),
                  pl.BlockSpec((tm, K), lambda i, j: (i, 0)),
                  pl.BlockSpec((K, tn), lambda i, j: (0, j)),
                  pl.BlockSpec((K, tn), lambda i, j: (0, j)),
                  main(C_GATE_DN), nxt(C_GATE_DN), main(C_GATE_GLA), nxt(C_GATE_GLA)],
        out_specs=pl.BlockSpec((tm, tn), lambda i, j: (i, j)),
        out_shape=jax.ShapeDtypeStruct((M, N), BF16),
        compiler_params=_cparams(("parallel", "arbitrary")),
        name="merge",
    )(o_dn, o_gla, pa, pb, proj, proj, proj, proj)


def _mm_res_kernel(a_ref, b_ref, r_ref, o_ref):
    o_ref[...] = r_ref[...] + _dot(a_ref[...], b_ref[...].astype(BF16))


def _matmul_residual(a, b, res, tm=1024, tn=512, name="outproj"):
    M, K = a.shape
    N = b.shape[1]
    tm = min(tm, M)
    return pl.pallas_call(
        _mm_res_kernel,
        grid=(M // tm, N // tn),
        in_specs=[pl.BlockSpec((tm, K), lambda i, j: (i, 0)),
                  pl.BlockSpec((K, tn), lambda i, j: (0, j)),
                  pl.BlockSpec((tm, tn), lambda i, j: (i, j))],
        out_specs=pl.BlockSpec((tm, tn), lambda i, j: (i, j)),
        out_shape=jax.ShapeDtypeStruct((M, N), F32),
        compiler_params=_cparams(("parallel", "arbitrary")),
        name=name,
    )(a, b, res)


FFN_ROW_BLOCKS = 4


def _ffn_in_kernel(h_ref, wg_ref, wu_ref, cw_ref, cb_ref, o_ref):
    wg = wg_ref[...].astype(BF16)
    wu = wu_ref[...].astype(BF16)
    cw = cw_ref[...]
    cb = cb_ref[...]
    rb = h_ref.shape[0] // FFN_ROW_BLOCKS
    tail = jnp.zeros((HALO, wg.shape[1]), F32)
    for r in range(FFN_ROW_BLOCKS):
        h = h_ref[r * rb:(r + 1) * rb, :]
        gate = _dot(h, wg)
        up = _dot(h, wu)
        ext = jnp.concatenate([tail, gate], axis=0)
        y = gate * cw[FFN_CONV - 1:FFN_CONV, :] + cb
        for s in range(1, FFN_CONV):
            y = y + pltpu.roll(ext, s, 0)[HALO:, :] * cw[FFN_CONV - 1 - s:FFN_CONV - s, :]
        o_ref[r * rb:(r + 1) * rb, :] = (_silu(y) * up).astype(o_ref.dtype)
        tail = gate[rb - HALO:, :]


def _ffn_in(h, w_in, conv_w, conv_b, seq, tn=256):
    M, K = h.shape
    N = w_in.shape[1] // 2
    nb = N // tn
    assert seq % (FFN_ROW_BLOCKS * HALO) == 0
    return pl.pallas_call(
        _ffn_in_kernel,
        grid=(M // seq, nb),
        in_specs=[pl.BlockSpec((seq, K), lambda i, j: (i, 0), pipeline_mode=pl.Buffered(1)),
                  pl.BlockSpec((K, tn), lambda i, j: (0, j)),
                  pl.BlockSpec((K, tn), lambda i, j: (0, nb + j)),
                  pl.BlockSpec((FFN_CONV, tn), lambda i, j: (0, j)),
                  pl.BlockSpec((1, tn), lambda i, j: (0, j))],
        out_specs=pl.BlockSpec((seq, tn), lambda i, j: (i, j)),
        out_shape=jax.ShapeDtypeStruct((M, N), BF16),
        compiler_params=_cparams(("parallel", "arbitrary")),
        name="ffn_in",
    )(h, w_in, w_in, conv_w, conv_b.reshape(1, N))


def _ffn_out_kernel(a_ref, b_ref, r_ref, o_ref, acc_ref):
    kk = pl.program_id(2)

    @pl.when(kk == 0)
    def _():
        acc_ref[...] = r_ref[...]

    acc_ref[...] += _dot(a_ref[...], b_ref[...])

    @pl.when(kk == pl.num_programs(2) - 1)
    def _():
        o_ref[...] = acc_ref[...]


def _ffn_out(a, b, res, tm=1024, tn=512, ksplit=2):
    M, K = a.shape
    N = b.shape[1]
    tm = min(tm, M)
    tk = K // ksplit
    return pl.pallas_call(
        _ffn_out_kernel,
        grid=(M // tm, N // tn, ksplit),
        in_specs=[pl.BlockSpec((tm, tk), lambda i, j, k: (i, k)),
                  pl.BlockSpec((tk, tn), lambda i, j, k: (k, j)),
                  pl.BlockSpec((tm, tn), lambda i, j, k: (i, j))],
        out_specs=pl.BlockSpec((tm, tn), lambda i, j, k: (i, j)),
        out_shape=jax.ShapeDtypeStruct((M, N), F32),
        scratch_shapes=[pltpu.VMEM((tm, tn), F32)],
        compiler_params=_cparams(("parallel", "arbitrary", "arbitrary")),
        name="ffn_out",
    )(a, b, res)


def _layer(x2d, batch, seq, norm_mix_w, w_in, dn_conv_w, dn_a_log, dn_dt_bias, dn_norm_w,
           gla_w_alpha2, gla_b_alpha, gla_norm_w, w_branch_dn, w_branch_gla, w_out,
           norm_ffn_w, w_ffn_in, ffn_conv_w, ffn_conv_b, w_ffn_out):
    h = _rmsnorm(x2d, norm_mix_w, BF16)
    proj = _inproj(h, w_in.T)
    o_dn = _deltanet(proj, dn_conv_w, dn_a_log, dn_dt_bias, dn_norm_w, batch, seq)
    o_gla = _gla(proj, gla_w_alpha2, gla_b_alpha, gla_norm_w, batch, seq)
    merged = _merge(o_dn, o_gla, w_branch_dn, w_branch_gla, proj)
    x1 = _matmul_residual(merged, w_out, x2d)

    h2 = _rmsnorm(x1, norm_ffn_w, BF16)
    act = _ffn_in(h2, w_ffn_in, ffn_conv_w, ffn_conv_b, seq)
    return _ffn_out(act, w_ffn_out.astype(BF16), x1)


def kernel(x, norm_mix_w, w_in, dn_conv_w, dn_a_log, dn_dt_bias, dn_norm_w, gla_w_alpha2, gla_b_alpha, gla_norm_w, w_branch_dn, w_branch_gla, w_out, norm_ffn_w, w_ffn_in, ffn_conv_w, ffn_conv_b, w_ffn_out, norm_final_w):
    batch, seq, D = x.shape
    x2d = x.reshape(batch * seq, D)
    for l in range(norm_mix_w.shape[0]):
        x2d = _layer(x2d, batch, seq, norm_mix_w[l], w_in[l], dn_conv_w[l], dn_a_log[l], dn_dt_bias[l],
                     dn_norm_w[l], gla_w_alpha2[l], gla_b_alpha[l], gla_norm_w[l], w_branch_dn[l],
                     w_branch_gla[l], w_out[l], norm_ffn_w[l], w_ffn_in[l], ffn_conv_w[l], ffn_conv_b[l],
                     w_ffn_out[l])
    return _rmsnorm(x2d, norm_final_w, F32).reshape(batch, seq, D)
```

```python
import functools

import numpy as np
import jax
import jax.numpy as jnp
from jax import lax
from jax.experimental import pallas as pl
from jax.experimental.pallas import tpu as pltpu

F32 = jnp.float32
BF16 = jnp.bfloat16
HI = lax.Precision.HIGHEST

EPS = 1e-6
CHUNK = 64
DN_HEADS = 16
DN_D = 128
DN_CONV = 4
GLA_HEADS = 8
GLA_DK = 128
GLA_DV = 256
GLA_RANK = 16
GLA_TAU = 16.0
FFN_CONV = 3
LANES = 128
HALO = 8

C_DQ, C_DK, C_DV, C_DZ = 0, 2048, 4096, 6144
C_DGATE = 8192
C_GQ, C_GK, C_GV, C_GR = 8224, 9248, 10272, 12320
C_LR = 14368
C_GATE_DN, C_GATE_GLA = 14384, 18480
S_B, S_A = 0, 16
S_LR = C_LR % LANES
GLA_SHIFT = C_GQ % LANES
GATE_SHIFT = C_GATE_DN % LANES
assert C_GK % LANES == C_GV % LANES == C_GR % LANES == GLA_SHIFT and C_GATE_GLA % LANES == GATE_SHIFT

VMEM_LIMIT = 56 * 1024 * 1024


def _cparams(sem):
    return pltpu.CompilerParams(dimension_semantics=sem, vmem_limit_bytes=VMEM_LIMIT)


def _silu(x):
    return x * jax.nn.sigmoid(x)


def _softplus(x):
    return jnp.maximum(x, 0.0) + jnp.log1p(jnp.exp(-jnp.abs(x)))


def _dot(a, b, precision=None):
    return jnp.dot(a, b, precision=precision, preferred_element_type=F32)


def _dot_nt(a, b, precision=None):
    return lax.dot_general(a, b, (((1,), (1,)), ((), ())), precision=precision, preferred_element_type=F32)


def _dot_tn(a, b, precision=None):
    return lax.dot_general(a, b, (((0,), (0,)), ((), ())), precision=precision, preferred_element_type=F32)


def _rmsnorm_kernel(x_ref, w_ref, o_ref):
    x = x_ref[...]
    ms = jnp.mean(x * x, axis=-1, keepdims=True)
    o_ref[...] = (x * lax.rsqrt(ms + EPS) * w_ref[...]).astype(o_ref.dtype)


def _rmsnorm(x, w, out_dtype, tm=256):
    M, D = x.shape
    tm = min(tm, M)
    return pl.pallas_call(
        _rmsnorm_kernel,
        grid=(M // tm,),
        in_specs=[pl.BlockSpec((tm, D), lambda i: (i, 0)), pl.BlockSpec((1, D), lambda i: (0, 0))],
        out_specs=pl.BlockSpec((tm, D), lambda i: (i, 0)),
        out_shape=jax.ShapeDtypeStruct((M, D), out_dtype),
        compiler_params=_cparams(("parallel",)),
        name="rmsnorm",
    )(x, w.reshape(1, D))


def _inproj_kernel(a_ref, wt_ref, o_ref):
    o_ref[...] = _dot_nt(a_ref[...], wt_ref[...].astype(BF16))


def _inproj(a, wt, tm=2048, tn=512):
    M, K = a.shape
    N = wt.shape[0]
    tm = min(tm, M)
    return pl.pallas_call(
        _inproj_kernel,
        grid=(M // tm, pl.cdiv(N, tn)),
        in_specs=[pl.BlockSpec((tm, K), lambda i, j: (i, 0), pipeline_mode=pl.Buffered(1)),
                  pl.BlockSpec((tn, K), lambda i, j: (j, 0))],
        out_specs=pl.BlockSpec((tm, tn), lambda i, j: (i, j)),
        out_shape=jax.ShapeDtypeStruct((M, N), F32),
        compiler_params=_cparams(("parallel", "arbitrary")),
        name="inproj",
    )(a, wt)


def _lane_shift(tiles, shift):
    lane = lax.broadcasted_iota(jnp.int32, tiles[0].shape, 1)
    rolled = [pltpu.roll(t, LANES - shift, 1) for t in tiles]
    return [jnp.where(lane < LANES - shift, rolled[i], rolled[i + 1]) for i in range(len(tiles) - 1)]


def _l2norm(x):
    return x * lax.rsqrt(jnp.sum(x * x, axis=-1, keepdims=True) + EPS)


def _split3(x):
    hi = x.astype(BF16)
    r = x - hi.astype(F32)
    mid = r.astype(BF16)
    lo = (r - mid.astype(F32)).astype(BF16)
    return hi, mid, lo


def _split2(x):
    hi = x.astype(BF16)
    return hi, (x - hi.astype(F32)).astype(BF16)


DN_UNROLL = 8
DN_P_PIECES = 4
DN_P_PER_STAGE = 2


def _dn_kernel(q_ref, k_ref, v_ref, z_ref, small_ref, cwq_ref, cwk_ref, cwv_ref, alog_ref, dtb_ref, nw_ref,
               o_ref, gates_s, qs_s, k_s, v_s, beta_s, bc_s, lhs_s, add_s, ebl_s, *, seq):
    h = pl.program_id(1)
    C = CHUNK
    D = DN_D
    U = range(DN_UNROLL)
    n_groups = seq // (C * DN_UNROLL)
    row = lax.broadcasted_iota(jnp.int32, (seq, D), 0)

    @pl.when(h == 0)
    def _():
        small = small_ref[...]
        lane = lax.broadcasted_iota(jnp.int32, small.shape, 1)
        x = -jnp.exp(alog_ref[...]) * _softplus(small + dtb_ref[...])
        in_chunk = row % C
        s = 1
        while s < C:
            x = x + jnp.where(in_chunk >= s, pltpu.roll(x, s, 0), 0.0)
            s *= 2
        gates_s[...] = jnp.where(lane < S_A, jax.nn.sigmoid(small), x)

    sr = lax.broadcasted_iota(jnp.int32, (3 * LANES, 2 * LANES), 0) % LANES
    sc = lax.broadcasted_iota(jnp.int32, (3 * LANES, 2 * LANES), 1)
    sel = (sr == jnp.where(sc < LANES, h + S_B, h + S_A)).astype(BF16)
    G = C * DN_UNROLL

    def pass_p(g):
        rep = _dot(jnp.concatenate(_split3(gates_s[pl.ds(pl.multiple_of(g * G, G), G), :]), axis=1), sel)
        beta_s[pl.ds(pl.multiple_of(g * G, G), G), :] = rep[:, :LANES]
        bc_s[pl.ds(pl.multiple_of(g * G, G), G), :] = rep[:, LANES:]
        yield
        R = G // DN_P_PIECES
        for piece in range(DN_P_PIECES):
            r0 = g * G + piece * R
            rows = pl.ds(pl.multiple_of(r0, R), R)
            halo_rows = pl.ds(pl.multiple_of(jnp.maximum(r0 - HALO, 0), HALO), HALO)
            for x_ref, cw_ref, out_s, norm in ((q_ref, cwq_ref, qs_s, D ** -0.5), (k_ref, cwk_ref, k_s, 1.0),
                                               (v_ref, cwv_ref, v_s, None)):
                cw = cw_ref[...]
                halo = jnp.where(r0 == 0, 0.0, x_ref[halo_rows, :])
                x = x_ref[rows, :]
                ext = jnp.concatenate([halo, x], axis=0)
                y = x * cw[DN_CONV - 1:DN_CONV, :]
                for s in range(1, DN_CONV):
                    y = y + pltpu.roll(ext, s, 0)[HALO:, :] * cw[DN_CONV - 1 - s:DN_CONV - s, :]
                y = _silu(y)
                out_s[rows, :] = y if norm is None else _l2norm(y) * norm
                yield

    ri = lax.broadcasted_iota(jnp.int32, (C, C), 0)
    ci = lax.broadcasted_iota(jnp.int32, (C, C), 1)
    tril = ri >= ci
    strict = ri > ci
    eye = (ri == ci).astype(F32)
    nw = nw_ref[...]

    def pass_a(g):
        ns = [g * DN_UNROLL + u for u in U]
        sls = [pl.ds(pl.multiple_of(n * C, C), C) for n in ns]
        qc = [qs_s[s, :] for s in sls]
        kc = [k_s[s, :] for s in sls]
        vc = [v_s[s, :] for s in sls]
        bt = [beta_s[s, :] for s in sls]
        bc = [bc_s[s, :] for s in sls]
        kb = [kc[u] * bt[u] for u in U]
        gram = [_dot_nt(jnp.concatenate([kb[u], qc[u]], axis=0).astype(BF16), kc[u].astype(BF16)) for u in U]
        yield
        dm = [bc[u][:, :C] - jnp.transpose(bc[u])[:C, :] for u in U]
        decay = [jnp.where(tril, jnp.exp(jnp.minimum(dm[u], 0.0)), 0.0) for u in U]
        eb = [jnp.exp(bc[u]) for u in U]
        a = [jnp.where(strict, gram[u][:C] * decay[u], 0.0) for u in U]
        att = [(gram[u][C:] * decay[u]).astype(BF16) for u in U]
        p = [eye - a[u] for u in U]
        a_bf = [a[u].astype(BF16) for u in U]
        x = [_dot(a_bf[u], a_bf[u]) for u in U]
        yield
        for _ in range(4):
            px = [_dot(jnp.concatenate([p[u], x[u]], axis=0).astype(BF16), x[u].astype(BF16)) for u in U]
            p = [p[u] + px[u][:C] for u in U]
            x = [px[u][C:] for u in U]
            yield
        px = [_dot(p[u].astype(BF16), x[u].astype(BF16)) for u in U]
        p = [(p[u] + px[u]).astype(BF16) for u in U]
        yield
        rhs = [jnp.concatenate([kb[u] * eb[u], vc[u] * bt[u]], axis=1).astype(BF16) for u in U]
        wu = [_dot(p[u], rhs[u]).astype(BF16) for u in U]
        yield
        kd = [(kc[u] * jnp.exp(bc[u][C - 1:C, :] - bc[u])).astype(BF16) for u in U]
        aw = [_dot(att[u], wu[u]) for u in U]
        yield
        mn = [_dot_tn(kd[u], wu[u]) for u in U]
        for u in U:
            n = ns[u]
            lhs_s[n, 0:D, :] = mn[u][:, :D].astype(BF16)
            lhs_s[n, D:D + C, :] = (qc[u] * eb[u] - aw[u][:, :D]).astype(BF16)
            add_s[n, 0:D, :] = mn[u][:, D:]
            add_s[n, D:D + C, :] = aw[u][:, D:]
            ebl_s[n] = jnp.broadcast_to(eb[u][C - 1:C, :], (8, LANES))
        yield

    def finish(n, o):
        sl = pl.ds(pl.multiple_of(n * C, C), C)
        y = o * lax.rsqrt(jnp.mean(o * o, axis=-1, keepdims=True) + EPS) * nw * _silu(z_ref[sl, :])
        o_ref[sl, :] = y.astype(o_ref.dtype)

    def step_b(n, carry):
        S, o_prev = carry
        finish(jnp.maximum(n - 1, 0), o_prev)
        ps = _dot(lhs_s[n], S.astype(BF16))
        add = add_s[n]
        o = ps[D:] + add[D:]
        S = ebl_s[n][0:1, :] * S + (add[:D] - ps[:D])
        return S, o

    def interleave(g_a, g_b, g_p, carry):
        stages = pass_a(g_a) if g_a is not None else iter(())
        steps = [g_b * DN_UNROLL + u for u in U] if g_b is not None else []
        pieces = pass_p(g_p) if g_p is not None else iter(())
        next(stages, None)
        for _ in stages:
            if steps:
                carry = step_b(steps.pop(0), carry)
            for _ in range(DN_P_PER_STAGE):
                next(pieces, None)
        for n in steps:
            carry = step_b(n, carry)
        for _ in pieces:
            pass
        return carry

    assert n_groups >= 2
    carry = (jnp.zeros((D, D), F32), jnp.zeros((C, D), F32))
    for _ in pass_p(0):
        pass
    carry = interleave(0, None, 1, carry)
    carry = lax.fori_loop(1, n_groups - 1, lambda g, c: interleave(g, g - 1, g + 1, c), carry)
    carry = interleave(n_groups - 1, n_groups - 2, None, carry)
    _, o_last = interleave(None, n_groups - 1, None, carry)
    finish(seq // C - 1, o_last)


def _deltanet(proj, conv_w, a_log, dt_bias, norm_w, batch, seq):
    M = proj.shape[0]
    H = DN_HEADS
    n_chunks = seq // CHUNK
    assert seq % (CHUNK * DN_UNROLL) == 0
    blk = lambda c0: pl.BlockSpec((seq, DN_D), lambda b, h: (b, c0 // DN_D + h))
    cw = lambda c0: pl.BlockSpec((DN_CONV, DN_D), lambda b, h: (0, c0 // DN_D + h))
    row = pl.BlockSpec((1, LANES), lambda b, h: (0, 0))
    alog = jnp.zeros((1, LANES), F32).at[0, S_A:S_A + H].set(a_log)
    dtb = jnp.zeros((1, LANES), F32).at[0, S_A:S_A + H].set(dt_bias)
    seq_f32 = pltpu.VMEM((seq, DN_D), F32)
    return pl.pallas_call(
        functools.partial(_dn_kernel, seq=seq),
        grid=(batch, H),
        in_specs=[blk(C_DQ), blk(C_DK), blk(C_DV), blk(C_DZ),
                  pl.BlockSpec((seq, LANES), lambda b, h: (b, C_DGATE // LANES)),
                  cw(0), cw(2048), cw(4096), row, row, row],
        out_specs=pl.BlockSpec((seq, DN_D), lambda b, h: (b, h)),
        out_shape=jax.ShapeDtypeStruct((M, H * DN_D), BF16),
        scratch_shapes=[seq_f32] * 6 + [pltpu.VMEM((n_chunks, DN_D + CHUNK, DN_D), BF16),
                                        pltpu.VMEM((n_chunks, DN_D + CHUNK, DN_D), F32),
                                        pltpu.VMEM((n_chunks, 8, LANES), F32)],
        compiler_params=_cparams(("parallel", "arbitrary")),
        name="deltanet",
    )(proj, proj, proj, proj, proj, conv_w, conv_w, conv_w, alog, dtb, norm_w.reshape(1, DN_D))


GLA_LEVELS = (32, 16, 8, 4, 2, 1)
GLA_UNROLL = 4


def _gla_constants():
    C = CHUNK
    i = np.arange(C)[:, None]
    m = np.arange(C)[None, :]
    blocks = [(m <= i), (m > i)]
    upper_rows, masks = [], []
    for s in GLA_LEVELS:
        p = (i // (2 * s)) * (2 * s)
        m0 = p + s - 1
        up = (i - p) >= s
        blocks.append(np.where(up, (m > m0) & (m <= i), (m > i) & (m <= m0)))
        upper_rows.append(np.broadcast_to(up, (C, LANES)))
        j = m
        pj = (j // (2 * s)) * (2 * s)
        masks.append(up & ((j - pj) < s) & (p == pj))
    masks.append(i == m)
    mcat = np.concatenate(blocks, axis=0).astype(np.float32)
    upper = np.stack(upper_rows).astype(np.float32)
    mask = np.stack(masks).astype(np.float32)
    return mcat, upper, mask


N_QK_TILES = GLA_DK // LANES + 1
N_VR_TILES = GLA_DV // LANES + 1


def _gla_kernel(*refs, seq):
    q_t, refs = refs[:N_QK_TILES], refs[N_QK_TILES:]
    k_t, refs = refs[:N_QK_TILES], refs[N_QK_TILES:]
    v_t, refs = refs[:N_VR_TILES], refs[N_VR_TILES:]
    r_t, refs = refs[:N_VR_TILES], refs[N_VR_TILES:]
    lr_ref, wa_ref, ba_ref, nw_ref, mcat_ref, upper_ref, mask_ref, o_ref, gk_s, st_s = refs
    C = CHUNK

    def rows(tile_refs, s):
        return jnp.concatenate(_lane_shift([t[s, :] for t in tile_refs], GLA_SHIFT), axis=1)

    x_alpha = _dot(lr_ref[...], wa_ref[...], HI) + ba_ref[...]
    gk_s[...] = -_softplus(-x_alpha) / GLA_TAU
    st_s[...] = jnp.zeros_like(st_s)
    mcat = mcat_ref[...]
    nw = nw_ref[...]
    n_lvl = len(GLA_LEVELS)

    def body(g, carry):
        U = range(GLA_UNROLL)
        sls = [pl.ds(pl.multiple_of((g * GLA_UNROLL + u) * C, C), C) for u in U]
        qc = [rows(q_t, s) * (GLA_DK ** -0.5) for s in sls]
        kc = [rows(k_t, s) for s in sls]
        v_bf = [rows(v_t, s).astype(BF16) for s in sls]
        e = [_dot(mcat, jnp.concatenate(_split2(gk_s[s, :]), axis=1)) for s in sls]
        ex = [jnp.exp(e[u][:, :GLA_DK] + e[u][:, GLA_DK:]) for u in U]
        k_bf = [kc[u].astype(BF16) for u in U]
        att = [mask_ref[n_lvl] * _dot_nt(qc[u].astype(BF16), k_bf[u]) for u in U]
        for l in range(n_lvl):
            up = upper_ref[l] > 0.5
            zl = [(jnp.where(up, qc[u], kc[u]) * ex[u][(2 + l) * C:(3 + l) * C, :]).astype(BF16) for u in U]
            gl = [_dot_nt(zl[u], zl[u]) for u in U]
            att = [att[u] + mask_ref[l] * gl[u] for u in U]
        ds = [_dot_tn(v_bf[u], (kc[u] * ex[u][C:2 * C, :]).astype(BF16)) for u in U]
        ov = [_dot(att[u].astype(BF16), v_bf[u]) for u in U]
        st = st_s[...]
        for u in U:
            eb = ex[u][0:C, :]
            o = _dot_nt((qc[u] * eb).astype(BF16), st.astype(BF16)) + ov[u]
            st = st * eb[C - 1:C, :] + ds[u]
            y = o * lax.rsqrt(jnp.mean(o * o, axis=-1, keepdims=True) + EPS) * nw * _silu(rows(r_t, sls[u]))
            o_ref[sls[u], :] = y.astype(o_ref.dtype)
        st_s[...] = st
        return carry

    lax.fori_loop(0, seq // (C * GLA_UNROLL), body, 0)


def _gla(proj, w_alpha2, b_alpha, norm_w, batch, seq):
    M = proj.shape[0]
    H = GLA_HEADS
    assert seq % (CHUNK * GLA_UNROLL) == 0
    mcat, upper, mask = _gla_constants()
    wa = jnp.zeros((LANES, H * GLA_DK), F32).at[S_LR:S_LR + GLA_RANK, :].set(w_alpha2)
    full = lambda a: pl.BlockSpec(a.shape, lambda b, h: (0,) * a.ndim)

    def tiles(c0, width, n):
        return [pl.BlockSpec((seq, LANES), lambda b, h, t=t: (b, c0 // LANES + h * (width // LANES) + t))
                for t in range(n)]

    n_in = 2 * N_QK_TILES + 2 * N_VR_TILES
    return pl.pallas_call(
        functools.partial(_gla_kernel, seq=seq),
        grid=(batch, H),
        in_specs=[*tiles(C_GQ, GLA_DK, N_QK_TILES), *tiles(C_GK, GLA_DK, N_QK_TILES),
                  *tiles(C_GV, GLA_DV, N_VR_TILES), *tiles(C_GR, GLA_DV, N_VR_TILES),
                  pl.BlockSpec((seq, LANES), lambda b, h: (b, C_LR // LANES)),
                  pl.BlockSpec((LANES, GLA_DK), lambda b, h: (0, h)),
                  pl.BlockSpec((1, GLA_DK), lambda b, h: (0, h)),
                  pl.BlockSpec((1, GLA_DV), lambda b, h: (0, 0)),
                  full(mcat), full(upper), full(mask)],
        out_specs=pl.BlockSpec((seq, GLA_DV), lambda b, h: (b, h)),
        out_shape=jax.ShapeDtypeStruct((M, H * GLA_DV), BF16),
        scratch_shapes=[pltpu.VMEM((seq, GLA_DK), F32), pltpu.VMEM((GLA_DV, GLA_DK), F32)],
        compiler_params=_cparams(("parallel", "parallel")),
        name="gla",
    )(*([proj] * (n_in + 1)), wa, b_alpha.reshape(1, -1), norm_w.reshape(1, GLA_DV),
      jnp.asarray(mcat, BF16), jnp.asarray(upper), jnp.asarray(mask))


def _merge_kernel(odn_ref, ogla_ref, pa_ref, pb_ref, gd_ref, gdn_ref, gg_ref, ggn_ref, o_ref):
    ya = _dot(odn_ref[...], pa_ref[...].astype(BF16))
    yb = _dot(ogla_ref[...], pb_ref[...].astype(BF16))

    def gate(main_ref, next_ref):
        n = main_ref.shape[1] // LANES
        tiles = [main_ref[:, t * LANES:(t + 1) * LANES] for t in range(n)] + [next_ref[...]]
        return jax.nn.sigmoid(jnp.concatenate(_lane_shift(tiles, GATE_SHIFT), axis=1))

    o_ref[...] = (gate(gd_ref, gdn_ref) * ya + gate(gg_ref, ggn_ref) * yb).astype(o_ref.dtype)


def _merge(o_dn, o_gla, pa, pb, proj, tm=1024, tn=512):
    M, K = o_dn.shape
    N = pa.shape[1]
    tm = min(tm, M)
    lt = tn // LANES
    main = lambda c0: pl.BlockSpec((tm, tn), lambda i, j: (i, c0 // tn + j))
    nxt = lambda c0: pl.BlockSpec((tm, LANES), lambda i, j: (i, c0 // LANES + (j + 1) * lt))
    assert (C_GATE_DN - GATE_SHIFT) % tn == 0 and (C_GATE_GLA - GATE_SHIFT) % tn == 0
    return pl.pallas_call(
        _merge_kernel,
        grid=(M // tm, N // tn),
        in_specs=[pl.BlockSpec((tm, K), lambda i, j: (i, 0)),
                  pl.BlockSpec((tm, K), lambda i, j: (i, 0)),
                  pl.BlockSpec((K, tn), lambda i, j: (0, j)),
                  pl.BlockSpec((K, tn), lambda i, j: (0, j)),
                  main(C_GATE_DN), nxt(C_GATE_DN), main(C_GATE_GLA), nxt(C_GATE_GLA)],
        out_specs=pl.BlockSpec((tm, tn), lambda i, j: (i, j)),
        out_shape=jax.ShapeDtypeStruct((M, N), BF16),
        compiler_params=_cparams(("parallel", "arbitrary")),
        name="merge",
    )(o_dn, o_gla, pa, pb, proj, proj, proj, proj)


def _mm_res_kernel(a_ref, b_ref, r_ref, o_ref):
    o_ref[...] = r_ref[...] + _dot(a_ref[...], b_ref[...].astype(BF16))


def _matmul_residual(a, b, res, tm=1024, tn=512, name="outproj"):
    M, K = a.shape
    N = b.shape[1]
    tm = min(tm, M)
    return pl.pallas_call(
        _mm_res_kernel,
        grid=(M // tm, N // tn),
        in_specs=[pl.BlockSpec((tm, K), lambda i, j: (i, 0)),
                  pl.BlockSpec((K, tn), lambda i, j: (0, j)),
                  pl.BlockSpec((tm, tn), lambda i, j: (i, j))],
        out_specs=pl.BlockSpec((tm, tn), lambda i, j: (i, j)),
        out_shape=jax.ShapeDtypeStruct((M, N), F32),
        compiler_params=_cparams(("parallel", "arbitrary")),
        name=name,
    )(a, b, res)


FFN_ROW_BLOCKS = 4


def _ffn_in_kernel(h_ref, wg_ref, wu_ref, cw_ref, cb_ref, o_ref):
    wg = wg_ref[...].astype(BF16)
    wu = wu_ref[...].astype(BF16)
    cw = cw_ref[...]
    cb = cb_ref[...]
    rb = h_ref.shape[0] // FFN_ROW_BLOCKS
    tail = jnp.zeros((HALO, wg.shape[1]), F32)
    for r in range(FFN_ROW_BLOCKS):
        h = h_ref[r * rb:(r + 1) * rb, :]
        gate = _dot(h, wg)
        up = _dot(h, wu)
        ext = jnp.concatenate([tail, gate], axis=0)
        y = gate * cw[FFN_CONV - 1:FFN_CONV, :] + cb
        for s in range(1, FFN_CONV):
            y = y + pltpu.roll(ext, s, 0)[HALO:, :] * cw[FFN_CONV - 1 - s:FFN_CONV - s, :]
        o_ref[r * rb:(r + 1) * rb, :] = (_silu(y) * up).astype(o_ref.dtype)
        tail = gate[rb - HALO:, :]


def _ffn_in(h, w_in, conv_w, conv_b, seq, tn=256):
    M, K = h.shape
    N = w_in.shape[1] // 2
    nb = N // tn
    assert seq % (FFN_ROW_BLOCKS * HALO) == 0
    return pl.pallas_call(
        _ffn_in_kernel,
        grid=(M // seq, nb),
        in_specs=[pl.BlockSpec((seq, K), lambda i, j: (i, 0), pipeline_mode=pl.Buffered(1)),
                  pl.BlockSpec((K, tn), lambda i, j: (0, j)),
                  pl.BlockSpec((K, tn), lambda i, j: (0, nb + j)),
                  pl.BlockSpec((FFN_CONV, tn), lambda i, j: (0, j)),
                  pl.BlockSpec((1, tn), lambda i, j: (0, j))],
        out_specs=pl.BlockSpec((seq, tn), lambda i, j: (i, j)),
        out_shape=jax.ShapeDtypeStruct((M, N), BF16),
        compiler_params=_cparams(("parallel", "arbitrary")),
        name="ffn_in",
    )(h, w_in, w_in, conv_w, conv_b.reshape(1, N))


def _ffn_out(a, b, res, tm=1024, tn=256):
    M, K = a.shape
    N = b.shape[1]
    tm = min(tm, M)
    return pl.pallas_call(
        _mm_res_kernel,
        grid=(M // tm, N // tn),
        in_specs=[pl.BlockSpec((tm, K), lambda i, j: (i, 0), pipeline_mode=pl.Buffered(1)),
                  pl.BlockSpec((K, tn), lambda i, j: (0, j)),
                  pl.BlockSpec((tm, tn), lambda i, j: (i, j))],
        out_specs=pl.BlockSpec((tm, tn), lambda i, j: (i, j)),
        out_shape=jax.ShapeDtypeStruct((M, N), F32),
        compiler_params=_cparams(("parallel", "arbitrary")),
        name="ffn_out",
    )(a, b, res)


def _layer(x2d, batch, seq, norm_mix_w, w_in, dn_conv_w, dn_a_log, dn_dt_bias, dn_norm_w,
           gla_w_alpha2, gla_b_alpha, gla_norm_w, w_branch_dn, w_branch_gla, w_out,
           norm_ffn_w, w_ffn_in, ffn_conv_w, ffn_conv_b, w_ffn_out):
    h = _rmsnorm(x2d, norm_mix_w, BF16)
    proj = _inproj(h, w_in.T)
    o_dn = _deltanet(proj, dn_conv_w, dn_a_log, dn_dt_bias, dn_norm_w, batch, seq)
    o_gla = _gla(proj, gla_w_alpha2, gla_b_alpha, gla_norm_w, batch, seq)
    merged = _merge(o_dn, o_gla, w_branch_dn, w_branch_gla, proj)
    x1 = _matmul_residual(merged, w_out, x2d)

    h2 = _rmsnorm(x1, norm_ffn_w, BF16)
    act = _ffn_in(h2, w_ffn_in, ffn_conv_w, ffn_conv_b, seq)
    return _ffn_out(act, w_ffn_out.astype(BF16), x1)


def kernel(x, norm_mix_w, w_in, dn_conv_w, dn_a_log, dn_dt_bias, dn_norm_w, gla_w_alpha2, gla_b_alpha, gla_norm_w, w_branch_dn, w_branch_gla, w_out, norm_ffn_w, w_ffn_in, ffn_conv_w, ffn_conv_b, w_ffn_out, norm_final_w):
    batch, seq, D = x.shape
    x2d = x.reshape(batch * seq, D)
    for l in range(norm_mix_w.shape[0]):
        x2d = _layer(x2d, batch, seq, norm_mix_w[l], w_in[l], dn_conv_w[l], dn_a_log[l], dn_dt_bias[l],
                     dn_norm_w[l], gla_w_alpha2[l], gla_b_alpha[l], gla_norm_w[l], w_branch_dn[l],
                     w_branch_gla[l], w_out[l], norm_ffn_w[l], w_ffn_in[l], ffn_conv_w[l], ffn_conv_b[l],
                     w_ffn_out[l])
    return _rmsnorm(x2d, norm_final_w, F32).reshape(batch, seq, D)
```

```python
import functools

import numpy as np
import jax
import jax.numpy as jnp
from jax import lax
from jax.experimental import pallas as pl
from jax.experimental.pallas import tpu as pltpu

F32 = jnp.float32
BF16 = jnp.bfloat16
HI = lax.Precision.HIGHEST

EPS = 1e-6
CHUNK = 64
DN_HEADS = 16
DN_D = 128
DN_CONV = 4
GLA_HEADS = 8
GLA_DK = 128
GLA_DV = 256
GLA_RANK = 16
GLA_TAU = 16.0
FFN_CONV = 3
LANES = 128
HALO = 8

C_DQ, C_DK, C_DV, C_DZ = 0, 2048, 4096, 6144
C_DGATE = 8192
C_GQ, C_GK, C_GV, C_GR = 8224, 9248, 10272, 12320
C_LR = 14368
C_GATE_DN, C_GATE_GLA = 14384, 18480
S_B, S_A = 0, 16
S_LR = C_LR % LANES
GLA_SHIFT = C_GQ % LANES
GATE_SHIFT = C_GATE_DN % LANES
assert C_GK % LANES == C_GV % LANES == C_GR % LANES == GLA_SHIFT and C_GATE_GLA % LANES == GATE_SHIFT

VMEM_LIMIT = 56 * 1024 * 1024


def _cparams(sem):
    return pltpu.CompilerParams(dimension_semantics=sem, vmem_limit_bytes=VMEM_LIMIT)


def _silu(x):
    return x * jax.nn.sigmoid(x)


def _softplus(x):
    return jnp.maximum(x, 0.0) + jnp.log1p(jnp.exp(-jnp.abs(x)))


def _dot(a, b, precision=None):
    return jnp.dot(a, b, precision=precision, preferred_element_type=F32)


def _dot_nt(a, b, precision=None):
    return lax.dot_general(a, b, (((1,), (1,)), ((), ())), precision=precision, preferred_element_type=F32)


def _dot_tn(a, b, precision=None):
    return lax.dot_general(a, b, (((0,), (0,)), ((), ())), precision=precision, preferred_element_type=F32)


def _rmsnorm_kernel(x_ref, w_ref, o_ref):
    x = x_ref[...]
    ms = jnp.mean(x * x, axis=-1, keepdims=True)
    o_ref[...] = (x * lax.rsqrt(ms + EPS) * w_ref[...]).astype(o_ref.dtype)


def _rmsnorm(x, w, out_dtype, tm=256):
    M, D = x.shape
    tm = min(tm, M)
    return pl.pallas_call(
        _rmsnorm_kernel,
        grid=(M // tm,),
        in_specs=[pl.BlockSpec((tm, D), lambda i: (i, 0)), pl.BlockSpec((1, D), lambda i: (0, 0))],
        out_specs=pl.BlockSpec((tm, D), lambda i: (i, 0)),
        out_shape=jax.ShapeDtypeStruct((M, D), out_dtype),
        compiler_params=_cparams(("parallel",)),
        name="rmsnorm",
    )(x, w.reshape(1, D))


def _inproj_kernel(a_ref, wt_ref, o_ref):
    hn = wt_ref.shape[0] // 2
    for c in (slice(0, hn), slice(hn, 2 * hn)):
        o_ref[:, c] = _dot_nt(a_ref[...], wt_ref[c, :].astype(BF16))


def _inproj(a, wt, tm=2048, tn=512):
    M, K = a.shape
    N = wt.shape[0]
    tm = min(tm, M)
    return pl.pallas_call(
        _inproj_kernel,
        grid=(M // tm, pl.cdiv(N, tn)),
        in_specs=[pl.BlockSpec((tm, K), lambda i, j: (i, 0), pipeline_mode=pl.Buffered(1)),
                  pl.BlockSpec((tn, K), lambda i, j: (j, 0))],
        out_specs=pl.BlockSpec((tm, tn), lambda i, j: (i, j)),
        out_shape=jax.ShapeDtypeStruct((M, N), F32),
        compiler_params=_cparams(("parallel", "arbitrary")),
        name="inproj",
    )(a, wt)


def _lane_shift(tiles, shift):
    lane = lax.broadcasted_iota(jnp.int32, tiles[0].shape, 1)
    rolled = [pltpu.roll(t, LANES - shift, 1) for t in tiles]
    return [jnp.where(lane < LANES - shift, rolled[i], rolled[i + 1]) for i in range(len(tiles) - 1)]


def _l2norm(x):
    return x * lax.rsqrt(jnp.sum(x * x, axis=-1, keepdims=True) + EPS)


def _split3(x):
    hi = x.astype(BF16)
    r = x - hi.astype(F32)
    mid = r.astype(BF16)
    lo = (r - mid.astype(F32)).astype(BF16)
    return hi, mid, lo


def _split2(x):
    hi = x.astype(BF16)
    return hi, (x - hi.astype(F32)).astype(BF16)


DN_UNROLL = 8
DN_P_PIECES = 4
DN_P_PER_STAGE = 2


def _dn_kernel(q_ref, k_ref, v_ref, z_ref, small_ref, cwq_ref, cwk_ref, cwv_ref, alog_ref, dtb_ref, nw_ref,
               o_ref, gates_s, qs_s, k_s, v_s, beta_s, bc_s, lhs_s, add_s, ebl_s, *, seq):
    h = pl.program_id(1)
    C = CHUNK
    D = DN_D
    U = range(DN_UNROLL)
    n_groups = seq // (C * DN_UNROLL)
    row = lax.broadcasted_iota(jnp.int32, (seq, D), 0)

    @pl.when(h == 0)
    def _():
        small = small_ref[...]
        lane = lax.broadcasted_iota(jnp.int32, small.shape, 1)
        x = -jnp.exp(alog_ref[...]) * _softplus(small + dtb_ref[...])
        in_chunk = row % C
        s = 1
        while s < C:
            x = x + jnp.where(in_chunk >= s, pltpu.roll(x, s, 0), 0.0)
            s *= 2
        gates_s[...] = jnp.where(lane < S_A, jax.nn.sigmoid(small), x)

    sr = lax.broadcasted_iota(jnp.int32, (3 * LANES, 2 * LANES), 0) % LANES
    sc = lax.broadcasted_iota(jnp.int32, (3 * LANES, 2 * LANES), 1)
    sel = (sr == jnp.where(sc < LANES, h + S_B, h + S_A)).astype(BF16)
    G = C * DN_UNROLL

    def pass_p(g):
        rep = _dot(jnp.concatenate(_split3(gates_s[pl.ds(pl.multiple_of(g * G, G), G), :]), axis=1), sel)
        beta_s[pl.ds(pl.multiple_of(g * G, G), G), :] = rep[:, :LANES]
        bc_s[pl.ds(pl.multiple_of(g * G, G), G), :] = rep[:, LANES:]
        yield
        R = G // DN_P_PIECES
        for piece in range(DN_P_PIECES):
            r0 = g * G + piece * R
            rows = pl.ds(pl.multiple_of(r0, R), R)
            halo_rows = pl.ds(pl.multiple_of(jnp.maximum(r0 - HALO, 0), HALO), HALO)
            for x_ref, cw_ref, out_s, norm in ((q_ref, cwq_ref, qs_s, D ** -0.5), (k_ref, cwk_ref, k_s, 1.0),
                                               (v_ref, cwv_ref, v_s, None)):
                cw = cw_ref[...]
                halo = jnp.where(r0 == 0, 0.0, x_ref[halo_rows, :])
                x = x_ref[rows, :]
                ext = jnp.concatenate([halo, x], axis=0)
                y = x * cw[DN_CONV - 1:DN_CONV, :]
                for s in range(1, DN_CONV):
                    y = y + pltpu.roll(ext, s, 0)[HALO:, :] * cw[DN_CONV - 1 - s:DN_CONV - s, :]
                y = _silu(y)
                out_s[rows, :] = y if norm is None else _l2norm(y) * norm
                yield

    ri = lax.broadcasted_iota(jnp.int32, (C, C), 0)
    ci = lax.broadcasted_iota(jnp.int32, (C, C), 1)
    tril = ri >= ci
    strict = ri > ci
    eye = (ri == ci).astype(F32)
    nw = nw_ref[...]

    def pass_a(g):
        ns = [g * DN_UNROLL + u for u in U]
        sls = [pl.ds(pl.multiple_of(n * C, C), C) for n in ns]
        qc = [qs_s[s, :] for s in sls]
        kc = [k_s[s, :] for s in sls]
        vc = [v_s[s, :] for s in sls]
        bt = [beta_s[s, :] for s in sls]
        bc = [bc_s[s, :] for s in sls]
        kb = [kc[u] * bt[u] for u in U]
        gram = [_dot_nt(jnp.concatenate([kb[u], qc[u]], axis=0).astype(BF16), kc[u].astype(BF16)) for u in U]
        yield
        dm = [bc[u][:, :C] - jnp.transpose(bc[u])[:C, :] for u in U]
        decay = [jnp.where(tril, jnp.exp(jnp.minimum(dm[u], 0.0)), 0.0) for u in U]
        eb = [jnp.exp(bc[u]) for u in U]
        a = [jnp.where(strict, gram[u][:C] * decay[u], 0.0) for u in U]
        att = [(gram[u][C:] * decay[u]).astype(BF16) for u in U]
        p = [eye - a[u] for u in U]
        a_bf = [a[u].astype(BF16) for u in U]
        x = [_dot(a_bf[u], a_bf[u]) for u in U]
        yield
        for _ in range(4):
            px = [_dot(jnp.concatenate([p[u], x[u]], axis=0).astype(BF16), x[u].astype(BF16)) for u in U]
            p = [p[u] + px[u][:C] for u in U]
            x = [px[u][C:] for u in U]
            yield
        px = [_dot(p[u].astype(BF16), x[u].astype(BF16)) for u in U]
        p = [(p[u] + px[u]).astype(BF16) for u in U]
        yield
        rhs = [jnp.concatenate([kb[u] * eb[u], vc[u] * bt[u]], axis=1).astype(BF16) for u in U]
        wu = [_dot(p[u], rhs[u]).astype(BF16) for u in U]
        yield
        kd = [(kc[u] * jnp.exp(bc[u][C - 1:C, :] - bc[u])).astype(BF16) for u in U]
        aw = [_dot(att[u], wu[u]) for u in U]
        yield
        mn = [_dot_tn(kd[u], wu[u]) for u in U]
        for u in U:
            n = ns[u]
            lhs_s[n, 0:D, :] = mn[u][:, :D].astype(BF16)
            lhs_s[n, D:D + C, :] = (qc[u] * eb[u] - aw[u][:, :D]).astype(BF16)
            add_s[n, 0:D, :] = mn[u][:, D:]
            add_s[n, D:D + C, :] = aw[u][:, D:]
            ebl_s[n] = jnp.broadcast_to(eb[u][C - 1:C, :], (8, LANES))
        yield

    def finish(n, o):
        sl = pl.ds(pl.multiple_of(n * C, C), C)
        y = o * lax.rsqrt(jnp.mean(o * o, axis=-1, keepdims=True) + EPS) * nw * _silu(z_ref[sl, :])
        o_ref[sl, :] = y.astype(o_ref.dtype)

    def step_b(n, carry):
        S, o_prev = carry
        finish(jnp.maximum(n - 1, 0), o_prev)
        ps = _dot(lhs_s[n], S.astype(BF16))
        add = add_s[n]
        o = ps[D:] + add[D:]
        S = ebl_s[n][0:1, :] * S + (add[:D] - ps[:D])
        return S, o

    def interleave(g_a, g_b, g_p, carry):
        stages = pass_a(g_a) if g_a is not None else iter(())
        steps = [g_b * DN_UNROLL + u for u in U] if g_b is not None else []
        pieces = pass_p(g_p) if g_p is not None else iter(())
        next(stages, None)
        for _ in stages:
            if steps:
                carry = step_b(steps.pop(0), carry)
            for _ in range(DN_P_PER_STAGE):
                next(pieces, None)
        for n in steps:
            carry = step_b(n, carry)
        for _ in pieces:
            pass
        return carry

    assert n_groups >= 2
    carry = (jnp.zeros((D, D), F32), jnp.zeros((C, D), F32))
    for _ in pass_p(0):
        pass
    carry = interleave(0, None, 1, carry)
    carry = lax.fori_loop(1, n_groups - 1, lambda g, c: interleave(g, g - 1, g + 1, c), carry)
    carry = interleave(n_groups - 1, n_groups - 2, None, carry)
    _, o_last = interleave(None, n_groups - 1, None, carry)
    finish(seq // C - 1, o_last)


def _deltanet(proj, conv_w, a_log, dt_bias, norm_w, batch, seq):
    M = proj.shape[0]
    H = DN_HEADS
    n_chunks = seq // CHUNK
    assert seq % (CHUNK * DN_UNROLL) == 0
    blk = lambda c0: pl.BlockSpec((seq, DN_D), lambda b, h: (b, c0 // DN_D + h))
    cw = lambda c0: pl.BlockSpec((DN_CONV, DN_D), lambda b, h: (0, c0 // DN_D + h))
    row = pl.BlockSpec((1, LANES), lambda b, h: (0, 0))
    alog = jnp.zeros((1, LANES), F32).at[0, S_A:S_A + H].set(a_log)
    dtb = jnp.zeros((1, LANES), F32).at[0, S_A:S_A + H].set(dt_bias)
    seq_f32 = pltpu.VMEM((seq, DN_D), F32)
    return pl.pallas_call(
        functools.partial(_dn_kernel, seq=seq),
        grid=(batch, H),
        in_specs=[blk(C_DQ), blk(C_DK), blk(C_DV), blk(C_DZ),
                  pl.BlockSpec((seq, LANES), lambda b, h: (b, C_DGATE // LANES)),
                  cw(0), cw(2048), cw(4096), row, row, row],
        out_specs=pl.BlockSpec((seq, DN_D), lambda b, h: (b, h)),
        out_shape=jax.ShapeDtypeStruct((M, H * DN_D), BF16),
        scratch_shapes=[seq_f32] * 6 + [pltpu.VMEM((n_chunks, DN_D + CHUNK, DN_D), BF16),
                                        pltpu.VMEM((n_chunks, DN_D + CHUNK, DN_D), F32),
                                        pltpu.VMEM((n_chunks, 8, LANES), F32)],
        compiler_params=_cparams(("parallel", "arbitrary")),
        name="deltanet",
    )(proj, proj, proj, proj, proj, conv_w, conv_w, conv_w, alog, dtb, norm_w.reshape(1, DN_D))


GLA_LEVELS = (32, 16, 8, 4, 2, 1)
GLA_UNROLL = 4


def _gla_constants():
    C = CHUNK
    i = np.arange(C)[:, None]
    m = np.arange(C)[None, :]
    blocks = [(m <= i), (m > i)]
    upper_rows, masks = [], []
    for s in GLA_LEVELS:
        p = (i // (2 * s)) * (2 * s)
        m0 = p + s - 1
        up = (i - p) >= s
        blocks.append(np.where(up, (m > m0) & (m <= i), (m > i) & (m <= m0)))
        upper_rows.append(np.broadcast_to(up, (C, LANES)))
        j = m
        pj = (j // (2 * s)) * (2 * s)
        masks.append(up & ((j - pj) < s) & (p == pj))
    masks.append(i == m)
    mcat = np.concatenate(blocks, axis=0).astype(np.float32)
    upper = np.stack(upper_rows).astype(np.float32)
    mask = np.stack(masks).astype(np.float32)
    return mcat, upper, mask


N_QK_TILES = GLA_DK // LANES + 1
N_VR_TILES = GLA_DV // LANES + 1


def _gla_kernel(*refs, seq):
    q_t, refs = refs[:N_QK_TILES], refs[N_QK_TILES:]
    k_t, refs = refs[:N_QK_TILES], refs[N_QK_TILES:]
    v_t, refs = refs[:N_VR_TILES], refs[N_VR_TILES:]
    r_t, refs = refs[:N_VR_TILES], refs[N_VR_TILES:]
    lr_ref, wa_ref, ba_ref, nw_ref, mcat_ref, upper_ref, mask_ref, o_ref, gk_s, st_s, o_s = refs
    C = CHUNK

    def rows(tile_refs, s):
        return jnp.concatenate(_lane_shift([t[s, :] for t in tile_refs], GLA_SHIFT), axis=1)

    x_alpha = _dot(lr_ref[...], wa_ref[...], HI) + ba_ref[...]
    gk_s[...] = -_softplus(-x_alpha) / GLA_TAU
    st_s[...] = jnp.zeros_like(st_s)
    mcat = mcat_ref[...]
    nw = nw_ref[...]
    n_lvl = len(GLA_LEVELS)

    U = range(GLA_UNROLL)
    o_s[...] = jnp.zeros_like(o_s)

    def finish(g):
        for u in U:
            sl = pl.ds(pl.multiple_of((g * GLA_UNROLL + u) * C, C), C)
            o = o_s[u * C:(u + 1) * C, :]
            y = o * lax.rsqrt(jnp.mean(o * o, axis=-1, keepdims=True) + EPS) * nw * _silu(rows(r_t, sl))
            o_ref[sl, :] = y.astype(o_ref.dtype)

    def body(g, carry):
        finish(jnp.maximum(g - 1, 0))
        sls = [pl.ds(pl.multiple_of((g * GLA_UNROLL + u) * C, C), C) for u in U]
        qc = [rows(q_t, s) * (GLA_DK ** -0.5) for s in sls]
        kc = [rows(k_t, s) for s in sls]
        v_bf = [rows(v_t, s).astype(BF16) for s in sls]
        e = [_dot(mcat, jnp.concatenate(_split2(gk_s[s, :]), axis=1)) for s in sls]
        ex = [jnp.exp(e[u][:, :GLA_DK] + e[u][:, GLA_DK:]) for u in U]
        k_bf = [kc[u].astype(BF16) for u in U]
        att = [mask_ref[n_lvl] * _dot_nt(qc[u].astype(BF16), k_bf[u]) for u in U]
        for l in range(n_lvl):
            up = upper_ref[l] > 0.5
            zl = [(jnp.where(up, qc[u], kc[u]) * ex[u][(2 + l) * C:(3 + l) * C, :]).astype(BF16) for u in U]
            gl = [_dot_nt(zl[u], zl[u]) for u in U]
            att = [att[u] + mask_ref[l] * gl[u] for u in U]
        ds = [_dot_tn(v_bf[u], (kc[u] * ex[u][C:2 * C, :]).astype(BF16)) for u in U]
        ov = [_dot(att[u].astype(BF16), v_bf[u]) for u in U]
        st = st_s[...]
        for u in U:
            eb = ex[u][0:C, :]
            o_s[u * C:(u + 1) * C, :] = _dot_nt((qc[u] * eb).astype(BF16), st.astype(BF16)) + ov[u]
            st = st * eb[C - 1:C, :] + ds[u]
        st_s[...] = st
        return carry

    n_groups = seq // (C * GLA_UNROLL)
    lax.fori_loop(0, n_groups, body, 0)
    finish(n_groups - 1)


def _gla(proj, w_alpha2, b_alpha, norm_w, batch, seq):
    M = proj.shape[0]
    H = GLA_HEADS
    assert seq % (CHUNK * GLA_UNROLL) == 0
    mcat, upper, mask = _gla_constants()
    wa = jnp.zeros((LANES, H * GLA_DK), F32).at[S_LR:S_LR + GLA_RANK, :].set(w_alpha2)
    full = lambda a: pl.BlockSpec(a.shape, lambda b, h: (0,) * a.ndim)

    def tiles(c0, width, n):
        return [pl.BlockSpec((seq, LANES), lambda b, h, t=t: (b, c0 // LANES + h * (width // LANES) + t))
                for t in range(n)]

    n_in = 2 * N_QK_TILES + 2 * N_VR_TILES
    return pl.pallas_call(
        functools.partial(_gla_kernel, seq=seq),
        grid=(batch, H),
        in_specs=[*tiles(C_GQ, GLA_DK, N_QK_TILES), *tiles(C_GK, GLA_DK, N_QK_TILES),
                  *tiles(C_GV, GLA_DV, N_VR_TILES), *tiles(C_GR, GLA_DV, N_VR_TILES),
                  pl.BlockSpec((seq, LANES), lambda b, h: (b, C_LR // LANES)),
                  pl.BlockSpec((LANES, GLA_DK), lambda b, h: (0, h)),
                  pl.BlockSpec((1, GLA_DK), lambda b, h: (0, h)),
                  pl.BlockSpec((1, GLA_DV), lambda b, h: (0, 0)),
                  full(mcat), full(upper), full(mask)],
        out_specs=pl.BlockSpec((seq, GLA_DV), lambda b, h: (b, h)),
        out_shape=jax.ShapeDtypeStruct((M, H * GLA_DV), BF16),
        scratch_shapes=[pltpu.VMEM((seq, GLA_DK), F32), pltpu.VMEM((GLA_DV, GLA_DK), F32),
                        pltpu.VMEM((GLA_UNROLL * CHUNK, GLA_DV), F32)],
        compiler_params=_cparams(("parallel", "parallel")),
        name="gla",
    )(*([proj] * (n_in + 1)), wa, b_alpha.reshape(1, -1), norm_w.reshape(1, GLA_DV),
      jnp.asarray(mcat, BF16), jnp.asarray(upper), jnp.asarray(mask))


MERGE_ROW_BLOCKS = 4


def _merge_kernel(odn_ref, ogla_ref, pa_ref, pb_ref, gd_ref, gdn_ref, gg_ref, ggn_ref, o_ref):
    def gate(main_ref, next_ref, rows):
        n = main_ref.shape[1] // LANES
        tiles = [main_ref[rows, t * LANES:(t + 1) * LANES] for t in range(n)] + [next_ref[rows, :]]
        return jax.nn.sigmoid(jnp.concatenate(_lane_shift(tiles, GATE_SHIFT), axis=1))

    pa = pa_ref[...].astype(BF16)
    pb = None
    rb = o_ref.shape[0] // MERGE_ROW_BLOCKS
    for r in range(MERGE_ROW_BLOCKS):
        rows = slice(r * rb, (r + 1) * rb)
        ya = _dot(odn_ref[rows, :], pa)
        if pb is None:
            pb = pb_ref[...].astype(BF16)
        yb = _dot(ogla_ref[rows, :], pb)
        o_ref[rows, :] = (gate(gd_ref, gdn_ref, rows) * ya + gate(gg_ref, ggn_ref, rows) * yb).astype(o_ref.dtype)


def _merge(o_dn, o_gla, pa, pb, proj, tm=2048, tn=256):
    M, K = o_dn.shape
    N = pa.shape[1]
    tm = min(tm, M)
    lt = tn // LANES
    main = lambda c0: pl.BlockSpec((tm, tn), lambda i, j: (i, c0 // tn + j))
    nxt = lambda c0: pl.BlockSpec((tm, LANES), lambda i, j: (i, c0 // LANES + (j + 1) * lt))
    assert (C_GATE_DN - GATE_SHIFT) % tn == 0 and (C_GATE_GLA - GATE_SHIFT) % tn == 0
    assert tm % (MERGE_ROW_BLOCKS * 16) == 0
    return pl.pallas_call(
        _merge_kernel,
        grid=(M // tm, N // tn),
        in_specs=[pl.BlockSpec((tm, K), lambda i, j: (i, 0), pipeline_mode=pl.Buffered(1)),
                  pl.BlockSpec((tm, K), lambda i, j: (i, 0), pipeline_mode=pl.Buffered(1)),
                  pl.BlockSpec((K, tn), lambda i, j: (0, j)),
                  pl.BlockSpec((K, tn), lambda i, j: (0, j)),
                  main(C_GATE_DN), nxt(C_GATE_DN), main(C_GATE_GLA), nxt(C_GATE_GLA)],
        out_specs=pl.BlockSpec((tm, tn), lambda i, j: (i, j)),
        out_shape=jax.ShapeDtypeStruct((M, N), BF16),
        compiler_params=_cparams(("parallel", "arbitrary")),
        name="merge",
    )(o_dn, o_gla, pa, pb, proj, proj, proj, proj)


def _mm_res_kernel(a_ref, b_ref, r_ref, o_ref):
    o_ref[...] = r_ref[...] + _dot(a_ref[...], b_ref[...].astype(BF16))


def _matmul_residual(a, b, res, tm=2048, tn=256, name="outproj"):
    M, K = a.shape
    N = b.shape[1]
    tm = min(tm, M)
    return pl.pallas_call(
        _mm_res_kernel,
        grid=(M // tm, N // tn),
        in_specs=[pl.BlockSpec((tm, K), lambda i, j: (i, 0), pipeline_mode=pl.Buffered(1)),
                  pl.BlockSpec((K, tn), lambda i, j: (0, j)),
                  pl.BlockSpec((tm, tn), lambda i, j: (i, j))],
        out_specs=pl.BlockSpec((tm, tn), lambda i, j: (i, j)),
        out_shape=jax.ShapeDtypeStruct((M, N), F32),
        compiler_params=_cparams(("parallel", "arbitrary")),
        name=name,
    )(a, b, res)


FFN_ROW_BLOCKS = 8


def _ffn_in_kernel(h_ref, wg_ref, wu_ref, cw_ref, cb_ref, o_ref):
    wg = wg_ref[...].astype(BF16)
    wu = None
    cw = cw_ref[...]
    cb = cb_ref[...]
    rb = h_ref.shape[0] // FFN_ROW_BLOCKS
    tail = jnp.zeros((HALO, wg.shape[1]), F32)
    for r in range(FFN_ROW_BLOCKS):
        h = h_ref[r * rb:(r + 1) * rb, :]
        gate = _dot(h, wg)
        if wu is None:
            wu = wu_ref[...].astype(BF16)
        up = _dot(h, wu)
        ext = jnp.concatenate([tail, gate], axis=0)
        y = gate * cw[FFN_CONV - 1:FFN_CONV, :] + cb
        for s in range(1, FFN_CONV):
            y = y + pltpu.roll(ext, s, 0)[HALO:, :] * cw[FFN_CONV - 1 - s:FFN_CONV - s, :]
        o_ref[r * rb:(r + 1) * rb, :] = (_silu(y) * up).astype(o_ref.dtype)
        tail = gate[rb - HALO:, :]


def _ffn_in(h, w_in, conv_w, conv_b, seq, tn=256):
    M, K = h.shape
    N = w_in.shape[1] // 2
    nb = N // tn
    assert seq % (FFN_ROW_BLOCKS * HALO) == 0
    return pl.pallas_call(
        _ffn_in_kernel,
        grid=(M // seq, nb),
        in_specs=[pl.BlockSpec((seq, K), lambda i, j: (i, 0), pipeline_mode=pl.Buffered(1)),
                  pl.BlockSpec((K, tn), lambda i, j: (0, j)),
                  pl.BlockSpec((K, tn), lambda i, j: (0, nb + j)),
                  pl.BlockSpec((FFN_CONV, tn), lambda i, j: (0, j)),
                  pl.BlockSpec((1, tn), lambda i, j: (0, j))],
        out_specs=pl.BlockSpec((seq, tn), lambda i, j: (i, j)),
        out_shape=jax.ShapeDtypeStruct((M, N), BF16),
        compiler_params=_cparams(("parallel", "arbitrary")),
        name="ffn_in",
    )(h, w_in, w_in, conv_w, conv_b.reshape(1, N))


def _ffn_out_kernel(a_ref, b_ref, r_ref, o_ref, acc_ref):
    kk = pl.program_id(2)

    @pl.when(kk == 0)
    def _():
        acc_ref[...] = r_ref[...]

    acc_ref[...] += _dot(a_ref[...], b_ref[...])

    @pl.when(kk == pl.num_programs(2) - 1)
    def _():
        o_ref[...] = acc_ref[...]


def _ffn_out(a, b, res, tm=1024, tn=512, ksplit=2):
    M, K = a.shape
    N = b.shape[1]
    tm = min(tm, M)
    tk = K // ksplit
    return pl.pallas_call(
        _ffn_out_kernel,
        grid=(M // tm, N // tn, ksplit),
        in_specs=[pl.BlockSpec((tm, tk), lambda i, j, k: (i, k)),
                  pl.BlockSpec((tk, tn), lambda i, j, k: (k, j)),
                  pl.BlockSpec((tm, tn), lambda i, j, k: (i, j))],
        out_specs=pl.BlockSpec((tm, tn), lambda i, j, k: (i, j)),
        out_shape=jax.ShapeDtypeStruct((M, N), F32),
        scratch_shapes=[pltpu.VMEM((tm, tn), F32)],
        compiler_params=_cparams(("parallel", "arbitrary", "arbitrary")),
        name="ffn_out",
    )(a, b, res)


def _layer(x2d, batch, seq, norm_mix_w, w_in, dn_conv_w, dn_a_log, dn_dt_bias, dn_norm_w,
           gla_w_alpha2, gla_b_alpha, gla_norm_w, w_branch_dn, w_branch_gla, w_out,
           norm_ffn_w, w_ffn_in, ffn_conv_w, ffn_conv_b, w_ffn_out):
    h = _rmsnorm(x2d, norm_mix_w, BF16)
    proj = _inproj(h, w_in.T)
    o_dn = _deltanet(proj, dn_conv_w, dn_a_log, dn_dt_bias, dn_norm_w, batch, seq)
    o_gla = _gla(proj, gla_w_alpha2, gla_b_alpha, gla_norm_w, batch, seq)
    merged = _merge(o_dn, o_gla, w_branch_dn, w_branch_gla, proj)
    x1 = _matmul_residual(merged, w_out, x2d)

    h2 = _rmsnorm(x1, norm_ffn_w, BF16)
    act = _ffn_in(h2, w_ffn_in, ffn_conv_w, ffn_conv_b, seq)
    return _ffn_out(act, w_ffn_out.astype(BF16), x1)


def kernel(x, norm_mix_w, w_in, dn_conv_w, dn_a_log, dn_dt_bias, dn_norm_w, gla_w_alpha2, gla_b_alpha, gla_norm_w, w_branch_dn, w_branch_gla, w_out, norm_ffn_w, w_ffn_in, ffn_conv_w, ffn_conv_b, w_ffn_out, norm_final_w):
    batch, seq, D = x.shape
    x2d = x.reshape(batch * seq, D)
    for l in range(norm_mix_w.shape[0]):
        x2d = _layer(x2d, batch, seq, norm_mix_w[l], w_in[l], dn_conv_w[l], dn_a_log[l], dn_dt_bias[l],
                     dn_norm_w[l], gla_w_alpha2[l], gla_b_alpha[l], gla_norm_w[l], w_branch_dn[l],
                     w_branch_gla[l], w_out[l], norm_ffn_w[l], w_ffn_in[l], ffn_conv_w[l], ffn_conv_b[l],
                     w_ffn_out[l])
    return _rmsnorm(x2d, norm_final_w, F32).reshape(batch, seq, D)
```

```python
import functools

import numpy as np
import jax
import jax.numpy as jnp
from jax import lax
from jax.experimental import pallas as pl
from jax.experimental.pallas import tpu as pltpu

F32 = jnp.float32
BF16 = jnp.bfloat16
HI = lax.Precision.HIGHEST

EPS = 1e-6
CHUNK = 64
DN_HEADS = 16
DN_D = 128
DN_CONV = 4
GLA_HEADS = 8
GLA_DK = 128
GLA_DV = 256
GLA_RANK = 16
GLA_TAU = 16.0
FFN_CONV = 3
LANES = 128
HALO = 8

C_DQ, C_DK, C_DV, C_DZ = 0, 2048, 4096, 6144
C_DGATE = 8192
C_GQ, C_GK, C_GV, C_GR = 8224, 9248, 10272, 12320
C_LR = 14368
C_GATE_DN, C_GATE_GLA = 14384, 18480
S_B, S_A = 0, 16
S_LR = C_LR % LANES
GLA_SHIFT = C_GQ % LANES
GATE_SHIFT = C_GATE_DN % LANES
assert C_GK % LANES == C_GV % LANES == C_GR % LANES == GLA_SHIFT and C_GATE_GLA % LANES == GATE_SHIFT

VMEM_LIMIT = 56 * 1024 * 1024


def _cparams(sem):
    return pltpu.CompilerParams(dimension_semantics=sem, vmem_limit_bytes=VMEM_LIMIT)


def _silu(x):
    return x * jax.nn.sigmoid(x)


def _softplus(x):
    return jnp.maximum(x, 0.0) + jnp.log1p(jnp.exp(-jnp.abs(x)))


def _dot(a, b, precision=None):
    return jnp.dot(a, b, precision=precision, preferred_element_type=F32)


def _dot_nt(a, b, precision=None):
    return lax.dot_general(a, b, (((1,), (1,)), ((), ())), precision=precision, preferred_element_type=F32)


def _dot_tn(a, b, precision=None):
    return lax.dot_general(a, b, (((0,), (0,)), ((), ())), precision=precision, preferred_element_type=F32)


def _rmsnorm_kernel(x_ref, w_ref, o_ref):
    x = x_ref[...]
    ms = jnp.mean(x * x, axis=-1, keepdims=True)
    o_ref[...] = (x * lax.rsqrt(ms + EPS) * w_ref[...]).astype(o_ref.dtype)


def _rmsnorm(x, w, out_dtype, tm=256):
    M, D = x.shape
    tm = min(tm, M)
    return pl.pallas_call(
        _rmsnorm_kernel,
        grid=(M // tm,),
        in_specs=[pl.BlockSpec((tm, D), lambda i: (i, 0)), pl.BlockSpec((1, D), lambda i: (0, 0))],
        out_specs=pl.BlockSpec((tm, D), lambda i: (i, 0)),
        out_shape=jax.ShapeDtypeStruct((M, D), out_dtype),
        compiler_params=_cparams(("parallel",)),
        name="rmsnorm",
    )(x, w.reshape(1, D))


def _inproj_kernel(a_ref, wt_ref, o_ref):
    hn = wt_ref.shape[0] // 2
    for c in (slice(0, hn), slice(hn, 2 * hn)):
        o_ref[:, c] = _dot_nt(a_ref[...], wt_ref[c, :].astype(BF16))


def _inproj(a, wt, tm=2048, tn=512):
    M, K = a.shape
    N = wt.shape[0]
    tm = min(tm, M)
    return pl.pallas_call(
        _inproj_kernel,
        grid=(M // tm, pl.cdiv(N, tn)),
        in_specs=[pl.BlockSpec((tm, K), lambda i, j: (i, 0), pipeline_mode=pl.Buffered(1)),
                  pl.BlockSpec((tn, K), lambda i, j: (j, 0))],
        out_specs=pl.BlockSpec((tm, tn), lambda i, j: (i, j)),
        out_shape=jax.ShapeDtypeStruct((M, N), F32),
        compiler_params=_cparams(("parallel", "arbitrary")),
        name="inproj",
    )(a, wt)


def _lane_shift(tiles, shift):
    lane = lax.broadcasted_iota(jnp.int32, tiles[0].shape, 1)
    rolled = [pltpu.roll(t, LANES - shift, 1) for t in tiles]
    return [jnp.where(lane < LANES - shift, rolled[i], rolled[i + 1]) for i in range(len(tiles) - 1)]


def _l2norm(x):
    return x * lax.rsqrt(jnp.sum(x * x, axis=-1, keepdims=True) + EPS)


def _split3(x):
    hi = x.astype(BF16)
    r = x - hi.astype(F32)
    mid = r.astype(BF16)
    lo = (r - mid.astype(F32)).astype(BF16)
    return hi, mid, lo


def _split2(x):
    hi = x.astype(BF16)
    return hi, (x - hi.astype(F32)).astype(BF16)


DN_UNROLL = 8
DN_P_PIECES = 4
DN_P_PER_STAGE = 2


def _dn_kernel(q_ref, k_ref, v_ref, z_ref, small_ref, cwq_ref, cwk_ref, cwv_ref, alog_ref, dtb_ref, nw_ref,
               o_ref, gates_s, qs_s, k_s, v_s, beta_s, bc_s, lhs_s, add_s, ebl_s, *, seq):
    h = pl.program_id(1)
    C = CHUNK
    D = DN_D
    U = range(DN_UNROLL)
    n_groups = seq // (C * DN_UNROLL)
    row = lax.broadcasted_iota(jnp.int32, (seq, D), 0)

    @pl.when(h == 0)
    def _():
        small = small_ref[...]
        lane = lax.broadcasted_iota(jnp.int32, small.shape, 1)
        x = -jnp.exp(alog_ref[...]) * _softplus(small + dtb_ref[...])
        in_chunk = row % C
        s = 1
        while s < C:
            x = x + jnp.where(in_chunk >= s, pltpu.roll(x, s, 0), 0.0)
            s *= 2
        gates_s[...] = jnp.where(lane < S_A, jax.nn.sigmoid(small), x)

    sr = lax.broadcasted_iota(jnp.int32, (3 * LANES, 2 * LANES), 0) % LANES
    sc = lax.broadcasted_iota(jnp.int32, (3 * LANES, 2 * LANES), 1)
    sel = (sr == jnp.where(sc < LANES, h + S_B, h + S_A)).astype(BF16)
    G = C * DN_UNROLL

    def pass_p(g):
        rep = _dot(jnp.concatenate(_split3(gates_s[pl.ds(pl.multiple_of(g * G, G), G), :]), axis=1), sel)
        beta_s[pl.ds(pl.multiple_of(g * G, G), G), :] = rep[:, :LANES]
        bc_s[pl.ds(pl.multiple_of(g * G, G), G), :] = rep[:, LANES:]
        yield
        R = G // DN_P_PIECES
        for piece in range(DN_P_PIECES):
            r0 = g * G + piece * R
            rows = pl.ds(pl.multiple_of(r0, R), R)
            halo_rows = pl.ds(pl.multiple_of(jnp.maximum(r0 - HALO, 0), HALO), HALO)
            for x_ref, cw_ref, out_s, norm in ((q_ref, cwq_ref, qs_s, D ** -0.5), (k_ref, cwk_ref, k_s, 1.0),
                                               (v_ref, cwv_ref, v_s, None)):
                cw = cw_ref[...]
                halo = jnp.where(r0 == 0, 0.0, x_ref[halo_rows, :])
                x = x_ref[rows, :]
                ext = jnp.concatenate([halo, x], axis=0)
                y = x * cw[DN_CONV - 1:DN_CONV, :]
                for s in range(1, DN_CONV):
                    y = y + pltpu.roll(ext, s, 0)[HALO:, :] * cw[DN_CONV - 1 - s:DN_CONV - s, :]
                y = _silu(y)
                out_s[rows, :] = y if norm is None else _l2norm(y) * norm
                yield

    ri = lax.broadcasted_iota(jnp.int32, (C, C), 0)
    ci = lax.broadcasted_iota(jnp.int32, (C, C), 1)
    tril = ri >= ci
    strict = ri > ci
    eye = (ri == ci).astype(F32)
    nw = nw_ref[...]

    def pass_a(g):
        ns = [g * DN_UNROLL + u for u in U]
        sls = [pl.ds(pl.multiple_of(n * C, C), C) for n in ns]
        qc = [qs_s[s, :] for s in sls]
        kc = [k_s[s, :] for s in sls]
        vc = [v_s[s, :] for s in sls]
        bt = [beta_s[s, :] for s in sls]
        bc = [bc_s[s, :] for s in sls]
        kb = [kc[u] * bt[u] for u in U]
        gram = [_dot_nt(jnp.concatenate([kb[u], qc[u]], axis=0).astype(BF16), kc[u].astype(BF16)) for u in U]
        yield
        dm = [bc[u][:, :C] - jnp.transpose(bc[u])[:C, :] for u in U]
        decay = [jnp.where(tril, jnp.exp(jnp.minimum(dm[u], 0.0)), 0.0) for u in U]
        eb = [jnp.exp(bc[u]) for u in U]
        a = [jnp.where(strict, gram[u][:C] * decay[u], 0.0) for u in U]
        att = [(gram[u][C:] * decay[u]).astype(BF16) for u in U]
        p = [eye - a[u] for u in U]
        a_bf = [a[u].astype(BF16) for u in U]
        x = [_dot(a_bf[u], a_bf[u]) for u in U]
        yield
        for _ in range(4):
            px = [_dot(jnp.concatenate([p[u], x[u]], axis=0).astype(BF16), x[u].astype(BF16)) for u in U]
            p = [p[u] + px[u][:C] for u in U]
            x = [px[u][C:] for u in U]
            yield
        px = [_dot(p[u].astype(BF16), x[u].astype(BF16)) for u in U]
        p = [(p[u] + px[u]).astype(BF16) for u in U]
        yield
        rhs = [jnp.concatenate([kb[u] * eb[u], vc[u] * bt[u]], axis=1).astype(BF16) for u in U]
        wu = [_dot(p[u], rhs[u]).astype(BF16) for u in U]
        yield
        kd = [(kc[u] * jnp.exp(bc[u][C - 1:C, :] - bc[u])).astype(BF16) for u in U]
        aw = [_dot(att[u], wu[u]) for u in U]
        yield
        mn = [_dot_tn(kd[u], wu[u]) for u in U]
        for u in U:
            n = ns[u]
            lhs_s[n, 0:D, :] = mn[u][:, :D].astype(BF16)
            lhs_s[n, D:D + C, :] = (qc[u] * eb[u] - aw[u][:, :D]).astype(BF16)
            add_s[n, 0:D, :] = mn[u][:, D:]
            add_s[n, D:D + C, :] = aw[u][:, D:]
            ebl_s[n] = jnp.broadcast_to(eb[u][C - 1:C, :], (8, LANES))
        yield

    def finish(n, o):
        sl = pl.ds(pl.multiple_of(n * C, C), C)
        y = o * lax.rsqrt(jnp.mean(o * o, axis=-1, keepdims=True) + EPS) * nw * _silu(z_ref[sl, :])
        o_ref[sl, :] = y.astype(o_ref.dtype)

    def step_b(n, carry):
        S, o_prev = carry
        finish(jnp.maximum(n - 1, 0), o_prev)
        ps = _dot(lhs_s[n], S.astype(BF16))
        add = add_s[n]
        o = ps[D:] + add[D:]
        S = ebl_s[n][0:1, :] * S + (add[:D] - ps[:D])
        return S, o

    def interleave(g_a, g_b, g_p, carry):
        stages = pass_a(g_a) if g_a is not None else iter(())
        steps = [g_b * DN_UNROLL + u for u in U] if g_b is not None else []
        pieces = pass_p(g_p) if g_p is not None else iter(())
        next(stages, None)
        for _ in stages:
            if steps:
                carry = step_b(steps.pop(0), carry)
            for _ in range(DN_P_PER_STAGE):
                next(pieces, None)
        for n in steps:
            carry = step_b(n, carry)
        for _ in pieces:
            pass
        return carry

    assert n_groups >= 2
    carry = (jnp.zeros((D, D), F32), jnp.zeros((C, D), F32))
    for _ in pass_p(0):
        pass
    carry = interleave(0, None, 1, carry)
    carry = lax.fori_loop(1, n_groups - 1, lambda g, c: interleave(g, g - 1, g + 1, c), carry)
    carry = interleave(n_groups - 1, n_groups - 2, None, carry)
    _, o_last = interleave(None, n_groups - 1, None, carry)
    finish(seq // C - 1, o_last)


def _deltanet(proj, conv_w, a_log, dt_bias, norm_w, batch, seq):
    M = proj.shape[0]
    H = DN_HEADS
    n_chunks = seq // CHUNK
    assert seq % (CHUNK * DN_UNROLL) == 0
    blk = lambda c0: pl.BlockSpec((seq, DN_D), lambda b, h: (b, c0 // DN_D + h))
    cw = lambda c0: pl.BlockSpec((DN_CONV, DN_D), lambda b, h: (0, c0 // DN_D + h))
    row = pl.BlockSpec((1, LANES), lambda b, h: (0, 0))
    alog = jnp.zeros((1, LANES), F32).at[0, S_A:S_A + H].set(a_log)
    dtb = jnp.zeros((1, LANES), F32).at[0, S_A:S_A + H].set(dt_bias)
    seq_f32 = pltpu.VMEM((seq, DN_D), F32)
    return pl.pallas_call(
        functools.partial(_dn_kernel, seq=seq),
        grid=(batch, H),
        in_specs=[blk(C_DQ), blk(C_DK), blk(C_DV), blk(C_DZ),
                  pl.BlockSpec((seq, LANES), lambda b, h: (b, C_DGATE // LANES)),
                  cw(0), cw(2048), cw(4096), row, row, row],
        out_specs=pl.BlockSpec((seq, DN_D), lambda b, h: (b, h)),
        out_shape=jax.ShapeDtypeStruct((M, H * DN_D), BF16),
        scratch_shapes=[seq_f32] * 6 + [pltpu.VMEM((n_chunks, DN_D + CHUNK, DN_D), BF16),
                                        pltpu.VMEM((n_chunks, DN_D + CHUNK, DN_D), F32),
                                        pltpu.VMEM((n_chunks, 8, LANES), F32)],
        compiler_params=_cparams(("parallel", "arbitrary")),
        name="deltanet",
    )(proj, proj, proj, proj, proj, conv_w, conv_w, conv_w, alog, dtb, norm_w.reshape(1, DN_D))


GLA_LEVELS = (32, 16, 8, 4, 2, 1)
GLA_UNROLL = 4


def _gla_constants():
    C = CHUNK
    i = np.arange(C)[:, None]
    m = np.arange(C)[None, :]
    blocks = [(m <= i), (m > i)]
    upper_rows, masks = [], []
    for s in GLA_LEVELS:
        p = (i // (2 * s)) * (2 * s)
        m0 = p + s - 1
        up = (i - p) >= s
        blocks.append(np.where(up, (m > m0) & (m <= i), (m > i) & (m <= m0)))
        upper_rows.append(np.broadcast_to(up, (C, LANES)))
        j = m
        pj = (j // (2 * s)) * (2 * s)
        masks.append(up & ((j - pj) < s) & (p == pj))
    masks.append(i == m)
    mcat = np.concatenate(blocks, axis=0).astype(np.float32)
    upper = np.stack(upper_rows).astype(np.float32)
    mask = np.stack(masks).astype(np.float32)
    return mcat, upper, mask


N_QK_TILES = GLA_DK // LANES + 1
N_VR_TILES = GLA_DV // LANES + 1


def _gla_kernel(*refs, seq):
    q_t, refs = refs[:N_QK_TILES], refs[N_QK_TILES:]
    k_t, refs = refs[:N_QK_TILES], refs[N_QK_TILES:]
    v_t, refs = refs[:N_VR_TILES], refs[N_VR_TILES:]
    r_t, refs = refs[:N_VR_TILES], refs[N_VR_TILES:]
    lr_ref, wa_ref, ba_ref, nw_ref, mcat_ref, upper_ref, mask_ref, o_ref, gk_s, st_s, o_s = refs
    C = CHUNK

    def rows(tile_refs, s):
        return jnp.concatenate(_lane_shift([t[s, :] for t in tile_refs], GLA_SHIFT), axis=1)

    x_alpha = _dot(lr_ref[...], wa_ref[...], HI) + ba_ref[...]
    gk_s[...] = -_softplus(-x_alpha) / GLA_TAU
    st_s[...] = jnp.zeros_like(st_s)
    mcat = mcat_ref[...]
    nw = nw_ref[...]
    n_lvl = len(GLA_LEVELS)

    U = range(GLA_UNROLL)
    o_s[...] = jnp.zeros_like(o_s)

    def finish(g):
        for u in U:
            sl = pl.ds(pl.multiple_of((g * GLA_UNROLL + u) * C, C), C)
            o = o_s[u * C:(u + 1) * C, :]
            y = o * lax.rsqrt(jnp.mean(o * o, axis=-1, keepdims=True) + EPS) * nw * _silu(rows(r_t, sl))
            o_ref[sl, :] = y.astype(o_ref.dtype)

    def body(g, carry):
        finish(jnp.maximum(g - 1, 0))
        sls = [pl.ds(pl.multiple_of((g * GLA_UNROLL + u) * C, C), C) for u in U]
        qc = [rows(q_t, s) * (GLA_DK ** -0.5) for s in sls]
        kc = [rows(k_t, s) for s in sls]
        v_bf = [rows(v_t, s).astype(BF16) for s in sls]
        e = [_dot(mcat, jnp.concatenate(_split2(gk_s[s, :]), axis=1)) for s in sls]
        ex = [jnp.exp(e[u][:, :GLA_DK] + e[u][:, GLA_DK:]) for u in U]
        k_bf = [kc[u].astype(BF16) for u in U]
        att = [mask_ref[n_lvl] * _dot_nt(qc[u].astype(BF16), k_bf[u]) for u in U]
        for l in range(n_lvl):
            up = upper_ref[l] > 0.5
            zl = [(jnp.where(up, qc[u], kc[u]) * ex[u][(2 + l) * C:(3 + l) * C, :]).astype(BF16) for u in U]
            gl = [_dot_nt(zl[u], zl[u]) for u in U]
            att = [att[u] + mask_ref[l] * gl[u] for u in U]
        ds = [_dot_tn(v_bf[u], (kc[u] * ex[u][C:2 * C, :]).astype(BF16)) for u in U]
        ov = [_dot(att[u].astype(BF16), v_bf[u]) for u in U]
        st = st_s[...]
        for u in U:
            eb = ex[u][0:C, :]
            o_s[u * C:(u + 1) * C, :] = _dot_nt((qc[u] * eb).astype(BF16), st.astype(BF16)) + ov[u]
            st = st * eb[C - 1:C, :] + ds[u]
        st_s[...] = st
        return carry

    n_groups = seq // (C * GLA_UNROLL)
    lax.fori_loop(0, n_groups, body, 0)
    finish(n_groups - 1)


def _gla(proj, w_alpha2, b_alpha, norm_w, batch, seq):
    M = proj.shape[0]
    H = GLA_HEADS
    assert seq % (CHUNK * GLA_UNROLL) == 0
    mcat, upper, mask = _gla_constants()
    wa = jnp.zeros((LANES, H * GLA_DK), F32).at[S_LR:S_LR + GLA_RANK, :].set(w_alpha2)
    full = lambda a: pl.BlockSpec(a.shape, lambda b, h: (0,) * a.ndim)

    def tiles(c0, width, n):
        return [pl.BlockSpec((seq, LANES), lambda b, h, t=t: (b, c0 // LANES + h * (width // LANES) + t))
                for t in range(n)]

    n_in = 2 * N_QK_TILES + 2 * N_VR_TILES
    return pl.pallas_call(
        functools.partial(_gla_kernel, seq=seq),
        grid=(batch, H),
        in_specs=[*tiles(C_GQ, GLA_DK, N_QK_TILES), *tiles(C_GK, GLA_DK, N_QK_TILES),
                  *tiles(C_GV, GLA_DV, N_VR_TILES), *tiles(C_GR, GLA_DV, N_VR_TILES),
                  pl.BlockSpec((seq, LANES), lambda b, h: (b, C_LR // LANES)),
                  pl.BlockSpec((LANES, GLA_DK), lambda b, h: (0, h)),
                  pl.BlockSpec((1, GLA_DK), lambda b, h: (0, h)),
                  pl.BlockSpec((1, GLA_DV), lambda b, h: (0, 0)),
                  full(mcat), full(upper), full(mask)],
        out_specs=pl.BlockSpec((seq, GLA_DV), lambda b, h: (b, h)),
        out_shape=jax.ShapeDtypeStruct((M, H * GLA_DV), BF16),
        scratch_shapes=[pltpu.VMEM((seq, GLA_DK), F32), pltpu.VMEM((GLA_DV, GLA_DK), F32),
                        pltpu.VMEM((GLA_UNROLL * CHUNK, GLA_DV), F32)],
        compiler_params=_cparams(("parallel", "parallel")),
        name="gla",
    )(*([proj] * (n_in + 1)), wa, b_alpha.reshape(1, -1), norm_w.reshape(1, GLA_DV),
      jnp.asarray(mcat, BF16), jnp.asarray(upper), jnp.asarray(mask))


MERGE_ROW_BLOCKS = 4


def _merge_kernel(odn_ref, ogla_ref, pa_ref, pb_ref, gd_ref, gdn_ref, gg_ref, ggn_ref, o_ref):
    def gate(main_ref, next_ref, rows):
        n = main_ref.shape[1] // LANES
        tiles = [main_ref[rows, t * LANES:(t + 1) * LANES] for t in range(n)] + [next_ref[rows, :]]
        return jax.nn.sigmoid(jnp.concatenate(_lane_shift(tiles, GATE_SHIFT), axis=1))

    pa = pa_ref[...].astype(BF16)
    pb = None
    rb = o_ref.shape[0] // MERGE_ROW_BLOCKS
    for r in range(MERGE_ROW_BLOCKS):
        rows = slice(r * rb, (r + 1) * rb)
        ya = _dot(odn_ref[rows, :], pa)
        if pb is None:
            pb = pb_ref[...].astype(BF16)
        yb = _dot(ogla_ref[rows, :], pb)
        o_ref[rows, :] = (gate(gd_ref, gdn_ref, rows) * ya + gate(gg_ref, ggn_ref, rows) * yb).astype(o_ref.dtype)


def _merge(o_dn, o_gla, pa, pb, proj, tm=2048, tn=256):
    M, K = o_dn.shape
    N = pa.shape[1]
    tm = min(tm, M)
    lt = tn // LANES
    main = lambda c0: pl.BlockSpec((tm, tn), lambda i, j: (i, c0 // tn + j))
    nxt = lambda c0: pl.BlockSpec((tm, LANES), lambda i, j: (i, c0 // LANES + (j + 1) * lt))
    assert (C_GATE_DN - GATE_SHIFT) % tn == 0 and (C_GATE_GLA - GATE_SHIFT) % tn == 0
    assert tm % (MERGE_ROW_BLOCKS * 16) == 0
    return pl.pallas_call(
        _merge_kernel,
        grid=(M // tm, N // tn),
        in_specs=[pl.BlockSpec((tm, K), lambda i, j: (i, 0), pipeline_mode=pl.Buffered(1)),
                  pl.BlockSpec((tm, K), lambda i, j: (i, 0), pipeline_mode=pl.Buffered(1)),
                  pl.BlockSpec((K, tn), lambda i, j: (0, j)),
                  pl.BlockSpec((K, tn), lambda i, j: (0, j)),
                  main(C_GATE_DN), nxt(C_GATE_DN), main(C_GATE_GLA), nxt(C_GATE_GLA)],
        out_specs=pl.BlockSpec((tm, tn), lambda i, j: (i, j)),
        out_shape=jax.ShapeDtypeStruct((M, N), BF16),
        compiler_params=_cparams(("parallel", "arbitrary")),
        name="merge",
    )(o_dn, o_gla, pa, pb, proj, proj, proj, proj)


def _mm_res_kernel(a_ref, b_ref, r_ref, o_ref):
    o_ref[...] = r_ref[...] + _dot(a_ref[...], b_ref[...].astype(BF16))


def _matmul_residual(a, b, res, tm=1024, tn=512, name="outproj"):
    M, K = a.shape
    N = b.shape[1]
    tm = min(tm, M)
    return pl.pallas_call(
        _mm_res_kernel,
        grid=(M // tm, N // tn),
        in_specs=[pl.BlockSpec((tm, K), lambda i, j: (i, 0)),
                  pl.BlockSpec((K, tn), lambda i, j: (0, j)),
                  pl.BlockSpec((tm, tn), lambda i, j: (i, j))],
        out_specs=pl.BlockSpec((tm, tn), lambda i, j: (i, j)),
        out_shape=jax.ShapeDtypeStruct((M, N), F32),
        compiler_params=_cparams(("parallel", "arbitrary")),
        name=name,
    )(a, b, res)


FFN_ROW_BLOCKS = 4


def _ffn_in_kernel(h_ref, wg_ref, wu_ref, cw_ref, cb_ref, o_ref):
    wg = wg_ref[...].astype(BF16)
    wu = None
    cw = cw_ref[...]
    cb = cb_ref[...]
    rb = h_ref.shape[0] // FFN_ROW_BLOCKS
    tail = jnp.zeros((HALO, wg.shape[1]), F32)
    for r in range(FFN_ROW_BLOCKS):
        h = h_ref[r * rb:(r + 1) * rb, :]
        gate = _dot(h, wg)
        if wu is None:
            wu = wu_ref[...].astype(BF16)
        up = _dot(h, wu)
        ext = jnp.concatenate([tail, gate], axis=0)
        y = gate * cw[FFN_CONV - 1:FFN_CONV, :] + cb
        for s in range(1, FFN_CONV):
            y = y + pltpu.roll(ext, s, 0)[HALO:, :] * cw[FFN_CONV - 1 - s:FFN_CONV - s, :]
        o_ref[r * rb:(r + 1) * rb, :] = (_silu(y) * up).astype(o_ref.dtype)
        tail = gate[rb - HALO:, :]


def _ffn_in(h, w_in, conv_w, conv_b, seq, tn=256):
    M, K = h.shape
    N = w_in.shape[1] // 2
    nb = N // tn
    assert seq % (FFN_ROW_BLOCKS * HALO) == 0
    return pl.pallas_call(
        _ffn_in_kernel,
        grid=(M // seq, nb),
        in_specs=[pl.BlockSpec((seq, K), lambda i, j: (i, 0), pipeline_mode=pl.Buffered(1)),
                  pl.BlockSpec((K, tn), lambda i, j: (0, j)),
                  pl.BlockSpec((K, tn), lambda i, j: (0, nb + j)),
                  pl.BlockSpec((FFN_CONV, tn), lambda i, j: (0, j)),
                  pl.BlockSpec((1, tn), lambda i, j: (0, j))],
        out_specs=pl.BlockSpec((seq, tn), lambda i, j: (i, j)),
        out_shape=jax.ShapeDtypeStruct((M, N), BF16),
        compiler_params=_cparams(("parallel", "arbitrary")),
        name="ffn_in",
    )(h, w_in, w_in, conv_w, conv_b.reshape(1, N))


def _ffn_out_kernel(a_ref, b_ref, r_ref, o_ref, acc_ref):
    kk = pl.program_id(2)

    @pl.when(kk == 0)
    def _():
        acc_ref[...] = r_ref[...]

    acc_ref[...] += _dot(a_ref[...], b_ref[...])

    @pl.when(kk == pl.num_programs(2) - 1)
    def _():
        o_ref[...] = acc_ref[...]


def _ffn_out(a, b, res, tm=1024, tn=512, ksplit=2):
    M, K = a.shape
    N = b.shape[1]
    tm = min(tm, M)
    tk = K // ksplit
    return pl.pallas_call(
        _ffn_out_kernel,
        grid=(M // tm, N // tn, ksplit),
        in_specs=[pl.BlockSpec((tm, tk), lambda i, j, k: (i, k)),
                  pl.BlockSpec((tk, tn), lambda i, j, k: (k, j)),
                  pl.BlockSpec((tm, tn), lambda i, j, k: (i, j))],
        out_specs=pl.BlockSpec((tm, tn), lambda i, j, k: (i, j)),
        out_shape=jax.ShapeDtypeStruct((M, N), F32),
        scratch_shapes=[pltpu.VMEM((tm, tn), F32)],
        compiler_params=_cparams(("parallel", "arbitrary", "arbitrary")),
        name="ffn_out",
    )(a, b, res)


def _layer(x2d, batch, seq, norm_mix_w, w_in, dn_conv_w, dn_a_log, dn_dt_bias, dn_norm_w,
           gla_w_alpha2, gla_b_alpha, gla_norm_w, w_branch_dn, w_branch_gla, w_out,
           norm_ffn_w, w_ffn_in, ffn_conv_w, ffn_conv_b, w_ffn_out):
    h = _rmsnorm(x2d, norm_mix_w, BF16)
    proj = _inproj(h, w_in.T)
    o_dn = _deltanet(proj, dn_conv_w, dn_a_log, dn_dt_bias, dn_norm_w, batch, seq)
    o_gla = _gla(proj, gla_w_alpha2, gla_b_alpha, gla_norm_w, batch, seq)
    merged = _merge(o_dn, o_gla, w_branch_dn, w_branch_gla, proj)
    x1 = _matmul_residual(merged, w_out, x2d)

    h2 = _rmsnorm(x1, norm_ffn_w, BF16)
    act = _ffn_in(h2, w_ffn_in, ffn_conv_w, ffn_conv_b, seq)
    return _ffn_out(act, w_ffn_out.astype(BF16), x1)


def kernel(x, norm_mix_w, w_in, dn_conv_w, dn_a_log, dn_dt_bias, dn_norm_w, gla_w_alpha2, gla_b_alpha, gla_norm_w, w_branch_dn, w_branch_gla, w_out, norm_ffn_w, w_ffn_in, ffn_conv_w, ffn_conv_b, w_ffn_out, norm_final_w):
    batch, seq, D = x.shape
    x2d = x.reshape(batch * seq, D)
    for l in range(norm_mix_w.shape[0]):
        x2d = _layer(x2d, batch, seq, norm_mix_w[l], w_in[l], dn_conv_w[l], dn_a_log[l], dn_dt_bias[l],
                     dn_norm_w[l], gla_w_alpha2[l], gla_b_alpha[l], gla_norm_w[l], w_branch_dn[l],
                     w_branch_gla[l], w_out[l], norm_ffn_w[l], w_ffn_in[l], ffn_conv_w[l], ffn_conv_b[l],
                     w_ffn_out[l])
    return _rmsnorm(x2d, norm_final_w, F32).reshape(batch, seq, D)
```

```python
import functools

import numpy as np
import jax
import jax.numpy as jnp
from jax import lax
from jax.experimental import pallas as pl
from jax.experimental.pallas import tpu as pltpu

F32 = jnp.float32
BF16 = jnp.bfloat16
HI = lax.Precision.HIGHEST

EPS = 1e-6
CHUNK = 64
DN_HEADS = 16
DN_D = 128
DN_CONV = 4
GLA_HEADS = 8
GLA_DK = 128
GLA_DV = 256
GLA_RANK = 16
GLA_TAU = 16.0
FFN_CONV = 3
LANES = 128
HALO = 8

C_DQ, C_DK, C_DV, C_DZ = 0, 2048, 4096, 6144
C_DGATE = 8192
C_GQ, C_GK, C_GV, C_GR = 8224, 9248, 10272, 12320
C_LR = 14368
C_GATE_DN, C_GATE_GLA = 14384, 18480
S_B, S_A = 0, 16
S_LR = C_LR % LANES
GLA_SHIFT = C_GQ % LANES
GATE_SHIFT = C_GATE_DN % LANES
assert C_GK % LANES == C_GV % LANES == C_GR % LANES == GLA_SHIFT and C_GATE_GLA % LANES == GATE_SHIFT

VMEM_LIMIT = 56 * 1024 * 1024


def _cparams(sem):
    return pltpu.CompilerParams(dimension_semantics=sem, vmem_limit_bytes=VMEM_LIMIT)


def _silu(x):
    return x * jax.nn.sigmoid(x)


def _softplus(x):
    return jnp.maximum(x, 0.0) + jnp.log1p(jnp.exp(-jnp.abs(x)))


def _dot(a, b, precision=None):
    return jnp.dot(a, b, precision=precision, preferred_element_type=F32)


def _dot_nt(a, b, precision=None):
    return lax.dot_general(a, b, (((1,), (1,)), ((), ())), precision=precision, preferred_element_type=F32)


def _dot_tn(a, b, precision=None):
    return lax.dot_general(a, b, (((0,), (0,)), ((), ())), precision=precision, preferred_element_type=F32)


def _rmsnorm_kernel(x_ref, w_ref, o_ref):
    x = x_ref[...]
    ms = jnp.mean(x * x, axis=-1, keepdims=True)
    o_ref[...] = (x * lax.rsqrt(ms + EPS) * w_ref[...]).astype(o_ref.dtype)


def _rmsnorm(x, w, out_dtype, tm=512):
    M, D = x.shape
    tm = min(tm, M)
    return pl.pallas_call(
        _rmsnorm_kernel,
        grid=(M // tm,),
        in_specs=[pl.BlockSpec((tm, D), lambda i: (i, 0)), pl.BlockSpec((1, D), lambda i: (0, 0))],
        out_specs=pl.BlockSpec((tm, D), lambda i: (i, 0)),
        out_shape=jax.ShapeDtypeStruct((M, D), out_dtype),
        compiler_params=_cparams(("parallel",)),
        name="rmsnorm",
    )(x, w.reshape(1, D))


def _inproj_kernel(a_ref, wt_ref, o_ref):
    hn = wt_ref.shape[0] // 2
    for c in (slice(0, hn), slice(hn, 2 * hn)):
        o_ref[:, c] = _dot_nt(a_ref[...], wt_ref[c, :].astype(BF16))


def _inproj(a, wt, tm=2048, tn=512):
    M, K = a.shape
    N = wt.shape[0]
    tm = min(tm, M)
    return pl.pallas_call(
        _inproj_kernel,
        grid=(M // tm, pl.cdiv(N, tn)),
        in_specs=[pl.BlockSpec((tm, K), lambda i, j: (i, 0), pipeline_mode=pl.Buffered(1)),
                  pl.BlockSpec((tn, K), lambda i, j: (j, 0))],
        out_specs=pl.BlockSpec((tm, tn), lambda i, j: (i, j)),
        out_shape=jax.ShapeDtypeStruct((M, N), F32),
        compiler_params=_cparams(("parallel", "arbitrary")),
        name="inproj",
    )(a, wt)


def _lane_shift(tiles, shift):
    lane = lax.broadcasted_iota(jnp.int32, tiles[0].shape, 1)
    rolled = [pltpu.roll(t, LANES - shift, 1) for t in tiles]
    return [jnp.where(lane < LANES - shift, rolled[i], rolled[i + 1]) for i in range(len(tiles) - 1)]


def _l2norm(x):
    return x * lax.rsqrt(jnp.sum(x * x, axis=-1, keepdims=True) + EPS)


def _split3(x):
    hi = x.astype(BF16)
    r = x - hi.astype(F32)
    mid = r.astype(BF16)
    lo = (r - mid.astype(F32)).astype(BF16)
    return hi, mid, lo


def _split2(x):
    hi = x.astype(BF16)
    return hi, (x - hi.astype(F32)).astype(BF16)


DN_UNROLL = 8
DN_P_PIECES = 4
DN_P_PER_STAGE = 2


def _dn_kernel(q_ref, k_ref, v_ref, z_ref, small_ref, cwq_ref, cwk_ref, cwv_ref, alog_ref, dtb_ref, nw_ref,
               o_ref, gates_s, qs_s, k_s, v_s, beta_s, bc_s, lhs_s, add_s, ebl_s, *, seq):
    h = pl.program_id(1)
    C = CHUNK
    D = DN_D
    U = range(DN_UNROLL)
    n_groups = seq // (C * DN_UNROLL)
    row = lax.broadcasted_iota(jnp.int32, (seq, D), 0)

    @pl.when(h == 0)
    def _():
        small = small_ref[...]
        lane = lax.broadcasted_iota(jnp.int32, small.shape, 1)
        x = -jnp.exp(alog_ref[...]) * _softplus(small + dtb_ref[...])
        in_chunk = row % C
        s = 1
        while s < C:
            x = x + jnp.where(in_chunk >= s, pltpu.roll(x, s, 0), 0.0)
            s *= 2
        gates_s[...] = jnp.where(lane < S_A, jax.nn.sigmoid(small), x)

    sr = lax.broadcasted_iota(jnp.int32, (3 * LANES, 2 * LANES), 0) % LANES
    sc = lax.broadcasted_iota(jnp.int32, (3 * LANES, 2 * LANES), 1)
    sel = (sr == jnp.where(sc < LANES, h + S_B, h + S_A)).astype(BF16)
    G = C * DN_UNROLL

    def pass_p(g):
        rep = _dot(jnp.concatenate(_split3(gates_s[pl.ds(pl.multiple_of(g * G, G), G), :]), axis=1), sel)
        beta_s[pl.ds(pl.multiple_of(g * G, G), G), :] = rep[:, :LANES]
        bc_s[pl.ds(pl.multiple_of(g * G, G), G), :] = rep[:, LANES:]
        yield
        R = G // DN_P_PIECES
        for piece in range(DN_P_PIECES):
            r0 = g * G + piece * R
            rows = pl.ds(pl.multiple_of(r0, R), R)
            halo_rows = pl.ds(pl.multiple_of(jnp.maximum(r0 - HALO, 0), HALO), HALO)
            for x_ref, cw_ref, out_s, norm in ((q_ref, cwq_ref, qs_s, D ** -0.5), (k_ref, cwk_ref, k_s, 1.0),
                                               (v_ref, cwv_ref, v_s, None)):
                cw = cw_ref[...]
                halo = jnp.where(r0 == 0, 0.0, x_ref[halo_rows, :])
                x = x_ref[rows, :]
                ext = jnp.concatenate([halo, x], axis=0)
                y = x * cw[DN_CONV - 1:DN_CONV, :]
                for s in range(1, DN_CONV):
                    y = y + pltpu.roll(ext, s, 0)[HALO:, :] * cw[DN_CONV - 1 - s:DN_CONV - s, :]
                y = _silu(y)
                out_s[rows, :] = y if norm is None else _l2norm(y) * norm
                yield

    ri = lax.broadcasted_iota(jnp.int32, (C, C), 0)
    ci = lax.broadcasted_iota(jnp.int32, (C, C), 1)
    tril = ri >= ci
    strict = ri > ci
    eye = (ri == ci).astype(F32)
    nw = nw_ref[...]

    def pass_a(g):
        ns = [g * DN_UNROLL + u for u in U]
        sls = [pl.ds(pl.multiple_of(n * C, C), C) for n in ns]
        qc = [qs_s[s, :] for s in sls]
        kc = [k_s[s, :] for s in sls]
        vc = [v_s[s, :] for s in sls]
        bt = [beta_s[s, :] for s in sls]
        bc = [bc_s[s, :] for s in sls]
        kb = [kc[u] * bt[u] for u in U]
        gram = [_dot_nt(jnp.concatenate([kb[u], qc[u]], axis=0).astype(BF16), kc[u].astype(BF16)) for u in U]
        yield
        dm = [bc[u][:, :C] - jnp.transpose(bc[u])[:C, :] for u in U]
        decay = [jnp.where(tril, jnp.exp(jnp.minimum(dm[u], 0.0)), 0.0) for u in U]
        eb = [jnp.exp(bc[u]) for u in U]
        a = [jnp.where(strict, gram[u][:C] * decay[u], 0.0) for u in U]
        att = [(gram[u][C:] * decay[u]).astype(BF16) for u in U]
        p = [eye - a[u] for u in U]
        a_bf = [a[u].astype(BF16) for u in U]
        x = [_dot(a_bf[u], a_bf[u]) for u in U]
        yield
        for _ in range(4):
            px = [_dot(jnp.concatenate([p[u], x[u]], axis=0).astype(BF16), x[u].astype(BF16)) for u in U]
            p = [p[u] + px[u][:C] for u in U]
            x = [px[u][C:] for u in U]
            yield
        px = [_dot(p[u].astype(BF16), x[u].astype(BF16)) for u in U]
        p = [(p[u] + px[u]).astype(BF16) for u in U]
        yield
        rhs = [jnp.concatenate([kb[u] * eb[u], vc[u] * bt[u]], axis=1).astype(BF16) for u in U]
        wu = [_dot(p[u], rhs[u]).astype(BF16) for u in U]
        yield
        kd = [(kc[u] * jnp.exp(bc[u][C - 1:C, :] - bc[u])).astype(BF16) for u in U]
        aw = [_dot(att[u], wu[u]) for u in U]
        yield
        mn = [_dot_tn(kd[u], wu[u]) for u in U]
        for u in U:
            n = ns[u]
            lhs_s[n, 0:D, :] = mn[u][:, :D].astype(BF16)
            lhs_s[n, D:D + C, :] = (qc[u] * eb[u] - aw[u][:, :D]).astype(BF16)
            add_s[n, 0:D, :] = mn[u][:, D:]
            add_s[n, D:D + C, :] = aw[u][:, D:]
            ebl_s[n] = jnp.broadcast_to(eb[u][C - 1:C, :], (8, LANES))
        yield

    def finish(n, o):
        sl = pl.ds(pl.multiple_of(n * C, C), C)
        y = o * lax.rsqrt(jnp.mean(o * o, axis=-1, keepdims=True) + EPS) * nw * _silu(z_ref[sl, :])
        o_ref[sl, :] = y.astype(o_ref.dtype)

    def step_b(n, carry):
        S, o_prev = carry
        finish(jnp.maximum(n - 1, 0), o_prev)
        ps = _dot(lhs_s[n], S.astype(BF16))
        add = add_s[n]
        o = ps[D:] + add[D:]
        S = ebl_s[n][0:1, :] * S + (add[:D] - ps[:D])
        return S, o

    def interleave(g_a, g_b, g_p, carry):
        stages = pass_a(g_a) if g_a is not None else iter(())
        steps = [g_b * DN_UNROLL + u for u in U] if g_b is not None else []
        pieces = pass_p(g_p) if g_p is not None else iter(())
        next(stages, None)
        for _ in stages:
            if steps:
                carry = step_b(steps.pop(0), carry)
            for _ in range(DN_P_PER_STAGE):
                next(pieces, None)
        for n in steps:
            carry = step_b(n, carry)
        for _ in pieces:
            pass
        return carry

    assert n_groups >= 2
    carry = (jnp.zeros((D, D), F32), jnp.zeros((C, D), F32))
    for _ in pass_p(0):
        pass
    carry = interleave(0, None, 1, carry)
    carry = lax.fori_loop(1, n_groups - 1, lambda g, c: interleave(g, g - 1, g + 1, c), carry)
    carry = interleave(n_groups - 1, n_groups - 2, None, carry)
    _, o_last = interleave(None, n_groups - 1, None, carry)
    finish(seq // C - 1, o_last)


def _deltanet(proj, conv_w, a_log, dt_bias, norm_w, batch, seq):
    M = proj.shape[0]
    H = DN_HEADS
    n_chunks = seq // CHUNK
    assert seq % (CHUNK * DN_UNROLL) == 0
    blk = lambda c0: pl.BlockSpec((seq, DN_D), lambda b, h: (b, c0 // DN_D + h))
    cw = lambda c0: pl.BlockSpec((DN_CONV, DN_D), lambda b, h: (0, c0 // DN_D + h))
    row = pl.BlockSpec((1, LANES), lambda b, h: (0, 0))
    alog = jnp.zeros((1, LANES), F32).at[0, S_A:S_A + H].set(a_log)
    dtb = jnp.zeros((1, LANES), F32).at[0, S_A:S_A + H].set(dt_bias)
    seq_f32 = pltpu.VMEM((seq, DN_D), F32)
    return pl.pallas_call(
        functools.partial(_dn_kernel, seq=seq),
        grid=(batch, H),
        in_specs=[blk(C_DQ), blk(C_DK), blk(C_DV), blk(C_DZ),
                  pl.BlockSpec((seq, LANES), lambda b, h: (b, C_DGATE // LANES)),
                  cw(0), cw(2048), cw(4096), row, row, row],
        out_specs=pl.BlockSpec((seq, DN_D), lambda b, h: (b, h)),
        out_shape=jax.ShapeDtypeStruct((M, H * DN_D), BF16),
        scratch_shapes=[seq_f32] * 6 + [pltpu.VMEM((n_chunks, DN_D + CHUNK, DN_D), BF16),
                                        pltpu.VMEM((n_chunks, DN_D + CHUNK, DN_D), F32),
                                        pltpu.VMEM((n_chunks, 8, LANES), F32)],
        compiler_params=_cparams(("parallel", "arbitrary")),
        name="deltanet",
    )(proj, proj, proj, proj, proj, conv_w, conv_w, conv_w, alog, dtb, norm_w.reshape(1, DN_D))


GLA_LEVELS = (32, 16, 8, 4, 2, 1)
GLA_UNROLL = 4


def _gla_constants():
    C = CHUNK
    i = np.arange(C)[:, None]
    m = np.arange(C)[None, :]
    blocks = [(m <= i), (m > i)]
    upper_rows, masks = [], []
    for s in GLA_LEVELS:
        p = (i // (2 * s)) * (2 * s)
        m0 = p + s - 1
        up = (i - p) >= s
        blocks.append(np.where(up, (m > m0) & (m <= i), (m > i) & (m <= m0)))
        upper_rows.append(np.broadcast_to(up, (C, LANES)))
        j = m
        pj = (j // (2 * s)) * (2 * s)
        masks.append(up & ((j - pj) < s) & (p == pj))
    masks.append(i == m)
    mcat = np.concatenate(blocks, axis=0).astype(np.float32)
    upper = np.stack(upper_rows).astype(np.float32)
    mask = np.stack(masks).astype(np.float32)
    return mcat, upper, mask


N_QK_TILES = GLA_DK // LANES + 1
N_VR_TILES = GLA_DV // LANES + 1


def _gla_kernel(*refs, seq):
    q_t, refs = refs[:N_QK_TILES], refs[N_QK_TILES:]
    k_t, refs = refs[:N_QK_TILES], refs[N_QK_TILES:]
    v_t, refs = refs[:N_VR_TILES], refs[N_VR_TILES:]
    r_t, refs = refs[:N_VR_TILES], refs[N_VR_TILES:]
    lr_ref, wa_ref, ba_ref, nw_ref, mcat_ref, upper_ref, mask_ref, o_ref, gk_s, st_s, o_s = refs
    C = CHUNK

    def rows(tile_refs, s):
        return jnp.concatenate(_lane_shift([t[s, :] for t in tile_refs], GLA_SHIFT), axis=1)

    x_alpha = _dot(lr_ref[...], wa_ref[...], HI) + ba_ref[...]
    gk_s[...] = -_softplus(-x_alpha) / GLA_TAU
    st_s[...] = jnp.zeros_like(st_s)
    mcat = mcat_ref[...]
    nw = nw_ref[...]
    n_lvl = len(GLA_LEVELS)

    U = range(GLA_UNROLL)
    o_s[...] = jnp.zeros_like(o_s)

    def finish(g):
        for u in U:
            sl = pl.ds(pl.multiple_of((g * GLA_UNROLL + u) * C, C), C)
            o = o_s[u * C:(u + 1) * C, :]
            y = o * lax.rsqrt(jnp.mean(o * o, axis=-1, keepdims=True) + EPS) * nw * _silu(rows(r_t, sl))
            o_ref[sl, :] = y.astype(o_ref.dtype)

    def body(g, carry):
        finish(jnp.maximum(g - 1, 0))
        sls = [pl.ds(pl.multiple_of((g * GLA_UNROLL + u) * C, C), C) for u in U]
        qc = [rows(q_t, s) * (GLA_DK ** -0.5) for s in sls]
        kc = [rows(k_t, s) for s in sls]
        v_bf = [rows(v_t, s).astype(BF16) for s in sls]
        e = [_dot(mcat, jnp.concatenate(_split2(gk_s[s, :]), axis=1)) for s in sls]
        ex = [jnp.exp(e[u][:, :GLA_DK] + e[u][:, GLA_DK:]) for u in U]
        k_bf = [kc[u].astype(BF16) for u in U]
        att = [mask_ref[n_lvl] * _dot_nt(qc[u].astype(BF16), k_bf[u]) for u in U]
        for l in range(n_lvl):
            up = upper_ref[l] > 0.5
            zl = [(jnp.where(up, qc[u], kc[u]) * ex[u][(2 + l) * C:(3 + l) * C, :]).astype(BF16) for u in U]
            gl = [_dot_nt(zl[u], zl[u]) for u in U]
            att = [att[u] + mask_ref[l] * gl[u] for u in U]
        ds = [_dot_tn(v_bf[u], (kc[u] * ex[u][C:2 * C, :]).astype(BF16)) for u in U]
        ov = [_dot(att[u].astype(BF16), v_bf[u]) for u in U]
        st = st_s[...]
        for u in U:
            eb = ex[u][0:C, :]
            o_s[u * C:(u + 1) * C, :] = _dot_nt((qc[u] * eb).astype(BF16), st.astype(BF16)) + ov[u]
            st = st * eb[C - 1:C, :] + ds[u]
        st_s[...] = st
        return carry

    n_groups = seq // (C * GLA_UNROLL)
    lax.fori_loop(0, n_groups, body, 0)
    finish(n_groups - 1)


def _gla(proj, w_alpha2, b_alpha, norm_w, batch, seq):
    M = proj.shape[0]
    H = GLA_HEADS
    assert seq % (CHUNK * GLA_UNROLL) == 0
    mcat, upper, mask = _gla_constants()
    wa = jnp.zeros((LANES, H * GLA_DK), F32).at[S_LR:S_LR + GLA_RANK, :].set(w_alpha2)
    full = lambda a: pl.BlockSpec(a.shape, lambda b, h: (0,) * a.ndim)

    def tiles(c0, width, n):
        return [pl.BlockSpec((seq, LANES), lambda b, h, t=t: (b, c0 // LANES + h * (width // LANES) + t))
                for t in range(n)]

    n_in = 2 * N_QK_TILES + 2 * N_VR_TILES
    return pl.pallas_call(
        functools.partial(_gla_kernel, seq=seq),
        grid=(batch, H),
        in_specs=[*tiles(C_GQ, GLA_DK, N_QK_TILES), *tiles(C_GK, GLA_DK, N_QK_TILES),
                  *tiles(C_GV, GLA_DV, N_VR_TILES), *tiles(C_GR, GLA_DV, N_VR_TILES),
                  pl.BlockSpec((seq, LANES), lambda b, h: (b, C_LR // LANES)),
                  pl.BlockSpec((LANES, GLA_DK), lambda b, h: (0, h)),
                  pl.BlockSpec((1, GLA_DK), lambda b, h: (0, h)),
                  pl.BlockSpec((1, GLA_DV), lambda b, h: (0, 0)),
                  full(mcat), full(upper), full(mask)],
        out_specs=pl.BlockSpec((seq, GLA_DV), lambda b, h: (b, h)),
        out_shape=jax.ShapeDtypeStruct((M, H * GLA_DV), BF16),
        scratch_shapes=[pltpu.VMEM((seq, GLA_DK), F32), pltpu.VMEM((GLA_DV, GLA_DK), F32),
                        pltpu.VMEM((GLA_UNROLL * CHUNK, GLA_DV), F32)],
        compiler_params=_cparams(("parallel", "parallel")),
        name="gla",
    )(*([proj] * (n_in + 1)), wa, b_alpha.reshape(1, -1), norm_w.reshape(1, GLA_DV),
      jnp.asarray(mcat, BF16), jnp.asarray(upper), jnp.asarray(mask))


MERGE_ROW_BLOCKS = 2


def _merge_kernel(odn_ref, ogla_ref, pa_ref, pb_ref, gd_ref, gdn_ref, gg_ref, ggn_ref, o_ref):
    def gate(main_ref, next_ref, rows):
        n = main_ref.shape[1] // LANES
        tiles = [main_ref[rows, t * LANES:(t + 1) * LANES] for t in range(n)] + [next_ref[rows, :]]
        return jax.nn.sigmoid(jnp.concatenate(_lane_shift(tiles, GATE_SHIFT), axis=1))

    pa = pa_ref[...].astype(BF16)
    pb = None
    rb = o_ref.shape[0] // MERGE_ROW_BLOCKS
    for r in range(MERGE_ROW_BLOCKS):
        rows = slice(r * rb, (r + 1) * rb)
        ya = _dot(odn_ref[rows, :], pa)
        if pb is None:
            pb = pb_ref[...].astype(BF16)
        yb = _dot(ogla_ref[rows, :], pb)
        o_ref[rows, :] = (gate(gd_ref, gdn_ref, rows) * ya + gate(gg_ref, ggn_ref, rows) * yb).astype(o_ref.dtype)


def _merge(o_dn, o_gla, pa, pb, proj, tm=2048, tn=256):
    M, K = o_dn.shape
    N = pa.shape[1]
    tm = min(tm, M)
    lt = tn // LANES
    main = lambda c0: pl.BlockSpec((tm, tn), lambda i, j: (i, c0 // tn + j))
    nxt = lambda c0: pl.BlockSpec((tm, LANES), lambda i, j: (i, c0 // LANES + (j + 1) * lt))
    assert (C_GATE_DN - GATE_SHIFT) % tn == 0 and (C_GATE_GLA - GATE_SHIFT) % tn == 0
    assert tm % (MERGE_ROW_BLOCKS * 16) == 0
    return pl.pallas_call(
        _merge_kernel,
        grid=(M // tm, N // tn),
        in_specs=[pl.BlockSpec((tm, K), lambda i, j: (i, 0), pipeline_mode=pl.Buffered(1)),
                  pl.BlockSpec((tm, K), lambda i, j: (i, 0), pipeline_mode=pl.Buffered(1)),
                  pl.BlockSpec((K, tn), lambda i, j: (0, j)),
                  pl.BlockSpec((K, tn), lambda i, j: (0, j)),
                  main(C_GATE_DN), nxt(C_GATE_DN), main(C_GATE_GLA), nxt(C_GATE_GLA)],
        out_specs=pl.BlockSpec((tm, tn), lambda i, j: (i, j)),
        out_shape=jax.ShapeDtypeStruct((M, N), BF16),
        compiler_params=_cparams(("parallel", "arbitrary")),
        name="merge",
    )(o_dn, o_gla, pa, pb, proj, proj, proj, proj)


def _mm_res_kernel(a_ref, b_ref, r_ref, o_ref):
    o_ref[...] = r_ref[...] + _dot(a_ref[...], b_ref[...].astype(BF16))


def _matmul_residual(a, b, res, tm=1024, tn=512, name="outproj"):
    M, K = a.shape
    N = b.shape[1]
    tm = min(tm, M)
    return pl.pallas_call(
        _mm_res_kernel,
        grid=(M // tm, N // tn),
        in_specs=[pl.BlockSpec((tm, K), lambda i, j: (i, 0)),
                  pl.BlockSpec((K, tn), lambda i, j: (0, j)),
                  pl.BlockSpec((tm, tn), lambda i, j: (i, j))],
        out_specs=pl.BlockSpec((tm, tn), lambda i, j: (i, j)),
        out_shape=jax.ShapeDtypeStruct((M, N), F32),
        compiler_params=_cparams(("parallel", "arbitrary")),
        name=name,
    )(a, b, res)


FFN_ROW_BLOCKS = 4


def _ffn_in_kernel(h_ref, wg_ref, wu_ref, cw_ref, cb_ref, o_ref):
    wg = wg_ref[...].astype(BF16)
    wu = None
    cw = cw_ref[...]
    cb = cb_ref[...]
    rb = h_ref.shape[0] // FFN_ROW_BLOCKS
    tail = jnp.zeros((HALO, wg.shape[1]), F32)
    for r in range(FFN_ROW_BLOCKS):
        h = h_ref[r * rb:(r + 1) * rb, :]
        gate = _dot(h, wg)
        if wu is None:
            wu = wu_ref[...].astype(BF16)
        up = _dot(h, wu)
        ext = jnp.concatenate([tail, gate], axis=0)
        y = gate * cw[FFN_CONV - 1:FFN_CONV, :] + cb
        for s in range(1, FFN_CONV):
            y = y + pltpu.roll(ext, s, 0)[HALO:, :] * cw[FFN_CONV - 1 - s:FFN_CONV - s, :]
        o_ref[r * rb:(r + 1) * rb, :] = (_silu(y) * up).astype(o_ref.dtype)
        tail = gate[rb - HALO:, :]


def _ffn_in(h, w_in, conv_w, conv_b, seq, tn=256):
    M, K = h.shape
    N = w_in.shape[1] // 2
    nb = N // tn
    assert seq % (FFN_ROW_BLOCKS * HALO) == 0
    return pl.pallas_call(
        _ffn_in_kernel,
        grid=(M // seq, nb),
        in_specs=[pl.BlockSpec((seq, K), lambda i, j: (i, 0), pipeline_mode=pl.Buffered(1)),
                  pl.BlockSpec((K, tn), lambda i, j: (0, j)),
                  pl.BlockSpec((K, tn), lambda i, j: (0, nb + j)),
                  pl.BlockSpec((FFN_CONV, tn), lambda i, j: (0, j)),
                  pl.BlockSpec((1, tn), lambda i, j: (0, j))],
        out_specs=pl.BlockSpec((seq, tn), lambda i, j: (i, j)),
        out_shape=jax.ShapeDtypeStruct((M, N), BF16),
        compiler_params=_cparams(("parallel", "arbitrary")),
        name="ffn_in",
    )(h, w_in, w_in, conv_w, conv_b.reshape(1, N))


def _ffn_out_kernel(a_ref, b_ref, r_ref, o_ref, acc_ref):
    kk = pl.program_id(2)

    @pl.when(kk == 0)
    def _():
        acc_ref[...] = r_ref[...]

    acc_ref[...] += _dot(a_ref[...], b_ref[...])

    @pl.when(kk == pl.num_programs(2) - 1)
    def _():
        o_ref[...] = acc_ref[...]


def _ffn_out(a, b, res, tm=1024, tn=512, ksplit=2):
    M, K = a.shape
    N = b.shape[1]
    tm = min(tm, M)
    tk = K // ksplit
    return pl.pallas_call(
        _ffn_out_kernel,
        grid=(M // tm, N // tn, ksplit),
        in_specs=[pl.BlockSpec((tm, tk), lambda i, j, k: (i, k)),
                  pl.BlockSpec((tk, tn), lambda i, j, k: (k, j)),
                  pl.BlockSpec((tm, tn), lambda i, j, k: (i, j))],
        out_specs=pl.BlockSpec((tm, tn), lambda i, j, k: (i, j)),
        out_shape=jax.ShapeDtypeStruct((M, N), F32),
        scratch_shapes=[pltpu.VMEM((tm, tn), F32)],
        compiler_params=_cparams(("parallel", "arbitrary", "arbitrary")),
        name="ffn_out",
    )(a, b, res)


def _layer(x2d, batch, seq, norm_mix_w, w_in, dn_conv_w, dn_a_log, dn_dt_bias, dn_norm_w,
           gla_w_alpha2, gla_b_alpha, gla_norm_w, w_branch_dn, w_branch_gla, w_out,
           norm_ffn_w, w_ffn_in, ffn_conv_w, ffn_conv_b, w_ffn_out):
    h = _rmsnorm(x2d, norm_mix_w, BF16)
    proj = _inproj(h, w_in.T)
    o_dn = _deltanet(proj, dn_conv_w, dn_a_log, dn_dt_bias, dn_norm_w, batch, seq)
    o_gla = _gla(proj, gla_w_alpha2, gla_b_alpha, gla_norm_w, batch, seq)
    merged = _merge(o_dn, o_gla, w_branch_dn, w_branch_gla, proj)
    x1 = _matmul_residual(merged, w_out, x2d)

    h2 = _rmsnorm(x1, norm_ffn_w, BF16)
    act = _ffn_in(h2, w_ffn_in, ffn_conv_w, ffn_conv_b, seq)
    return _ffn_out(act, w_ffn_out.astype(BF16), x1)


def kernel(x, norm_mix_w, w_in, dn_conv_w, dn_a_log, dn_dt_bias, dn_norm_w, gla_w_alpha2, gla_b_alpha, gla_norm_w, w_branch_dn, w_branch_gla, w_out, norm_ffn_w, w_ffn_in, ffn_conv_w, ffn_conv_b, w_ffn_out, norm_final_w):
    batch, seq, D = x.shape
    x2d = x.reshape(batch * seq, D)
    for l in range(norm_mix_w.shape[0]):
        x2d = _layer(x2d, batch, seq, norm_mix_w[l], w_in[l], dn_conv_w[l], dn_a_log[l], dn_dt_bias[l],
                     dn_norm_w[l], gla_w_alpha2[l], gla_b_alpha[l], gla_norm_w[l], w_branch_dn[l],
                     w_branch_gla[l], w_out[l], norm_ffn_w[l], w_ffn_in[l], ffn_conv_w[l], ffn_conv_b[l],
                     w_ffn_out[l])
    return _rmsnorm(x2d, norm_final_w, F32).reshape(batch, seq, D)
```

```python
import functools

import numpy as np
import jax
import jax.numpy as jnp
from jax import lax
from jax.experimental import pallas as pl
from jax.experimental.pallas import tpu as pltpu

F32 = jnp.float32
BF16 = jnp.bfloat16
HI = lax.Precision.HIGHEST

EPS = 1e-6
CHUNK = 64
DN_HEADS = 16
DN_D = 128
DN_CONV = 4
GLA_HEADS = 8
GLA_DK = 128
GLA_DV = 256
GLA_RANK = 16
GLA_TAU = 16.0
FFN_CONV = 3
LANES = 128
HALO = 8

C_DQ, C_DK, C_DV, C_DZ = 0, 2048, 4096, 6144
C_DGATE = 8192
C_GQ, C_GK, C_GV, C_GR = 8224, 9248, 10272, 12320
C_LR = 14368
C_GATE_DN, C_GATE_GLA = 14384, 18480
S_B, S_A = 0, 16
S_LR = C_LR % LANES
GLA_SHIFT = C_GQ % LANES
GATE_SHIFT = C_GATE_DN % LANES
assert C_GK % LANES == C_GV % LANES == C_GR % LANES == GLA_SHIFT and C_GATE_GLA % LANES == GATE_SHIFT

VMEM_LIMIT = 56 * 1024 * 1024


def _cparams(sem):
    return pltpu.CompilerParams(dimension_semantics=sem, vmem_limit_bytes=VMEM_LIMIT)


def _silu(x):
    return x * jax.nn.sigmoid(x)


def _softplus(x):
    return jnp.maximum(x, 0.0) + jnp.log1p(jnp.exp(-jnp.abs(x)))


def _dot(a, b, precision=None):
    return jnp.dot(a, b, precision=precision, preferred_element_type=F32)


def _dot_nt(a, b, precision=None):
    return lax.dot_general(a, b, (((1,), (1,)), ((), ())), precision=precision, preferred_element_type=F32)


def _dot_tn(a, b, precision=None):
    return lax.dot_general(a, b, (((0,), (0,)), ((), ())), precision=precision, preferred_element_type=F32)


def _rmsnorm_kernel(x_ref, w_ref, o_ref):
    x = x_ref[...]
    ms = jnp.mean(x * x, axis=-1, keepdims=True)
    o_ref[...] = (x * lax.rsqrt(ms + EPS) * w_ref[...]).astype(o_ref.dtype)


def _rmsnorm(x, w, out_dtype, tm=512):
    M, D = x.shape
    tm = min(tm, M)
    return pl.pallas_call(
        _rmsnorm_kernel,
        grid=(M // tm,),
        in_specs=[pl.BlockSpec((tm, D), lambda i: (i, 0)), pl.BlockSpec((1, D), lambda i: (0, 0))],
        out_specs=pl.BlockSpec((tm, D), lambda i: (i, 0)),
        out_shape=jax.ShapeDtypeStruct((M, D), out_dtype),
        compiler_params=_cparams(("parallel",)),
        name="rmsnorm",
    )(x, w.reshape(1, D))


def _inproj_kernel(a_ref, wt_ref, o_ref):
    hn = wt_ref.shape[0] // 2
    lt = hn // LANES
    for half in range(2):
        r = _dot_nt(a_ref[...], wt_ref[half * hn:(half + 1) * hn, :].astype(BF16))
        for t in range(lt):
            o_ref[half * lt + t] = r[:, t * LANES:(t + 1) * LANES]


def _inproj(a, wt, tm=2048, tn=512):
    M, K = a.shape
    N = wt.shape[0]
    tm = min(tm, M)
    return pl.pallas_call(
        _inproj_kernel,
        grid=(M // tm, pl.cdiv(N, tn)),
        in_specs=[pl.BlockSpec((tm, K), lambda i, j: (i, 0), pipeline_mode=pl.Buffered(1)),
                  pl.BlockSpec((tn, K), lambda i, j: (j, 0))],
        out_specs=pl.BlockSpec((tn // LANES, tm, LANES), lambda i, j: (j, i, 0)),
        out_shape=jax.ShapeDtypeStruct((pl.cdiv(N, LANES), M, LANES), F32),
        compiler_params=_cparams(("parallel", "arbitrary")),
        name="inproj",
    )(a, wt)


def _lane_shift(tiles, shift):
    lane = lax.broadcasted_iota(jnp.int32, tiles[0].shape, 1)
    rolled = [pltpu.roll(t, LANES - shift, 1) for t in tiles]
    return [jnp.where(lane < LANES - shift, rolled[i], rolled[i + 1]) for i in range(len(tiles) - 1)]


def _l2norm(x):
    return x * lax.rsqrt(jnp.sum(x * x, axis=-1, keepdims=True) + EPS)


def _split3(x):
    hi = x.astype(BF16)
    r = x - hi.astype(F32)
    mid = r.astype(BF16)
    lo = (r - mid.astype(F32)).astype(BF16)
    return hi, mid, lo


def _split2(x):
    hi = x.astype(BF16)
    return hi, (x - hi.astype(F32)).astype(BF16)


DN_UNROLL = 8
DN_P_PIECES = 4
DN_P_PER_STAGE = 2


def _dn_kernel(q_ref, k_ref, v_ref, z_ref, small_ref, cwq_ref, cwk_ref, cwv_ref, alog_ref, dtb_ref, nw_ref,
               o_ref, gates_s, qs_s, k_s, v_s, beta_s, bc_s, lhs_s, add_s, ebl_s, *, seq):
    h = pl.program_id(1)
    C = CHUNK
    D = DN_D
    U = range(DN_UNROLL)
    n_groups = seq // (C * DN_UNROLL)
    row = lax.broadcasted_iota(jnp.int32, (seq, D), 0)

    @pl.when(h == 0)
    def _():
        small = small_ref[...]
        lane = lax.broadcasted_iota(jnp.int32, small.shape, 1)
        x = -jnp.exp(alog_ref[...]) * _softplus(small + dtb_ref[...])
        in_chunk = row % C
        s = 1
        while s < C:
            x = x + jnp.where(in_chunk >= s, pltpu.roll(x, s, 0), 0.0)
            s *= 2
        gates_s[...] = jnp.where(lane < S_A, jax.nn.sigmoid(small), x)

    sr = lax.broadcasted_iota(jnp.int32, (3 * LANES, 2 * LANES), 0) % LANES
    sc = lax.broadcasted_iota(jnp.int32, (3 * LANES, 2 * LANES), 1)
    sel = (sr == jnp.where(sc < LANES, h + S_B, h + S_A)).astype(BF16)
    G = C * DN_UNROLL

    def pass_p(g):
        rep = _dot(jnp.concatenate(_split3(gates_s[pl.ds(pl.multiple_of(g * G, G), G), :]), axis=1), sel)
        beta_s[pl.ds(pl.multiple_of(g * G, G), G), :] = rep[:, :LANES]
        bc_s[pl.ds(pl.multiple_of(g * G, G), G), :] = rep[:, LANES:]
        yield
        R = G // DN_P_PIECES
        for piece in range(DN_P_PIECES):
            r0 = g * G + piece * R
            rows = pl.ds(pl.multiple_of(r0, R), R)
            halo_rows = pl.ds(pl.multiple_of(jnp.maximum(r0 - HALO, 0), HALO), HALO)
            for x_ref, cw_ref, out_s, norm in ((q_ref, cwq_ref, qs_s, D ** -0.5), (k_ref, cwk_ref, k_s, 1.0),
                                               (v_ref, cwv_ref, v_s, None)):
                cw = cw_ref[...]
                halo = jnp.where(r0 == 0, 0.0, x_ref[halo_rows, :])
                x = x_ref[rows, :]
                ext = jnp.concatenate([halo, x], axis=0)
                y = x * cw[DN_CONV - 1:DN_CONV, :]
                for s in range(1, DN_CONV):
                    y = y + pltpu.roll(ext, s, 0)[HALO:, :] * cw[DN_CONV - 1 - s:DN_CONV - s, :]
                y = _silu(y)
                out_s[rows, :] = y if norm is None else _l2norm(y) * norm
                yield

    ri = lax.broadcasted_iota(jnp.int32, (C, C), 0)
    ci = lax.broadcasted_iota(jnp.int32, (C, C), 1)
    tril = ri >= ci
    strict = ri > ci
    eye = (ri == ci).astype(F32)
    nw = nw_ref[...]

    def pass_a(g):
        ns = [g * DN_UNROLL + u for u in U]
        sls = [pl.ds(pl.multiple_of(n * C, C), C) for n in ns]
        qc = [qs_s[s, :] for s in sls]
        kc = [k_s[s, :] for s in sls]
        vc = [v_s[s, :] for s in sls]
        bt = [beta_s[s, :] for s in sls]
        bc = [bc_s[s, :] for s in sls]
        kb = [kc[u] * bt[u] for u in U]
        gram = [_dot_nt(jnp.concatenate([kb[u], qc[u]], axis=0).astype(BF16), kc[u].astype(BF16)) for u in U]
        yield
        dm = [bc[u][:, :C] - jnp.transpose(bc[u])[:C, :] for u in U]
        decay = [jnp.where(tril, jnp.exp(jnp.minimum(dm[u], 0.0)), 0.0) for u in U]
        eb = [jnp.exp(bc[u]) for u in U]
        a = [jnp.where(strict, gram[u][:C] * decay[u], 0.0) for u in U]
        att = [(gram[u][C:] * decay[u]).astype(BF16) for u in U]
        p = [eye - a[u] for u in U]
        a_bf = [a[u].astype(BF16) for u in U]
        x = [_dot(a_bf[u], a_bf[u]) for u in U]
        yield
        for _ in range(4):
            px = [_dot(jnp.concatenate([p[u], x[u]], axis=0).astype(BF16), x[u].astype(BF16)) for u in U]
            p = [p[u] + px[u][:C] for u in U]
            x = [px[u][C:] for u in U]
            yield
        px = [_dot(p[u].astype(BF16), x[u].astype(BF16)) for u in U]
        p = [(p[u] + px[u]).astype(BF16) for u in U]
        yield
        rhs = [jnp.concatenate([kb[u] * eb[u], vc[u] * bt[u]], axis=1).astype(BF16) for u in U]
        wu = [_dot(p[u], rhs[u]).astype(BF16) for u in U]
        yield
        kd = [(kc[u] * jnp.exp(bc[u][C - 1:C, :] - bc[u])).astype(BF16) for u in U]
        aw = [_dot(att[u], wu[u]) for u in U]
        yield
        mn = [_dot_tn(kd[u], wu[u]) for u in U]
        for u in U:
            n = ns[u]
            lhs_s[n, 0:D, :] = mn[u][:, :D].astype(BF16)
            lhs_s[n, D:D + C, :] = (qc[u] * eb[u] - aw[u][:, :D]).astype(BF16)
            add_s[n, 0:D, :] = mn[u][:, D:]
            add_s[n, D:D + C, :] = aw[u][:, D:]
            ebl_s[n] = jnp.broadcast_to(eb[u][C - 1:C, :], (8, LANES))
        yield

    def finish(n, o):
        sl = pl.ds(pl.multiple_of(n * C, C), C)
        y = o * lax.rsqrt(jnp.mean(o * o, axis=-1, keepdims=True) + EPS) * nw * _silu(z_ref[sl, :])
        o_ref[sl, :] = y.astype(o_ref.dtype)

    def step_b(n, carry):
        S, o_prev = carry
        finish(jnp.maximum(n - 1, 0), o_prev)
        ps = _dot(lhs_s[n], S.astype(BF16))
        add = add_s[n]
        o = ps[D:] + add[D:]
        S = ebl_s[n][0:1, :] * S + (add[:D] - ps[:D])
        return S, o

    def interleave(g_a, g_b, g_p, carry):
        stages = pass_a(g_a) if g_a is not None else iter(())
        steps = [g_b * DN_UNROLL + u for u in U] if g_b is not None else []
        pieces = pass_p(g_p) if g_p is not None else iter(())
        next(stages, None)
        for _ in stages:
            if steps:
                carry = step_b(steps.pop(0), carry)
            for _ in range(DN_P_PER_STAGE):
                next(pieces, None)
        for n in steps:
            carry = step_b(n, carry)
        for _ in pieces:
            pass
        return carry

    assert n_groups >= 2
    carry = (jnp.zeros((D, D), F32), jnp.zeros((C, D), F32))
    for _ in pass_p(0):
        pass
    carry = interleave(0, None, 1, carry)
    carry = lax.fori_loop(1, n_groups - 1, lambda g, c: interleave(g, g - 1, g + 1, c), carry)
    carry = interleave(n_groups - 1, n_groups - 2, None, carry)
    _, o_last = interleave(None, n_groups - 1, None, carry)
    finish(seq // C - 1, o_last)


def _deltanet(proj, conv_w, a_log, dt_bias, norm_w, batch, seq):
    M = proj.shape[1]
    H = DN_HEADS
    n_chunks = seq // CHUNK
    assert seq % (CHUNK * DN_UNROLL) == 0
    blk = lambda c0: pl.BlockSpec((None, seq, DN_D), lambda b, h: (c0 // DN_D + h, b, 0))
    cw = lambda c0: pl.BlockSpec((DN_CONV, DN_D), lambda b, h: (0, c0 // DN_D + h))
    row = pl.BlockSpec((1, LANES), lambda b, h: (0, 0))
    alog = jnp.zeros((1, LANES), F32).at[0, S_A:S_A + H].set(a_log)
    dtb = jnp.zeros((1, LANES), F32).at[0, S_A:S_A + H].set(dt_bias)
    seq_f32 = pltpu.VMEM((seq, DN_D), F32)
    return pl.pallas_call(
        functools.partial(_dn_kernel, seq=seq),
        grid=(batch, H),
        in_specs=[blk(C_DQ), blk(C_DK), blk(C_DV), blk(C_DZ),
                  pl.BlockSpec((None, seq, LANES), lambda b, h: (C_DGATE // LANES, b, 0)),
                  cw(0), cw(2048), cw(4096), row, row, row],
        out_specs=pl.BlockSpec((seq, DN_D), lambda b, h: (b, h)),
        out_shape=jax.ShapeDtypeStruct((M, H * DN_D), BF16),
        scratch_shapes=[seq_f32] * 6 + [pltpu.VMEM((n_chunks, DN_D + CHUNK, DN_D), BF16),
                                        pltpu.VMEM((n_chunks, DN_D + CHUNK, DN_D), F32),
                                        pltpu.VMEM((n_chunks, 8, LANES), F32)],
        compiler_params=_cparams(("parallel", "arbitrary")),
        name="deltanet",
    )(proj, proj, proj, proj, proj, conv_w, conv_w, conv_w, alog, dtb, norm_w.reshape(1, DN_D))


GLA_LEVELS = (32, 16, 8, 4, 2, 1)
GLA_UNROLL = 4


def _gla_constants():
    C = CHUNK
    i = np.arange(C)[:, None]
    m = np.arange(C)[None, :]
    blocks = [(m <= i), (m > i)]
    upper_rows, masks = [], []
    for s in GLA_LEVELS:
        p = (i // (2 * s)) * (2 * s)
        m0 = p + s - 1
        up = (i - p) >= s
        blocks.append(np.where(up, (m > m0) & (m <= i), (m > i) & (m <= m0)))
        upper_rows.append(np.broadcast_to(up, (C, LANES)))
        j = m
        pj = (j // (2 * s)) * (2 * s)
        masks.append(up & ((j - pj) < s) & (p == pj))
    masks.append(i == m)
    mcat = np.concatenate(blocks, axis=0).astype(np.float32)
    upper = np.stack(upper_rows).astype(np.float32)
    mask = np.stack(masks).astype(np.float32)
    return mcat, upper, mask


N_QK_TILES = GLA_DK // LANES + 1
N_VR_TILES = GLA_DV // LANES + 1


def _gla_kernel(*refs, seq):
    q_t, refs = refs[:N_QK_TILES], refs[N_QK_TILES:]
    k_t, refs = refs[:N_QK_TILES], refs[N_QK_TILES:]
    v_t, refs = refs[:N_VR_TILES], refs[N_VR_TILES:]
    r_t, refs = refs[:N_VR_TILES], refs[N_VR_TILES:]
    lr_ref, wa_ref, ba_ref, nw_ref, mcat_ref, upper_ref, mask_ref, o_ref, gk_s, st_s, o_s = refs
    C = CHUNK

    def rows(tile_refs, s):
        return jnp.concatenate(_lane_shift([t[s, :] for t in tile_refs], GLA_SHIFT), axis=1)

    x_alpha = _dot(lr_ref[...], wa_ref[...], HI) + ba_ref[...]
    gk_s[...] = -_softplus(-x_alpha) / GLA_TAU
    st_s[...] = jnp.zeros_like(st_s)
    mcat = mcat_ref[...]
    nw = nw_ref[...]
    n_lvl = len(GLA_LEVELS)

    U = range(GLA_UNROLL)
    o_s[...] = jnp.zeros_like(o_s)

    def finish(g):
        for u in U:
            sl = pl.ds(pl.multiple_of((g * GLA_UNROLL + u) * C, C), C)
            o = o_s[u * C:(u + 1) * C, :]
            y = o * lax.rsqrt(jnp.mean(o * o, axis=-1, keepdims=True) + EPS) * nw * _silu(rows(r_t, sl))
            o_ref[sl, :] = y.astype(o_ref.dtype)

    def body(g, carry):
        finish(jnp.maximum(g - 1, 0))
        sls = [pl.ds(pl.multiple_of((g * GLA_UNROLL + u) * C, C), C) for u in U]
        qc = [rows(q_t, s) * (GLA_DK ** -0.5) for s in sls]
        kc = [rows(k_t, s) for s in sls]
        v_bf = [rows(v_t, s).astype(BF16) for s in sls]
        e = [_dot(mcat, jnp.concatenate(_split2(gk_s[s, :]), axis=1)) for s in sls]
        ex = [jnp.exp(e[u][:, :GLA_DK] + e[u][:, GLA_DK:]) for u in U]
        k_bf = [kc[u].astype(BF16) for u in U]
        att = [mask_ref[n_lvl] * _dot_nt(qc[u].astype(BF16), k_bf[u]) for u in U]
        for l in range(n_lvl):
            up = upper_ref[l] > 0.5
            zl = [(jnp.where(up, qc[u], kc[u]) * ex[u][(2 + l) * C:(3 + l) * C, :]).astype(BF16) for u in U]
            gl = [_dot_nt(zl[u], zl[u]) for u in U]
            att = [att[u] + mask_ref[l] * gl[u] for u in U]
        ds = [_dot_tn(v_bf[u], (kc[u] * ex[u][C:2 * C, :]).astype(BF16)) for u in U]
        ov = [_dot(att[u].astype(BF16), v_bf[u]) for u in U]
        st = st_s[...]
        for u in U:
            eb = ex[u][0:C, :]
            o_s[u * C:(u + 1) * C, :] = _dot_nt((qc[u] * eb).astype(BF16), st.astype(BF16)) + ov[u]
            st = st * eb[C - 1:C, :] + ds[u]
        st_s[...] = st
        return carry

    n_groups = seq // (C * GLA_UNROLL)
    lax.fori_loop(0, n_groups, body, 0)
    finish(n_groups - 1)


def _gla(proj, w_alpha2, b_alpha, norm_w, batch, seq):
    M = proj.shape[1]
    H = GLA_HEADS
    assert seq % (CHUNK * GLA_UNROLL) == 0
    mcat, upper, mask = _gla_constants()
    wa = jnp.zeros((LANES, H * GLA_DK), F32).at[S_LR:S_LR + GLA_RANK, :].set(w_alpha2)
    full = lambda a: pl.BlockSpec(a.shape, lambda b, h: (0,) * a.ndim)

    def tiles(c0, width, n):
        return [pl.BlockSpec((None, seq, LANES), lambda b, h, t=t: (c0 // LANES + h * (width // LANES) + t, b, 0))
                for t in range(n)]

    n_in = 2 * N_QK_TILES + 2 * N_VR_TILES
    return pl.pallas_call(
        functools.partial(_gla_kernel, seq=seq),
        grid=(batch, H),
        in_specs=[*tiles(C_GQ, GLA_DK, N_QK_TILES), *tiles(C_GK, GLA_DK, N_QK_TILES),
                  *tiles(C_GV, GLA_DV, N_VR_TILES), *tiles(C_GR, GLA_DV, N_VR_TILES),
                  pl.BlockSpec((None, seq, LANES), lambda b, h: (C_LR // LANES, b, 0)),
                  pl.BlockSpec((LANES, GLA_DK), lambda b, h: (0, h)),
                  pl.BlockSpec((1, GLA_DK), lambda b, h: (0, h)),
                  pl.BlockSpec((1, GLA_DV), lambda b, h: (0, 0)),
                  full(mcat), full(upper), full(mask)],
        out_specs=pl.BlockSpec((seq, GLA_DV), lambda b, h: (b, h)),
        out_shape=jax.ShapeDtypeStruct((M, H * GLA_DV), BF16),
        scratch_shapes=[pltpu.VMEM((seq, GLA_DK), F32), pltpu.VMEM((GLA_DV, GLA_DK), F32),
                        pltpu.VMEM((GLA_UNROLL * CHUNK, GLA_DV), F32)],
        compiler_params=_cparams(("parallel", "parallel")),
        name="gla",
    )(*([proj] * (n_in + 1)), wa, b_alpha.reshape(1, -1), norm_w.reshape(1, GLA_DV),
      jnp.asarray(mcat, BF16), jnp.asarray(upper), jnp.asarray(mask))


MERGE_ROW_BLOCKS = 2


def _merge_kernel(odn_ref, ogla_ref, pa_ref, pb_ref, gd_ref, gdn_ref, gg_ref, ggn_ref, o_ref):
    def gate(main_ref, next_ref, rows):
        tiles = [main_ref[t, rows, :] for t in range(main_ref.shape[0])] + [next_ref[rows, :]]
        return jax.nn.sigmoid(jnp.concatenate(_lane_shift(tiles, GATE_SHIFT), axis=1))

    pa = pa_ref[...].astype(BF16)
    pb = None
    rb = o_ref.shape[0] // MERGE_ROW_BLOCKS
    for r in range(MERGE_ROW_BLOCKS):
        rows = slice(r * rb, (r + 1) * rb)
        ya = _dot(odn_ref[rows, :], pa)
        if pb is None:
            pb = pb_ref[...].astype(BF16)
        yb = _dot(ogla_ref[rows, :], pb)
        o_ref[rows, :] = (gate(gd_ref, gdn_ref, rows) * ya + gate(gg_ref, ggn_ref, rows) * yb).astype(o_ref.dtype)


def _merge(o_dn, o_gla, pa, pb, proj, tm=2048, tn=256):
    M, K = o_dn.shape
    N = pa.shape[1]
    tm = min(tm, M)
    lt = tn // LANES
    main = lambda c0: pl.BlockSpec((lt, tm, LANES), lambda i, j: (c0 // tn + j, i, 0))
    nxt = lambda c0: pl.BlockSpec((None, tm, LANES), lambda i, j: (c0 // LANES + (j + 1) * lt, i, 0))
    assert (C_GATE_DN - GATE_SHIFT) % tn == 0 and (C_GATE_GLA - GATE_SHIFT) % tn == 0
    assert tm % (MERGE_ROW_BLOCKS * 16) == 0
    return pl.pallas_call(
        _merge_kernel,
        grid=(M // tm, N // tn),
        in_specs=[pl.BlockSpec((tm, K), lambda i, j: (i, 0), pipeline_mode=pl.Buffered(1)),
                  pl.BlockSpec((tm, K), lambda i, j: (i, 0), pipeline_mode=pl.Buffered(1)),
                  pl.BlockSpec((K, tn), lambda i, j: (0, j)),
                  pl.BlockSpec((K, tn), lambda i, j: (0, j)),
                  main(C_GATE_DN), nxt(C_GATE_DN), main(C_GATE_GLA), nxt(C_GATE_GLA)],
        out_specs=pl.BlockSpec((tm, tn), lambda i, j: (i, j)),
        out_shape=jax.ShapeDtypeStruct((M, N), BF16),
        compiler_params=_cparams(("parallel", "arbitrary")),
        name="merge",
    )(o_dn, o_gla, pa, pb, proj, proj, proj, proj)


def _mm_res_kernel(a_ref, b_ref, r_ref, o_ref):
    o_ref[...] = r_ref[...] + _dot(a_ref[...], b_ref[...].astype(BF16))


def _matmul_residual(a, b, res, tm=1024, tn=512, name="outproj"):
    M, K = a.shape
    N = b.shape[1]
    tm = min(tm, M)
    return pl.pallas_call(
        _mm_res_kernel,
        grid=(M // tm, N // tn),
        in_specs=[pl.BlockSpec((tm, K), lambda i, j: (i, 0)),
                  pl.BlockSpec((K, tn), lambda i, j: (0, j)),
                  pl.BlockSpec((tm, tn), lambda i, j: (i, j))],
        out_specs=pl.BlockSpec((tm, tn), lambda i, j: (i, j)),
        out_shape=jax.ShapeDtypeStruct((M, N), F32),
        compiler_params=_cparams(("parallel", "arbitrary")),
        name=name,
    )(a, b, res)


FFN_ROW_BLOCKS = 4


def _ffn_in_kernel(h_ref, wg_ref, wu_ref, cw_ref, cb_ref, o_ref):
    wg = wg_ref[...].astype(BF16)
    wu = None
    cw = cw_ref[...]
    cb = cb_ref[...]
    rb = h_ref.shape[0] // FFN_ROW_BLOCKS
    tail = jnp.zeros((HALO, wg.shape[1]), F32)
    for r in range(FFN_ROW_BLOCKS):
        h = h_ref[r * rb:(r + 1) * rb, :]
        gate = _dot(h, wg)
        if wu is None:
            wu = wu_ref[...].astype(BF16)
        up = _dot(h, wu)
        ext = jnp.concatenate([tail, gate], axis=0)
        y = gate * cw[FFN_CONV - 1:FFN_CONV, :] + cb
        for s in range(1, FFN_CONV):
            y = y + pltpu.roll(ext, s, 0)[HALO:, :] * cw[FFN_CONV - 1 - s:FFN_CONV - s, :]
        o_ref[r * rb:(r + 1) * rb, :] = (_silu(y) * up).astype(o_ref.dtype)
        tail = gate[rb - HALO:, :]


def _ffn_in(h, w_in, conv_w, conv_b, seq, tn=256):
    M, K = h.shape
    N = w_in.shape[1] // 2
    nb = N // tn
    assert seq % (FFN_ROW_BLOCKS * HALO) == 0
    return pl.pallas_call(
        _ffn_in_kernel,
        grid=(M // seq, nb),
        in_specs=[pl.BlockSpec((seq, K), lambda i, j: (i, 0), pipeline_mode=pl.Buffered(1)),
                  pl.BlockSpec((K, tn), lambda i, j: (0, j)),
                  pl.BlockSpec((K, tn), lambda i, j: (0, nb + j)),
                  pl.BlockSpec((FFN_CONV, tn), lambda i, j: (0, j)),
                  pl.BlockSpec((1, tn), lambda i, j: (0, j))],
        out_specs=pl.BlockSpec((seq, tn), lambda i, j: (i, j)),
        out_shape=jax.ShapeDtypeStruct((M, N), BF16),
        compiler_params=_cparams(("parallel", "arbitrary")),
        name="ffn_in",
    )(h, w_in, w_in, conv_w, conv_b.reshape(1, N))


def _ffn_out_kernel(a_ref, b_ref, r_ref, o_ref, acc_ref):
    kk = pl.program_id(2)

    @pl.when(kk == 0)
    def _():
        acc_ref[...] = r_ref[...]

    acc_ref[...] += _dot(a_ref[...], b_ref[...])

    @pl.when(kk == pl.num_programs(2) - 1)
    def _():
        o_ref[...] = acc_ref[...]


def _ffn_out(a, b, res, tm=1024, tn=512, ksplit=2):
    M, K = a.shape
    N = b.shape[1]
    tm = min(tm, M)
    tk = K // ksplit
    return pl.pallas_call(
        _ffn_out_kernel,
        grid=(M // tm, N // tn, ksplit),
        in_specs=[pl.BlockSpec((tm, tk), lambda i, j, k: (i, k)),
                  pl.BlockSpec((tk, tn), lambda i, j, k: (k, j)),
                  pl.BlockSpec((tm, tn), lambda i, j, k: (i, j))],
        out_specs=pl.BlockSpec((tm, tn), lambda i, j, k: (i, j)),
        out_shape=jax.ShapeDtypeStruct((M, N), F32),
        scratch_shapes=[pltpu.VMEM((tm, tn), F32)],
        compiler_params=_cparams(("parallel", "arbitrary", "arbitrary")),
        name="ffn_out",
    )(a, b, res)


def _layer(x2d, batch, seq, norm_mix_w, w_in, dn_conv_w, dn_a_log, dn_dt_bias, dn_norm_w,
           gla_w_alpha2, gla_b_alpha, gla_norm_w, w_branch_dn, w_branch_gla, w_out,
           norm_ffn_w, w_ffn_in, ffn_conv_w, ffn_conv_b, w_ffn_out):
    h = _rmsnorm(x2d, norm_mix_w, BF16)
    proj = _inproj(h, w_in.T)
    o_dn = _deltanet(proj, dn_conv_w, dn_a_log, dn_dt_bias, dn_norm_w, batch, seq)
    o_gla = _gla(proj, gla_w_alpha2, gla_b_alpha, gla_norm_w, batch, seq)
    merged = _merge(o_dn, o_gla, w_branch_dn, w_branch_gla, proj)
    x1 = _matmul_residual(merged, w_out, x2d)

    h2 = _rmsnorm(x1, norm_ffn_w, BF16)
    act = _ffn_in(h2, w_ffn_in, ffn_conv_w, ffn_conv_b, seq)
    return _ffn_out(act, w_ffn_out.astype(BF16), x1)


def kernel(x, norm_mix_w, w_in, dn_conv_w, dn_a_log, dn_dt_bias, dn_norm_w, gla_w_alpha2, gla_b_alpha, gla_norm_w, w_branch_dn, w_branch_gla, w_out, norm_ffn_w, w_ffn_in, ffn_conv_w, ffn_conv_b, w_ffn_out, norm_final_w):
    batch, seq, D = x.shape
    x2d = x.reshape(batch * seq, D)
    for l in range(norm_mix_w.shape[0]):
        x2d = _layer(x2d, batch, seq, norm_mix_w[l], w_in[l], dn_conv_w[l], dn_a_log[l], dn_dt_bias[l],
                     dn_norm_w[l], gla_w_alpha2[l], gla_b_alpha[l], gla_norm_w[l], w_branch_dn[l],
                     w_branch_gla[l], w_out[l], norm_ffn_w[l], w_ffn_in[l], ffn_conv_w[l], ffn_conv_b[l],
                     w_ffn_out[l])
    return _rmsnorm(x2d, norm_final_w, F32).reshape(batch, seq, D)
```

```python
import functools

import numpy as np
import jax
import jax.numpy as jnp
from jax import lax
from jax.experimental import pallas as pl
from jax.experimental.pallas import tpu as pltpu

F32 = jnp.float32
BF16 = jnp.bfloat16
HI = lax.Precision.HIGHEST

EPS = 1e-6
CHUNK = 64
DN_HEADS = 16
DN_D = 128
DN_CONV = 4
GLA_HEADS = 8
GLA_DK = 128
GLA_DV = 256
GLA_RANK = 16
GLA_TAU = 16.0
FFN_CONV = 3
LANES = 128
HALO = 8

C_DQ, C_DK, C_DV, C_DZ = 0, 2048, 4096, 6144
C_DGATE = 8192
C_GQ, C_GK, C_GV, C_GR = 8224, 9248, 10272, 12320
C_LR = 14368
C_GATE_DN, C_GATE_GLA = 14384, 18480
S_B, S_A = 0, 16
S_LR = C_LR % LANES
GLA_SHIFT = C_GQ % LANES
GATE_SHIFT = C_GATE_DN % LANES
assert C_GK % LANES == C_GV % LANES == C_GR % LANES == GLA_SHIFT and C_GATE_GLA % LANES == GATE_SHIFT

VMEM_LIMIT = 56 * 1024 * 1024


def _cparams(sem):
    return pltpu.CompilerParams(dimension_semantics=sem, vmem_limit_bytes=VMEM_LIMIT)


def _silu(x):
    return x * jax.nn.sigmoid(x)


def _softplus(x):
    return jnp.maximum(x, 0.0) + jnp.log1p(jnp.exp(-jnp.abs(x)))


def _dot(a, b, precision=None):
    return jnp.dot(a, b, precision=precision, preferred_element_type=F32)


def _dot_nt(a, b, precision=None):
    return lax.dot_general(a, b, (((1,), (1,)), ((), ())), precision=precision, preferred_element_type=F32)


def _dot_tn(a, b, precision=None):
    return lax.dot_general(a, b, (((0,), (0,)), ((), ())), precision=precision, preferred_element_type=F32)


def _rmsnorm_kernel(x_ref, w_ref, o_ref):
    x = x_ref[...]
    ms = jnp.mean(x * x, axis=-1, keepdims=True)
    o_ref[...] = (x * lax.rsqrt(ms + EPS) * w_ref[...]).astype(o_ref.dtype)


def _rmsnorm(x, w, out_dtype, tm=512):
    M, D = x.shape
    tm = min(tm, M)
    return pl.pallas_call(
        _rmsnorm_kernel,
        grid=(M // tm,),
        in_specs=[pl.BlockSpec((tm, D), lambda i: (i, 0)), pl.BlockSpec((1, D), lambda i: (0, 0))],
        out_specs=pl.BlockSpec((tm, D), lambda i: (i, 0)),
        out_shape=jax.ShapeDtypeStruct((M, D), out_dtype),
        compiler_params=_cparams(("parallel",)),
        name="rmsnorm",
    )(x, w.reshape(1, D))


def _inproj_kernel(a_ref, wt_ref, o_ref):
    hn = wt_ref.shape[0] // 2
    for c in (slice(0, hn), slice(hn, 2 * hn)):
        o_ref[:, c] = _dot_nt(a_ref[...], wt_ref[c, :].astype(BF16))


def _inproj(a, wt, tm=2048, tn=512):
    M, K = a.shape
    N = wt.shape[0]
    tm = min(tm, M)
    return pl.pallas_call(
        _inproj_kernel,
        grid=(M // tm, pl.cdiv(N, tn)),
        in_specs=[pl.BlockSpec((tm, K), lambda i, j: (i, 0), pipeline_mode=pl.Buffered(1)),
                  pl.BlockSpec((tn, K), lambda i, j: (j, 0))],
        out_specs=pl.BlockSpec((tm, tn), lambda i, j: (i, j)),
        out_shape=jax.ShapeDtypeStruct((M, N), F32),
        compiler_params=_cparams(("parallel", "arbitrary")),
        name="inproj",
    )(a, wt)


def _lane_shift(tiles, shift):
    lane = lax.broadcasted_iota(jnp.int32, tiles[0].shape, 1)
    rolled = [pltpu.roll(t, LANES - shift, 1) for t in tiles]
    return [jnp.where(lane < LANES - shift, rolled[i], rolled[i + 1]) for i in range(len(tiles) - 1)]


def _l2norm(x):
    return x * lax.rsqrt(jnp.sum(x * x, axis=-1, keepdims=True) + EPS)


def _split3(x):
    hi = x.astype(BF16)
    r = x - hi.astype(F32)
    mid = r.astype(BF16)
    lo = (r - mid.astype(F32)).astype(BF16)
    return hi, mid, lo


def _split2(x):
    hi = x.astype(BF16)
    return hi, (x - hi.astype(F32)).astype(BF16)


DN_UNROLL = 8
DN_P_PIECES = 4
DN_P_PER_STAGE = 2


def _dn_kernel(q_ref, k_ref, v_ref, z_ref, small_ref, cwq_ref, cwk_ref, cwv_ref, alog_ref, dtb_ref, nw_ref,
               o_ref, gates_s, qs_s, k_s, v_s, beta_s, bc_s, lhs_s, add_s, ebl_s, *, seq):
    h = pl.program_id(1)
    C = CHUNK
    D = DN_D
    U = range(DN_UNROLL)
    n_groups = seq // (C * DN_UNROLL)
    row = lax.broadcasted_iota(jnp.int32, (seq, D), 0)

    @pl.when(h == 0)
    def _():
        small = small_ref[...]
        lane = lax.broadcasted_iota(jnp.int32, small.shape, 1)
        x = -jnp.exp(alog_ref[...]) * _softplus(small + dtb_ref[...])
        in_chunk = row % C
        s = 1
        while s < C:
            x = x + jnp.where(in_chunk >= s, pltpu.roll(x, s, 0), 0.0)
            s *= 2
        gates_s[...] = jnp.where(lane < S_A, jax.nn.sigmoid(small), x)

    sr = lax.broadcasted_iota(jnp.int32, (3 * LANES, 2 * LANES), 0) % LANES
    sc = lax.broadcasted_iota(jnp.int32, (3 * LANES, 2 * LANES), 1)
    sel = (sr == jnp.where(sc < LANES, h + S_B, h + S_A)).astype(BF16)
    G = C * DN_UNROLL

    def pass_p(g):
        rep = _dot(jnp.concatenate(_split3(gates_s[pl.ds(pl.multiple_of(g * G, G), G), :]), axis=1), sel)
        beta_s[pl.ds(pl.multiple_of(g * G, G), G), :] = rep[:, :LANES]
        bc_s[pl.ds(pl.multiple_of(g * G, G), G), :] = rep[:, LANES:]
        yield
        R = G // DN_P_PIECES
        for piece in range(DN_P_PIECES):
            r0 = g * G + piece * R
            rows = pl.ds(pl.multiple_of(r0, R), R)
            halo_rows = pl.ds(pl.multiple_of(jnp.maximum(r0 - HALO, 0), HALO), HALO)
            for x_ref, cw_ref, out_s, norm in ((q_ref, cwq_ref, qs_s, D ** -0.5), (k_ref, cwk_ref, k_s, 1.0),
                                               (v_ref, cwv_ref, v_s, None)):
                cw = cw_ref[...]
                halo = jnp.where(r0 == 0, 0.0, x_ref[halo_rows, :])
                x = x_ref[rows, :]
                ext = jnp.concatenate([halo, x], axis=0)
                y = x * cw[DN_CONV - 1:DN_CONV, :]
                for s in range(1, DN_CONV):
                    y = y + pltpu.roll(ext, s, 0)[HALO:, :] * cw[DN_CONV - 1 - s:DN_CONV - s, :]
                y = _silu(y)
                out_s[rows, :] = y if norm is None else _l2norm(y) * norm
                yield

    ri = lax.broadcasted_iota(jnp.int32, (C, C), 0)
    ci = lax.broadcasted_iota(jnp.int32, (C, C), 1)
    tril = ri >= ci
    strict = ri > ci
    eye = (ri == ci).astype(F32)
    nw = nw_ref[...]

    def pass_a(g):
        ns = [g * DN_UNROLL + u for u in U]
        sls = [pl.ds(pl.multiple_of(n * C, C), C) for n in ns]
        qc = [qs_s[s, :] for s in sls]
        kc = [k_s[s, :] for s in sls]
        vc = [v_s[s, :] for s in sls]
        bt = [beta_s[s, :] for s in sls]
        bc = [bc_s[s, :] for s in sls]
        kb = [kc[u] * bt[u] for u in U]
        gram = [_dot_nt(jnp.concatenate([kb[u], qc[u]], axis=0).astype(BF16), kc[u].astype(BF16)) for u in U]
        yield
        dm = [bc[u][:, :C] - jnp.transpose(bc[u])[:C, :] for u in U]
        decay = [jnp.where(tril, jnp.exp(jnp.minimum(dm[u], 0.0)), 0.0) for u in U]
        eb = [jnp.exp(bc[u]) for u in U]
        a = [jnp.where(strict, gram[u][:C] * decay[u], 0.0) for u in U]
        att = [(gram[u][C:] * decay[u]).astype(BF16) for u in U]
        p = [eye - a[u] for u in U]
        a_bf = [a[u].astype(BF16) for u in U]
        x = [_dot(a_bf[u], a_bf[u]) for u in U]
        yield
        for _ in range(4):
            px = [_dot(jnp.concatenate([p[u], x[u]], axis=0).astype(BF16), x[u].astype(BF16)) for u in U]
            p = [p[u] + px[u][:C] for u in U]
            x = [px[u][C:] for u in U]
            yield
        px = [_dot(p[u].astype(BF16), x[u].astype(BF16)) for u in U]
        p = [(p[u] + px[u]).astype(BF16) for u in U]
        yield
        rhs = [jnp.concatenate([kb[u] * eb[u], vc[u] * bt[u]], axis=1).astype(BF16) for u in U]
        wu = [_dot(p[u], rhs[u]).astype(BF16) for u in U]
        yield
        kd = [(kc[u] * jnp.exp(bc[u][C - 1:C, :] - bc[u])).astype(BF16) for u in U]
        aw = [_dot(att[u], wu[u]) for u in U]
        yield
        mn = [_dot_tn(kd[u], wu[u]) for u in U]
        for u in U:
            n = ns[u]
            lhs_s[n, 0:D, :] = mn[u][:, :D].astype(BF16)
            lhs_s[n, D:D + C, :] = (qc[u] * eb[u] - aw[u][:, :D]).astype(BF16)
            add_s[n, 0:D, :] = mn[u][:, D:]
            add_s[n, D:D + C, :] = aw[u][:, D:]
            ebl_s[n] = jnp.broadcast_to(eb[u][C - 1:C, :], (8, LANES))
        yield

    def finish(n, o):
        sl = pl.ds(pl.multiple_of(n * C, C), C)
        y = o * lax.rsqrt(jnp.mean(o * o, axis=-1, keepdims=True) + EPS) * nw * _silu(z_ref[sl, :])
        o_ref[sl, :] = y.astype(o_ref.dtype)

    def step_b(n, carry):
        S, o_prev = carry
        finish(jnp.maximum(n - 1, 0), o_prev)
        ps = _dot(lhs_s[n], S.astype(BF16))
        add = add_s[n]
        o = ps[D:] + add[D:]
        S = ebl_s[n][0:1, :] * S + (add[:D] - ps[:D])
        return S, o

    def interleave(g_a, g_b, g_p, carry):
        stages = pass_a(g_a) if g_a is not None else iter(())
        steps = [g_b * DN_UNROLL + u for u in U] if g_b is not None else []
        pieces = pass_p(g_p) if g_p is not None else iter(())
        next(stages, None)
        for _ in stages:
            if steps:
                carry = step_b(steps.pop(0), carry)
            for _ in range(DN_P_PER_STAGE):
                next(pieces, None)
        for n in steps:
            carry = step_b(n, carry)
        for _ in pieces:
            pass
        return carry

    assert n_groups >= 2
    carry = (jnp.zeros((D, D), F32), jnp.zeros((C, D), F32))
    for _ in pass_p(0):
        pass
    carry = interleave(0, None, 1, carry)
    carry = lax.fori_loop(1, n_groups - 1, lambda g, c: interleave(g, g - 1, g + 1, c), carry)
    carry = interleave(n_groups - 1, n_groups - 2, None, carry)
    _, o_last = interleave(None, n_groups - 1, None, carry)
    finish(seq // C - 1, o_last)


def _deltanet(proj, conv_w, a_log, dt_bias, norm_w, batch, seq):
    M = proj.shape[0]
    H = DN_HEADS
    n_chunks = seq // CHUNK
    assert seq % (CHUNK * DN_UNROLL) == 0
    blk = lambda c0: pl.BlockSpec((seq, DN_D), lambda b, h: (b, c0 // DN_D + h))
    cw = lambda c0: pl.BlockSpec((DN_CONV, DN_D), lambda b, h: (0, c0 // DN_D + h))
    row = pl.BlockSpec((1, LANES), lambda b, h: (0, 0))
    alog = jnp.zeros((1, LANES), F32).at[0, S_A:S_A + H].set(a_log)
    dtb = jnp.zeros((1, LANES), F32).at[0, S_A:S_A + H].set(dt_bias)
    seq_f32 = pltpu.VMEM((seq, DN_D), F32)
    return pl.pallas_call(
        functools.partial(_dn_kernel, seq=seq),
        grid=(batch, H),
        in_specs=[blk(C_DQ), blk(C_DK), blk(C_DV), blk(C_DZ),
                  pl.BlockSpec((seq, LANES), lambda b, h: (b, C_DGATE // LANES)),
                  cw(0), cw(2048), cw(4096), row, row, row],
        out_specs=pl.BlockSpec((seq, DN_D), lambda b, h: (b, h)),
        out_shape=jax.ShapeDtypeStruct((M, H * DN_D), BF16),
        scratch_shapes=[seq_f32] * 6 + [pltpu.VMEM((n_chunks, DN_D + CHUNK, DN_D), BF16),
                                        pltpu.VMEM((n_chunks, DN_D + CHUNK, DN_D), F32),
                                        pltpu.VMEM((n_chunks, 8, LANES), F32)],
        compiler_params=_cparams(("parallel", "arbitrary")),
        name="deltanet",
    )(proj, proj, proj, proj, proj, conv_w, conv_w, conv_w, alog, dtb, norm_w.reshape(1, DN_D))


GLA_LEVELS = (32, 16, 8, 4, 2, 1)
GLA_UNROLL = 4


def _gla_constants():
    C = CHUNK
    i = np.arange(C)[:, None]
    m = np.arange(C)[None, :]
    blocks = [(m <= i), (m > i)]
    upper_rows, masks = [], []
    for s in GLA_LEVELS:
        p = (i // (2 * s)) * (2 * s)
        m0 = p + s - 1
        up = (i - p) >= s
        blocks.append(np.where(up, (m > m0) & (m <= i), (m > i) & (m <= m0)))
        upper_rows.append(np.broadcast_to(up, (C, LANES)))
        j = m
        pj = (j // (2 * s)) * (2 * s)
        masks.append(up & ((j - pj) < s) & (p == pj))
    masks.append(i == m)
    mcat = np.concatenate(blocks, axis=0).astype(np.float32)
    upper = np.stack(upper_rows).astype(np.float32)
    mask = np.stack(masks).astype(np.float32)
    return mcat, upper, mask


N_QK_TILES = GLA_DK // LANES + 1
N_VR_TILES = GLA_DV // LANES + 1


def _gla_kernel(*refs, seq):
    q_t, refs = refs[:N_QK_TILES], refs[N_QK_TILES:]
    k_t, refs = refs[:N_QK_TILES], refs[N_QK_TILES:]
    v_t, refs = refs[:N_VR_TILES], refs[N_VR_TILES:]
    r_t, refs = refs[:N_VR_TILES], refs[N_VR_TILES:]
    lr_ref, wa_ref, ba_ref, nw_ref, mcat_ref, upper_ref, mask_ref, o_ref, gk_s, st_s, o_s = refs
    C = CHUNK

    def rows(tile_refs, s):
        return jnp.concatenate(_lane_shift([t[s, :] for t in tile_refs], GLA_SHIFT), axis=1)

    x_alpha = _dot(lr_ref[...], wa_ref[...], HI) + ba_ref[...]
    gk_s[...] = -_softplus(-x_alpha) / GLA_TAU
    st_s[...] = jnp.zeros_like(st_s)
    mcat = mcat_ref[...]
    nw = nw_ref[...]
    n_lvl = len(GLA_LEVELS)

    U = range(GLA_UNROLL)
    o_s[...] = jnp.zeros_like(o_s)

    def finish(g):
        for u in U:
            sl = pl.ds(pl.multiple_of((g * GLA_UNROLL + u) * C, C), C)
            o = o_s[u * C:(u + 1) * C, :]
            y = o * lax.rsqrt(jnp.mean(o * o, axis=-1, keepdims=True) + EPS) * nw * _silu(rows(r_t, sl))
            o_ref[sl, :] = y.astype(o_ref.dtype)

    def body(g, carry):
        finish(jnp.maximum(g - 1, 0))
        sls = [pl.ds(pl.multiple_of((g * GLA_UNROLL + u) * C, C), C) for u in U]
        qc = [rows(q_t, s) * (GLA_DK ** -0.5) for s in sls]
        kc = [rows(k_t, s) for s in sls]
        v_bf = [rows(v_t, s).astype(BF16) for s in sls]
        e = [_dot(mcat, jnp.concatenate(_split2(gk_s[s, :]), axis=1)) for s in sls]
        ex = [jnp.exp(e[u][:, :GLA_DK] + e[u][:, GLA_DK:]) for u in U]
        k_bf = [kc[u].astype(BF16) for u in U]
        att = [mask_ref[n_lvl] * _dot_nt(qc[u].astype(BF16), k_bf[u]) for u in U]
        for l in range(n_lvl):
            up = upper_ref[l] > 0.5
            zl = [(jnp.where(up, qc[u], kc[u]) * ex[u][(2 + l) * C:(3 + l) * C, :]).astype(BF16) for u in U]
            gl = [_dot_nt(zl[u], zl[u]) for u in U]
            att = [att[u] + mask_ref[l] * gl[u] for u in U]
        ds = [_dot_tn(v_bf[u], (kc[u] * ex[u][C:2 * C, :]).astype(BF16)) for u in U]
        ov = [_dot(att[u].astype(BF16), v_bf[u]) for u in U]
        st = st_s[...]
        for u in U:
            eb = ex[u][0:C, :]
            o_s[u * C:(u + 1) * C, :] = _dot_nt((qc[u] * eb).astype(BF16), st.astype(BF16)) + ov[u]
            st = st * eb[C - 1:C, :] + ds[u]
        st_s[...] = st
        return carry

    n_groups = seq // (C * GLA_UNROLL)
    lax.fori_loop(0, n_groups, body, 0)
    finish(n_groups - 1)


def _gla(proj, w_alpha2, b_alpha, norm_w, batch, seq):
    M = proj.shape[0]
    H = GLA_HEADS
    assert seq % (CHUNK * GLA_UNROLL) == 0
    mcat, upper, mask = _gla_constants()
    wa = jnp.zeros((LANES, H * GLA_DK), F32).at[S_LR:S_LR + GLA_RANK, :].set(w_alpha2)
    full = lambda a: pl.BlockSpec(a.shape, lambda b, h: (0,) * a.ndim)

    def tiles(c0, width, n):
        return [pl.BlockSpec((seq, LANES), lambda b, h, t=t: (b, c0 // LANES + h * (width // LANES) + t))
                for t in range(n)]

    n_in = 2 * N_QK_TILES + 2 * N_VR_TILES
    return pl.pallas_call(
        functools.partial(_gla_kernel, seq=seq),
        grid=(batch, H),
        in_specs=[*tiles(C_GQ, GLA_DK, N_QK_TILES), *tiles(C_GK, GLA_DK, N_QK_TILES),
                  *tiles(C_GV, GLA_DV, N_VR_TILES), *tiles(C_GR, GLA_DV, N_VR_TILES),
                  pl.BlockSpec((seq, LANES), lambda b, h: (b, C_LR // LANES)),
                  pl.BlockSpec((LANES, GLA_DK), lambda b, h: (0, h)),
                  pl.BlockSpec((1, GLA_DK), lambda b, h: (0, h)),
                  pl.BlockSpec((1, GLA_DV), lambda b, h: (0, 0)),
                  full(mcat), full(upper), full(mask)],
        out_specs=pl.BlockSpec((seq, GLA_DV), lambda b, h: (b, h)),
        out_shape=jax.ShapeDtypeStruct((M, H * GLA_DV), BF16),
        scratch_shapes=[pltpu.VMEM((seq, GLA_DK), F32), pltpu.VMEM((GLA_DV, GLA_DK), F32),
                        pltpu.VMEM((GLA_UNROLL * CHUNK, GLA_DV), F32)],
        compiler_params=_cparams(("parallel", "parallel")),
        name="gla",
    )(*([proj] * (n_in + 1)), wa, b_alpha.reshape(1, -1), norm_w.reshape(1, GLA_DV),
      jnp.asarray(mcat, BF16), jnp.asarray(upper), jnp.asarray(mask))


MERGE_ROW_BLOCKS = 2


def _merge_kernel(odn_ref, ogla_ref, pa_ref, pb_ref, gd_ref, gdn_ref, gg_ref, ggn_ref, o_ref):
    def gate(main_ref, next_ref, rows):
        n = main_ref.shape[1] // LANES
        tiles = [main_ref[rows, t * LANES:(t + 1) * LANES] for t in range(n)] + [next_ref[rows, :]]
        return jax.nn.sigmoid(jnp.concatenate(_lane_shift(tiles, GATE_SHIFT), axis=1))

    pa = pa_ref[...].astype(BF16)
    pb = None
    rb = o_ref.shape[0] // MERGE_ROW_BLOCKS
    for r in range(MERGE_ROW_BLOCKS):
        rows = slice(r * rb, (r + 1) * rb)
        ya = _dot(odn_ref[rows, :], pa)
        if pb is None:
            pb = pb_ref[...].astype(BF16)
        yb = _dot(ogla_ref[rows, :], pb)
        o_ref[rows, :] = (gate(gd_ref, gdn_ref, rows) * ya + gate(gg_ref, ggn_ref, rows) * yb).astype(o_ref.dtype)


def _merge(o_dn, o_gla, pa, pb, proj, tm=2048, tn=256):
    M, K = o_dn.shape
    N = pa.shape[1]
    tm = min(tm, M)
    lt = tn // LANES
    main = lambda c0: pl.BlockSpec((tm, tn), lambda i, j: (i, c0 // tn + j))
    nxt = lambda c0: pl.BlockSpec((tm, LANES), lambda i, j: (i, c0 // LANES + (j + 1) * lt))
    assert (C_GATE_DN - GATE_SHIFT) % tn == 0 and (C_GATE_GLA - GATE_SHIFT) % tn == 0
    assert tm % (MERGE_ROW_BLOCKS * 16) == 0
    return pl.pallas_call(
        _merge_kernel,
        grid=(M // tm, N // tn),
        in_specs=[pl.BlockSpec((tm, K), lambda i, j: (i, 0), pipeline_mode=pl.Buffered(1)),
                  pl.BlockSpec((tm, K), lambda i, j: (i, 0), pipeline_mode=pl.Buffered(1)),
                  pl.BlockSpec((K, tn), lambda i, j: (0, j)),
                  pl.BlockSpec((K, tn), lambda i, j: (0, j)),
                  main(C_GATE_DN), nxt(C_GATE_DN), main(C_GATE_GLA), nxt(C_GATE_GLA)],
        out_specs=pl.BlockSpec((tm, tn), lambda i, j: (i, j)),
        out_shape=jax.ShapeDtypeStruct((M, N), BF16),
        compiler_params=_cparams(("parallel", "arbitrary")),
        name="merge",
    )(o_dn, o_gla, pa, pb, proj, proj, proj, proj)


def _mm_res_kernel(a_ref, b_ref, r_ref, o_ref):
    o_ref[...] = r_ref[...] + _dot(a_ref[...], b_ref[...].astype(BF16))


def _matmul_residual(a, b, res, tm=1024, tn=512, name="outproj"):
    M, K = a.shape
    N = b.shape[1]
    tm = min(tm, M)
    return pl.pallas_call(
        _mm_res_kernel,
        grid=(M // tm, N // tn),
        in_specs=[pl.BlockSpec((tm, K), lambda i, j: (i, 0)),
                  pl.BlockSpec((K, tn), lambda i, j: (0, j)),
                  pl.BlockSpec((tm, tn), lambda i, j: (i, j))],
        out_specs=pl.BlockSpec((tm, tn), lambda i, j: (i, j)),
        out_shape=jax.ShapeDtypeStruct((M, N), F32),
        compiler_params=_cparams(("parallel", "arbitrary")),
        name=name,
    )(a, b, res)


FFN_ROW_BLOCKS = 4


def _ffn_in_kernel(h_ref, wg_ref, wu_ref, cw_ref, cb_ref, o_ref):
    wg = wg_ref[...].astype(BF16)
    wu = None
    cw = cw_ref[...]
    cb = cb_ref[...]
    rb = h_ref.shape[0] // FFN_ROW_BLOCKS
    tail = jnp.zeros((HALO, wg.shape[1]), F32)
    for r in range(FFN_ROW_BLOCKS):
        h = h_ref[r * rb:(r + 1) * rb, :]
        gate = _dot(h, wg)
        if wu is None:
            wu = wu_ref[...].astype(BF16)
        up = _dot(h, wu)
        ext = jnp.concatenate([tail, gate], axis=0)
        y = gate * cw[FFN_CONV - 1:FFN_CONV, :] + cb
        for s in range(1, FFN_CONV):
            y = y + pltpu.roll(ext, s, 0)[HALO:, :] * cw[FFN_CONV - 1 - s:FFN_CONV - s, :]
        o_ref[r * rb:(r + 1) * rb, :] = (_silu(y) * up).astype(o_ref.dtype)
        tail = gate[rb - HALO:, :]


def _ffn_in(h, w_in, conv_w, conv_b, seq, tn=256):
    M, K = h.shape
    N = w_in.shape[1] // 2
    nb = N // tn
    assert seq % (FFN_ROW_BLOCKS * HALO) == 0
    return pl.pallas_call(
        _ffn_in_kernel,
        grid=(M // seq, nb),
        in_specs=[pl.BlockSpec((seq, K), lambda i, j: (i, 0), pipeline_mode=pl.Buffered(1)),
                  pl.BlockSpec((K, tn), lambda i, j: (0, j)),
                  pl.BlockSpec((K, tn), lambda i, j: (0, nb + j)),
                  pl.BlockSpec((FFN_CONV, tn), lambda i, j: (0, j)),
                  pl.BlockSpec((1, tn), lambda i, j: (0, j))],
        out_specs=pl.BlockSpec((seq, tn), lambda i, j: (i, j)),
        out_shape=jax.ShapeDtypeStruct((M, N), BF16),
        compiler_params=_cparams(("parallel", "arbitrary")),
        name="ffn_in",
    )(h, w_in, w_in, conv_w, conv_b.reshape(1, N))


def _ffn_out_kernel(a_ref, b_ref, r_ref, o_ref, acc_ref):
    kk = pl.program_id(2)

    @pl.when(kk == 0)
    def _():
        acc_ref[...] = r_ref[...]

    acc_ref[...] += _dot(a_ref[...], b_ref[...])

    @pl.when(kk == pl.num_programs(2) - 1)
    def _():
        o_ref[...] = acc_ref[...]


def _ffn_out(a, b, res, tm=512, tn=512, ksplit=1):
    M, K = a.shape
    N = b.shape[1]
    tm = min(tm, M)
    tk = K // ksplit
    return pl.pallas_call(
        _ffn_out_kernel,
        grid=(M // tm, N // tn, ksplit),
        in_specs=[pl.BlockSpec((tm, tk), lambda i, j, k: (i, k)),
                  pl.BlockSpec((tk, tn), lambda i, j, k: (k, j)),
                  pl.BlockSpec((tm, tn), lambda i, j, k: (i, j))],
        out_specs=pl.BlockSpec((tm, tn), lambda i, j, k: (i, j)),
        out_shape=jax.ShapeDtypeStruct((M, N), F32),
        scratch_shapes=[pltpu.VMEM((tm, tn), F32)],
        compiler_params=_cparams(("parallel", "arbitrary", "arbitrary")),
        name="ffn_out",
    )(a, b, res)


def _layer(x2d, batch, seq, norm_mix_w, w_in, dn_conv_w, dn_a_log, dn_dt_bias, dn_norm_w,
           gla_w_alpha2, gla_b_alpha, gla_norm_w, w_branch_dn, w_branch_gla, w_out,
           norm_ffn_w, w_ffn_in, ffn_conv_w, ffn_conv_b, w_ffn_out):
    h = _rmsnorm(x2d, norm_mix_w, BF16)
    proj = _inproj(h, w_in.T)
    o_dn = _deltanet(proj, dn_conv_w, dn_a_log, dn_dt_bias, dn_norm_w, batch, seq)
    o_gla = _gla(proj, gla_w_alpha2, gla_b_alpha, gla_norm_w, batch, seq)
    merged = _merge(o_dn, o_gla, w_branch_dn, w_branch_gla, proj)
    x1 = _matmul_residual(merged, w_out, x2d)

    h2 = _rmsnorm(x1, norm_ffn_w, BF16)
    act = _ffn_in(h2, w_ffn_in, ffn_conv_w, ffn_conv_b, seq)
    return _ffn_out(act, w_ffn_out.astype(BF16), x1)


def kernel(x, norm_mix_w, w_in, dn_conv_w, dn_a_log, dn_dt_bias, dn_norm_w, gla_w_alpha2, gla_b_alpha, gla_norm_w, w_branch_dn, w_branch_gla, w_out, norm_ffn_w, w_ffn_in, ffn_conv_w, ffn_conv_b, w_ffn_out, norm_final_w):
    batch, seq, D = x.shape
    x2d = x.reshape(batch * seq, D)
    for l in range(norm_mix_w.shape[0]):
        x2d = _layer(x2d, batch, seq, norm_mix_w[l], w_in[l], dn_conv_w[l], dn_a_log[l], dn_dt_bias[l],
                     dn_norm_w[l], gla_w_alpha2[l], gla_b_alpha[l], gla_norm_w[l], w_branch_dn[l],
                     w_branch_gla[l], w_out[l], norm_ffn_w[l], w_ffn_in[l], ffn_conv_w[l], ffn_conv_b[l],
                     w_ffn_out[l])
    return _rmsnorm(x2d, norm_final_w, F32).reshape(batch, seq, D)
```

```python
import functools

import numpy as np
import jax
import jax.numpy as jnp
from jax import lax
from jax.experimental import pallas as pl
from jax.experimental.pallas import tpu as pltpu

F32 = jnp.float32
BF16 = jnp.bfloat16
HI = lax.Precision.HIGHEST

EPS = 1e-6
CHUNK = 64
DN_HEADS = 16
DN_D = 128
DN_CONV = 4
GLA_HEADS = 8
GLA_DK = 128
GLA_DV = 256
GLA_RANK = 16
GLA_TAU = 16.0
FFN_CONV = 3
LANES = 128
HALO = 8

C_DQ, C_DK, C_DV, C_DZ = 0, 2048, 4096, 6144
C_DGATE = 8192
C_GQ, C_GK, C_GV, C_GR = 8224, 9248, 10272, 12320
C_LR = 14368
C_GATE_DN, C_GATE_GLA = 14384, 18480
S_B, S_A = 0, 16
S_LR = C_LR % LANES
GLA_SHIFT = C_GQ % LANES
GATE_SHIFT = C_GATE_DN % LANES
assert C_GK % LANES == C_GV % LANES == C_GR % LANES == GLA_SHIFT and C_GATE_GLA % LANES == GATE_SHIFT

VMEM_LIMIT = 56 * 1024 * 1024


def _cparams(sem):
    return pltpu.CompilerParams(dimension_semantics=sem, vmem_limit_bytes=VMEM_LIMIT)


def _silu(x):
    return x * jax.nn.sigmoid(x)


def _softplus(x):
    return jnp.maximum(x, 0.0) + jnp.log1p(jnp.exp(-jnp.abs(x)))


def _dot(a, b, precision=None):
    return jnp.dot(a, b, precision=precision, preferred_element_type=F32)


def _dot_nt(a, b, precision=None):
    return lax.dot_general(a, b, (((1,), (1,)), ((), ())), precision=precision, preferred_element_type=F32)


def _dot_tn(a, b, precision=None):
    return lax.dot_general(a, b, (((0,), (0,)), ((), ())), precision=precision, preferred_element_type=F32)


def _rmsnorm_kernel(x_ref, w_ref, o_ref):
    x = x_ref[...]
    ms = jnp.mean(x * x, axis=-1, keepdims=True)
    o_ref[...] = (x * lax.rsqrt(ms + EPS) * w_ref[...]).astype(o_ref.dtype)


def _rmsnorm(x, w, out_dtype, tm=512):
    M, D = x.shape
    tm = min(tm, M)
    return pl.pallas_call(
        _rmsnorm_kernel,
        grid=(M // tm,),
        in_specs=[pl.BlockSpec((tm, D), lambda i: (i, 0)), pl.BlockSpec((1, D), lambda i: (0, 0))],
        out_specs=pl.BlockSpec((tm, D), lambda i: (i, 0)),
        out_shape=jax.ShapeDtypeStruct((M, D), out_dtype),
        compiler_params=_cparams(("parallel",)),
        name="rmsnorm",
    )(x, w.reshape(1, D))


def _inproj_kernel(a_ref, wt_ref, o_ref):
    hn = wt_ref.shape[0] // 2
    for c in (slice(0, hn), slice(hn, 2 * hn)):
        o_ref[:, c] = _dot_nt(a_ref[...], wt_ref[c, :].astype(BF16))


def _inproj(a, wt, tm=2048, tn=512):
    M, K = a.shape
    N = wt.shape[0]
    tm = min(tm, M)
    return pl.pallas_call(
        _inproj_kernel,
        grid=(M // tm, pl.cdiv(N, tn)),
        in_specs=[pl.BlockSpec((tm, K), lambda i, j: (i, 0), pipeline_mode=pl.Buffered(1)),
                  pl.BlockSpec((tn, K), lambda i, j: (j, 0))],
        out_specs=pl.BlockSpec((tm, tn), lambda i, j: (i, j)),
        out_shape=jax.ShapeDtypeStruct((M, N), F32),
        compiler_params=_cparams(("parallel", "arbitrary")),
        name="inproj",
    )(a, wt)


def _lane_shift(tiles, shift):
    lane = lax.broadcasted_iota(jnp.int32, tiles[0].shape, 1)
    rolled = [pltpu.roll(t, LANES - shift, 1) for t in tiles]
    return [jnp.where(lane < LANES - shift, rolled[i], rolled[i + 1]) for i in range(len(tiles) - 1)]


def _l2norm(x):
    return x * lax.rsqrt(jnp.sum(x * x, axis=-1, keepdims=True) + EPS)


def _split3(x):
    hi = x.astype(BF16)
    r = x - hi.astype(F32)
    mid = r.astype(BF16)
    lo = (r - mid.astype(F32)).astype(BF16)
    return hi, mid, lo


def _split2(x):
    hi = x.astype(BF16)
    return hi, (x - hi.astype(F32)).astype(BF16)


DN_UNROLL = 8
DN_P_PIECES = 4
DN_P_PER_STAGE = 2


def _dn_kernel(q_ref, k_ref, v_ref, z_ref, small_ref, cwq_ref, cwk_ref, cwv_ref, alog_ref, dtb_ref, nw_ref,
               o_ref, gates_s, qs_s, k_s, v_s, beta_s, bc_s, lhs_s, add_s, ebl_s, *, seq):
    h = pl.program_id(1)
    C = CHUNK
    D = DN_D
    U = range(DN_UNROLL)
    n_groups = seq // (C * DN_UNROLL)
    row = lax.broadcasted_iota(jnp.int32, (seq, D), 0)

    @pl.when(h == 0)
    def _():
        small = small_ref[...]
        lane = lax.broadcasted_iota(jnp.int32, small.shape, 1)
        x = -jnp.exp(alog_ref[...]) * _softplus(small + dtb_ref[...])
        in_chunk = row % C
        s = 1
        while s < C:
            x = x + jnp.where(in_chunk >= s, pltpu.roll(x, s, 0), 0.0)
            s *= 2
        gates_s[...] = jnp.where(lane < S_A, jax.nn.sigmoid(small), x)

    sr = lax.broadcasted_iota(jnp.int32, (3 * LANES, 2 * LANES), 0) % LANES
    sc = lax.broadcasted_iota(jnp.int32, (3 * LANES, 2 * LANES), 1)
    sel = (sr == jnp.where(sc < LANES, h + S_B, h + S_A)).astype(BF16)
    G = C * DN_UNROLL

    def pass_p(g):
        rep = _dot(jnp.concatenate(_split3(gates_s[pl.ds(pl.multiple_of(g * G, G), G), :]), axis=1), sel)
        beta_s[pl.ds(pl.multiple_of(g * G, G), G), :] = rep[:, :LANES]
        bc_s[pl.ds(pl.multiple_of(g * G, G), G), :] = rep[:, LANES:]
        yield
        R = G // DN_P_PIECES
        for piece in range(DN_P_PIECES):
            r0 = g * G + piece * R
            rows = pl.ds(pl.multiple_of(r0, R), R)
            halo_rows = pl.ds(pl.multiple_of(jnp.maximum(r0 - HALO, 0), HALO), HALO)
            for x_ref, cw_ref, out_s, norm in ((q_ref, cwq_ref, qs_s, D ** -0.5), (k_ref, cwk_ref, k_s, 1.0),
                                               (v_ref, cwv_ref, v_s, None)):
                cw = cw_ref[...]
                halo = jnp.where(r0 == 0, 0.0, x_ref[halo_rows, :])
                x = x_ref[rows, :]
                ext = jnp.concatenate([halo, x], axis=0)
                y = x * cw[DN_CONV - 1:DN_CONV, :]
                for s in range(1, DN_CONV):
                    y = y + pltpu.roll(ext, s, 0)[HALO:, :] * cw[DN_CONV - 1 - s:DN_CONV - s, :]
                y = _silu(y)
                out_s[rows, :] = y if norm is None else _l2norm(y) * norm
                yield

    ri = lax.broadcasted_iota(jnp.int32, (C, C), 0)
    ci = lax.broadcasted_iota(jnp.int32, (C, C), 1)
    tril = ri >= ci
    strict = ri > ci
    eye = (ri == ci).astype(F32)
    nw = nw_ref[...]

    def pass_a(g):
        ns = [g * DN_UNROLL + u for u in U]
        sls = [pl.ds(pl.multiple_of(n * C, C), C) for n in ns]
        qc = [qs_s[s, :] for s in sls]
        kc = [k_s[s, :] for s in sls]
        vc = [v_s[s, :] for s in sls]
        bt = [beta_s[s, :] for s in sls]
        bc = [bc_s[s, :] for s in sls]
        kb = [kc[u] * bt[u] for u in U]
        gram = [_dot_nt(jnp.concatenate([kb[u], qc[u]], axis=0).astype(BF16), kc[u].astype(BF16)) for u in U]
        yield
        dm = [bc[u][:, :C] - jnp.transpose(bc[u])[:C, :] for u in U]
        decay = [jnp.where(tril, jnp.exp(jnp.minimum(dm[u], 0.0)), 0.0) for u in U]
        eb = [jnp.exp(bc[u]) for u in U]
        a = [jnp.where(strict, gram[u][:C] * decay[u], 0.0) for u in U]
        att = [(gram[u][C:] * decay[u]).astype(BF16) for u in U]
        p = [eye - a[u] for u in U]
        a_bf = [a[u].astype(BF16) for u in U]
        x = [_dot(a_bf[u], a_bf[u]) for u in U]
        yield
        for _ in range(4):
            px = [_dot(jnp.concatenate([p[u], x[u]], axis=0).astype(BF16), x[u].astype(BF16)) for u in U]
            p = [p[u] + px[u][:C] for u in U]
            x = [px[u][C:] for u in U]
            yield
        px = [_dot(p[u].astype(BF16), x[u].astype(BF16)) for u in U]
        p = [(p[u] + px[u]).astype(BF16) for u in U]
        yield
        rhs = [jnp.concatenate([kb[u] * eb[u], vc[u] * bt[u]], axis=1).astype(BF16) for u in U]
        wu = [_dot(p[u], rhs[u]).astype(BF16) for u in U]
        yield
        kd = [(kc[u] * jnp.exp(bc[u][C - 1:C, :] - bc[u])).astype(BF16) for u in U]
        aw = [_dot(att[u], wu[u]) for u in U]
        yield
        mn = [_dot_tn(kd[u], wu[u]) for u in U]
        for u in U:
            n = ns[u]
            lhs_s[n, 0:D, :] = mn[u][:, :D].astype(BF16)
            lhs_s[n, D:D + C, :] = (qc[u] * eb[u] - aw[u][:, :D]).astype(BF16)
            add_s[n, 0:D, :] = mn[u][:, D:]
            add_s[n, D:D + C, :] = aw[u][:, D:]
            ebl_s[n] = jnp.broadcast_to(eb[u][C - 1:C, :], (8, LANES))
        yield

    def finish(n, o):
        sl = pl.ds(pl.multiple_of(n * C, C), C)
        y = o * lax.rsqrt(jnp.mean(o * o, axis=-1, keepdims=True) + EPS) * nw * _silu(z_ref[sl, :])
        o_ref[sl, :] = y.astype(o_ref.dtype)

    def step_b(n, carry):
        S, o_prev = carry
        finish(jnp.maximum(n - 1, 0), o_prev)
        ps = _dot(lhs_s[n], S.astype(BF16))
        add = add_s[n]
        o = ps[D:] + add[D:]
        S = ebl_s[n][0:1, :] * S + (add[:D] - ps[:D])
        return S, o

    def interleave(g_a, g_b, g_p, carry):
        stages = pass_a(g_a) if g_a is not None else iter(())
        steps = [g_b * DN_UNROLL + u for u in U] if g_b is not None else []
        pieces = pass_p(g_p) if g_p is not None else iter(())
        next(stages, None)
        for _ in stages:
            if steps:
                carry = step_b(steps.pop(0), carry)
            for _ in range(DN_P_PER_STAGE):
                next(pieces, None)
        for n in steps:
            carry = step_b(n, carry)
        for _ in pieces:
            pass
        return carry

    assert n_groups >= 2
    carry = (jnp.zeros((D, D), F32), jnp.zeros((C, D), F32))
    for _ in pass_p(0):
        pass
    carry = interleave(0, None, 1, carry)
    carry = lax.fori_loop(1, n_groups - 1, lambda g, c: interleave(g, g - 1, g + 1, c), carry)
    carry = interleave(n_groups - 1, n_groups - 2, None, carry)
    _, o_last = interleave(None, n_groups - 1, None, carry)
    finish(seq // C - 1, o_last)


def _deltanet(proj, conv_w, a_log, dt_bias, norm_w, batch, seq):
    M = proj.shape[0]
    H = DN_HEADS
    n_chunks = seq // CHUNK
    assert seq % (CHUNK * DN_UNROLL) == 0
    blk = lambda c0: pl.BlockSpec((seq, DN_D), lambda b, h: (b, c0 // DN_D + h))
    cw = lambda c0: pl.BlockSpec((DN_CONV, DN_D), lambda b, h: (0, c0 // DN_D + h))
    row = pl.BlockSpec((1, LANES), lambda b, h: (0, 0))
    alog = jnp.zeros((1, LANES), F32).at[0, S_A:S_A + H].set(a_log)
    dtb = jnp.zeros((1, LANES), F32).at[0, S_A:S_A + H].set(dt_bias)
    seq_f32 = pltpu.VMEM((seq, DN_D), F32)
    return pl.pallas_call(
        functools.partial(_dn_kernel, seq=seq),
        grid=(batch, H),
        in_specs=[blk(C_DQ), blk(C_DK), blk(C_DV), blk(C_DZ),
                  pl.BlockSpec((seq, LANES), lambda b, h: (b, C_DGATE // LANES)),
                  cw(0), cw(2048), cw(4096), row, row, row],
        out_specs=pl.BlockSpec((seq, DN_D), lambda b, h: (b, h)),
        out_shape=jax.ShapeDtypeStruct((M, H * DN_D), BF16),
        scratch_shapes=[seq_f32] * 6 + [pltpu.VMEM((n_chunks, DN_D + CHUNK, DN_D), BF16),
                                        pltpu.VMEM((n_chunks, DN_D + CHUNK, DN_D), F32),
                                        pltpu.VMEM((n_chunks, 8, LANES), F32)],
        compiler_params=_cparams(("parallel", "arbitrary")),
        name="deltanet",
    )(proj, proj, proj, proj, proj, conv_w, conv_w, conv_w, alog, dtb, norm_w.reshape(1, DN_D))


GLA_LEVELS = (32, 16, 8, 4, 2, 1)
GLA_UNROLL = 4


def _gla_constants():
    C = CHUNK
    i = np.arange(C)[:, None]
    m = np.arange(C)[None, :]
    blocks = [(m <= i), (m > i)]
    upper_rows, masks = [], []
    for s in GLA_LEVELS:
        p = (i // (2 * s)) * (2 * s)
        m0 = p + s - 1
        up = (i - p) >= s
        blocks.append(np.where(up, (m > m0) & (m <= i), (m > i) & (m <= m0)))
        upper_rows.append(np.broadcast_to(up, (C, LANES)))
        j = m
        pj = (j // (2 * s)) * (2 * s)
        masks.append(up & ((j - pj) < s) & (p == pj))
    masks.append(i == m)
    mcat = np.concatenate(blocks, axis=0).astype(np.float32)
    upper = np.stack(upper_rows).astype(np.float32)
    mask = np.stack(masks).astype(np.float32)
    return mcat, upper, mask


N_QK_TILES = GLA_DK // LANES + 1
N_VR_TILES = GLA_DV // LANES + 1


def _gla_kernel(*refs, seq):
    q_t, refs = refs[:N_QK_TILES], refs[N_QK_TILES:]
    k_t, refs = refs[:N_QK_TILES], refs[N_QK_TILES:]
    v_t, refs = refs[:N_VR_TILES], refs[N_VR_TILES:]
    r_t, refs = refs[:N_VR_TILES], refs[N_VR_TILES:]
    lr_ref, wa_ref, ba_ref, nw_ref, mcat_ref, upper_ref, mask_ref, o_ref, gk_s, st_s, o_s = refs
    C = CHUNK

    def rows(tile_refs, s):
        return jnp.concatenate(_lane_shift([t[s, :] for t in tile_refs], GLA_SHIFT), axis=1)

    x_alpha = _dot(lr_ref[...], wa_ref[...], HI) + ba_ref[...]
    gk_s[...] = -_softplus(-x_alpha) / GLA_TAU
    st_s[...] = jnp.zeros_like(st_s)
    mcat = mcat_ref[...]
    nw = nw_ref[...]
    n_lvl = len(GLA_LEVELS)

    U = range(GLA_UNROLL)
    o_s[...] = jnp.zeros_like(o_s)

    def finish(g):
        for u in U:
            sl = pl.ds(pl.multiple_of((g * GLA_UNROLL + u) * C, C), C)
            o = o_s[u * C:(u + 1) * C, :]
            y = o * lax.rsqrt(jnp.mean(o * o, axis=-1, keepdims=True) + EPS) * nw * _silu(rows(r_t, sl))
            o_ref[sl, :] = y.astype(o_ref.dtype)

    def body(g, carry):
        finish(jnp.maximum(g - 1, 0))
        sls = [pl.ds(pl.multiple_of((g * GLA_UNROLL + u) * C, C), C) for u in U]
        qc = [rows(q_t, s) * (GLA_DK ** -0.5) for s in sls]
        kc = [rows(k_t, s) for s in sls]
        v_bf = [rows(v_t, s).astype(BF16) for s in sls]
        e = [_dot(mcat, jnp.concatenate(_split2(gk_s[s, :]), axis=1)) for s in sls]
        ex = [jnp.exp(e[u][:, :GLA_DK] + e[u][:, GLA_DK:]) for u in U]
        k_bf = [kc[u].astype(BF16) for u in U]
        att = [mask_ref[n_lvl] * _dot_nt(qc[u].astype(BF16), k_bf[u]) for u in U]
        for l in range(n_lvl):
            up = upper_ref[l] > 0.5
            zl = [(jnp.where(up, qc[u], kc[u]) * ex[u][(2 + l) * C:(3 + l) * C, :]).astype(BF16) for u in U]
            gl = [_dot_nt(zl[u], zl[u]) for u in U]
            att = [att[u] + mask_ref[l] * gl[u] for u in U]
        ds = [_dot_tn(v_bf[u], (kc[u] * ex[u][C:2 * C, :]).astype(BF16)) for u in U]
        ov = [_dot(att[u].astype(BF16), v_bf[u]) for u in U]
        st = st_s[...]
        for u in U:
            eb = ex[u][0:C, :]
            o_s[u * C:(u + 1) * C, :] = _dot_nt((qc[u] * eb).astype(BF16), st.astype(BF16)) + ov[u]
            st = st * eb[C - 1:C, :] + ds[u]
        st_s[...] = st
        return carry

    n_groups = seq // (C * GLA_UNROLL)
    lax.fori_loop(0, n_groups, body, 0)
    finish(n_groups - 1)


def _gla(proj, w_alpha2, b_alpha, norm_w, batch, seq):
    M = proj.shape[0]
    H = GLA_HEADS
    assert seq % (CHUNK * GLA_UNROLL) == 0
    mcat, upper, mask = _gla_constants()
    wa = jnp.zeros((LANES, H * GLA_DK), F32).at[S_LR:S_LR + GLA_RANK, :].set(w_alpha2)
    full = lambda a: pl.BlockSpec(a.shape, lambda b, h: (0,) * a.ndim)

    def tiles(c0, width, n):
        return [pl.BlockSpec((seq, LANES), lambda b, h, t=t: (b, c0 // LANES + h * (width // LANES) + t))
                for t in range(n)]

    n_in = 2 * N_QK_TILES + 2 * N_VR_TILES
    return pl.pallas_call(
        functools.partial(_gla_kernel, seq=seq),
        grid=(batch, H),
        in_specs=[*tiles(C_GQ, GLA_DK, N_QK_TILES), *tiles(C_GK, GLA_DK, N_QK_TILES),
                  *tiles(C_GV, GLA_DV, N_VR_TILES), *tiles(C_GR, GLA_DV, N_VR_TILES),
                  pl.BlockSpec((seq, LANES), lambda b, h: (b, C_LR // LANES)),
                  pl.BlockSpec((LANES, GLA_DK), lambda b, h: (0, h)),
                  pl.BlockSpec((1, GLA_DK), lambda b, h: (0, h)),
                  pl.BlockSpec((1, GLA_DV), lambda b, h: (0, 0)),
                  full(mcat), full(upper), full(mask)],
        out_specs=pl.BlockSpec((seq, GLA_DV), lambda b, h: (b, h)),
        out_shape=jax.ShapeDtypeStruct((M, H * GLA_DV), BF16),
        scratch_shapes=[pltpu.VMEM((seq, GLA_DK), F32), pltpu.VMEM((GLA_DV, GLA_DK), F32),
                        pltpu.VMEM((GLA_UNROLL * CHUNK, GLA_DV), F32)],
        compiler_params=_cparams(("parallel", "parallel")),
        name="gla",
    )(*([proj] * (n_in + 1)), wa, b_alpha.reshape(1, -1), norm_w.reshape(1, GLA_DV),
      jnp.asarray(mcat, BF16), jnp.asarray(upper), jnp.asarray(mask))


MERGE_ROW_BLOCKS = 2


def _merge_kernel(odn_ref, ogla_ref, pa_ref, pb_ref, gd_ref, gdn_ref, gg_ref, ggn_ref, o_ref):
    def gate(main_ref, next_ref, rows):
        n = main_ref.shape[1] // LANES
        tiles = [main_ref[rows, t * LANES:(t + 1) * LANES] for t in range(n)] + [next_ref[rows, :]]
        return jax.nn.sigmoid(jnp.concatenate(_lane_shift(tiles, GATE_SHIFT), axis=1))

    pa = pa_ref[...].astype(BF16)
    pb = None
    rb = o_ref.shape[0] // MERGE_ROW_BLOCKS
    for r in range(MERGE_ROW_BLOCKS):
        rows = slice(r * rb, (r + 1) * rb)
        ya = _dot(odn_ref[rows, :], pa)
        if pb is None:
            pb = pb_ref[...].astype(BF16)
        yb = _dot(ogla_ref[rows, :], pb)
        o_ref[rows, :] = (gate(gd_ref, gdn_ref, rows) * ya + gate(gg_ref, ggn_ref, rows) * yb).astype(o_ref.dtype)


def _merge(o_dn, o_gla, pa, pb, proj, tm=2048, tn=256):
    M, K = o_dn.shape
    N = pa.shape[1]
    tm = min(tm, M)
    lt = tn // LANES
    main = lambda c0: pl.BlockSpec((tm, tn), lambda i, j: (i, c0 // tn + j))
    nxt = lambda c0: pl.BlockSpec((tm, LANES), lambda i, j: (i, c0 // LANES + (j + 1) * lt))
    assert (C_GATE_DN - GATE_SHIFT) % tn == 0 and (C_GATE_GLA - GATE_SHIFT) % tn == 0
    assert tm % (MERGE_ROW_BLOCKS * 16) == 0
    return pl.pallas_call(
        _merge_kernel,
        grid=(M // tm, N // tn),
        in_specs=[pl.BlockSpec((tm, K), lambda i, j: (i, 0), pipeline_mode=pl.Buffered(1)),
                  pl.BlockSpec((tm, K), lambda i, j: (i, 0), pipeline_mode=pl.Buffered(1)),
                  pl.BlockSpec((K, tn), lambda i, j: (0, j)),
                  pl.BlockSpec((K, tn), lambda i, j: (0, j)),
                  main(C_GATE_DN), nxt(C_GATE_DN), main(C_GATE_GLA), nxt(C_GATE_GLA)],
        out_specs=pl.BlockSpec((tm, tn), lambda i, j: (i, j)),
        out_shape=jax.ShapeDtypeStruct((M, N), BF16),
        compiler_params=_cparams(("parallel", "arbitrary")),
        name="merge",
    )(o_dn, o_gla, pa, pb, proj, proj, proj, proj)


def _mm_res_kernel(a_ref, b_ref, r_ref, o_ref):
    o_ref[...] = r_ref[...] + _dot(a_ref[...], b_ref[...].astype(BF16))


def _matmul_residual(a, b, res, tm=1024, tn=512, name="outproj"):
    M, K = a.shape
    N = b.shape[1]
    tm = min(tm, M)
    return pl.pallas_call(
        _mm_res_kernel,
        grid=(M // tm, N // tn),
        in_specs=[pl.BlockSpec((tm, K), lambda i, j: (i, 0)),
                  pl.BlockSpec((K, tn), lambda i, j: (0, j)),
                  pl.BlockSpec((tm, tn), lambda i, j: (i, j))],
        out_specs=pl.BlockSpec((tm, tn), lambda i, j: (i, j)),
        out_shape=jax.ShapeDtypeStruct((M, N), F32),
        compiler_params=_cparams(("parallel", "arbitrary")),
        name=name,
    )(a, b, res)


FFN_ROW_BLOCKS = 4


def _ffn_in_kernel(h_ref, wg_ref, wu_ref, cw_ref, cb_ref, w2_ref, o_ref, w2_bf_ref):
    w2_bf_ref[...] = w2_ref[...].astype(BF16)
    wg = wg_ref[...].astype(BF16)
    wu = None
    cw = cw_ref[...]
    cb = cb_ref[...]
    rb = h_ref.shape[0] // FFN_ROW_BLOCKS
    tail = jnp.zeros((HALO, wg.shape[1]), F32)
    for r in range(FFN_ROW_BLOCKS):
        h = h_ref[r * rb:(r + 1) * rb, :]
        gate = _dot(h, wg)
        if wu is None:
            wu = wu_ref[...].astype(BF16)
        up = _dot(h, wu)
        ext = jnp.concatenate([tail, gate], axis=0)
        y = gate * cw[FFN_CONV - 1:FFN_CONV, :] + cb
        for s in range(1, FFN_CONV):
            y = y + pltpu.roll(ext, s, 0)[HALO:, :] * cw[FFN_CONV - 1 - s:FFN_CONV - s, :]
        o_ref[r * rb:(r + 1) * rb, :] = (_silu(y) * up).astype(o_ref.dtype)
        tail = gate[rb - HALO:, :]


def _ffn_in(h, w_in, conv_w, conv_b, w_out, seq, tn=256):
    M, K = h.shape
    N = w_in.shape[1] // 2
    nb = N // tn
    n_steps = (M // seq) * nb
    w2_rows = w_out.shape[0] // n_steps
    assert seq % (FFN_ROW_BLOCKS * HALO) == 0
    assert w2_rows * n_steps == w_out.shape[0] and w2_rows % 16 == 0
    w2_spec = pl.BlockSpec((w2_rows, w_out.shape[1]), lambda i, j: (i * nb + j, 0))
    return pl.pallas_call(
        _ffn_in_kernel,
        grid=(M // seq, nb),
        in_specs=[pl.BlockSpec((seq, K), lambda i, j: (i, 0), pipeline_mode=pl.Buffered(1)),
                  pl.BlockSpec((K, tn), lambda i, j: (0, j)),
                  pl.BlockSpec((K, tn), lambda i, j: (0, nb + j)),
                  pl.BlockSpec((FFN_CONV, tn), lambda i, j: (0, j)),
                  pl.BlockSpec((1, tn), lambda i, j: (0, j)),
                  w2_spec],
        out_specs=[pl.BlockSpec((seq, tn), lambda i, j: (i, j)), w2_spec],
        out_shape=[jax.ShapeDtypeStruct((M, N), BF16), jax.ShapeDtypeStruct(w_out.shape, BF16)],
        compiler_params=_cparams(("parallel", "arbitrary")),
        name="ffn_in",
    )(h, w_in, w_in, conv_w, conv_b.reshape(1, N), w_out)


def _ffn_out_kernel(a_ref, b_ref, r_ref, o_ref, acc_ref):
    kk = pl.program_id(2)

    @pl.when(kk == 0)
    def _():
        acc_ref[...] = r_ref[...]

    acc_ref[...] += _dot(a_ref[...], b_ref[...])

    @pl.when(kk == pl.num_programs(2) - 1)
    def _():
        o_ref[...] = acc_ref[...]


def _ffn_out(a, b, res, tm=512, tn=512, ksplit=1):
    M, K = a.shape
    N = b.shape[1]
    tm = min(tm, M)
    tk = K // ksplit
    return pl.pallas_call(
        _ffn_out_kernel,
        grid=(M // tm, N // tn, ksplit),
        in_specs=[pl.BlockSpec((tm, tk), lambda i, j, k: (i, k)),
                  pl.BlockSpec((tk, tn), lambda i, j, k: (k, j)),
                  pl.BlockSpec((tm, tn), lambda i, j, k: (i, j))],
        out_specs=pl.BlockSpec((tm, tn), lambda i, j, k: (i, j)),
        out_shape=jax.ShapeDtypeStruct((M, N), F32),
        scratch_shapes=[pltpu.VMEM((tm, tn), F32)],
        compiler_params=_cparams(("parallel", "arbitrary", "arbitrary")),
        name="ffn_out",
    )(a, b, res)


def _layer(x2d, batch, seq, norm_mix_w, w_in, dn_conv_w, dn_a_log, dn_dt_bias, dn_norm_w,
           gla_w_alpha2, gla_b_alpha, gla_norm_w, w_branch_dn, w_branch_gla, w_out,
           norm_ffn_w, w_ffn_in, ffn_conv_w, ffn_conv_b, w_ffn_out):
    h = _rmsnorm(x2d, norm_mix_w, BF16)
    proj = _inproj(h, w_in.T)
    o_dn = _deltanet(proj, dn_conv_w, dn_a_log, dn_dt_bias, dn_norm_w, batch, seq)
    o_gla = _gla(proj, gla_w_alpha2, gla_b_alpha, gla_norm_w, batch, seq)
    merged = _merge(o_dn, o_gla, w_branch_dn, w_branch_gla, proj)
    x1 = _matmul_residual(merged, w_out, x2d)

    h2 = _rmsnorm(x1, norm_ffn_w, BF16)
    act, w_ffn_out_bf = _ffn_in(h2, w_ffn_in, ffn_conv_w, ffn_conv_b, w_ffn_out, seq)
    return _ffn_out(act, w_ffn_out_bf, x1)


def kernel(x, norm_mix_w, w_in, dn_conv_w, dn_a_log, dn_dt_bias, dn_norm_w, gla_w_alpha2, gla_b_alpha, gla_norm_w, w_branch_dn, w_branch_gla, w_out, norm_ffn_w, w_ffn_in, ffn_conv_w, ffn_conv_b, w_ffn_out, norm_final_w):
    batch, seq, D = x.shape
    x2d = x.reshape(batch * seq, D)
    for l in range(norm_mix_w.shape[0]):
        x2d = _layer(x2d, batch, seq, norm_mix_w[l], w_in[l], dn_conv_w[l], dn_a_log[l], dn_dt_bias[l],
                     dn_norm_w[l], gla_w_alpha2[l], gla_b_alpha[l], gla_norm_w[l], w_branch_dn[l],
                     w_branch_gla[l], w_out[l], norm_ffn_w[l], w_ffn_in[l], ffn_conv_w[l], ffn_conv_b[l],
                     w_ffn_out[l])
    return _rmsnorm(x2d, norm_final_w, F32).reshape(batch, seq, D)
```

```python
import functools

import numpy as np
import jax
import jax.numpy as jnp
from jax import lax
from jax.experimental import pallas as pl
from jax.experimental.pallas import tpu as pltpu

F32 = jnp.float32
BF16 = jnp.bfloat16
HI = lax.Precision.HIGHEST

EPS = 1e-6
CHUNK = 64
DN_HEADS = 16
DN_D = 128
DN_CONV = 4
GLA_HEADS = 8
GLA_DK = 128
GLA_DV = 256
GLA_RANK = 16
GLA_TAU = 16.0
FFN_CONV = 3
LANES = 128
HALO = 8

C_DQ, C_DK, C_DV, C_DZ = 0, 2048, 4096, 6144
C_DGATE = 8192
C_GQ, C_GK, C_GV, C_GR = 8224, 9248, 10272, 12320
C_LR = 14368
C_GATE_DN, C_GATE_GLA = 14384, 18480
S_B, S_A = 0, 16
S_LR = C_LR % LANES
GLA_SHIFT = C_GQ % LANES
GATE_SHIFT = C_GATE_DN % LANES
assert C_GK % LANES == C_GV % LANES == C_GR % LANES == GLA_SHIFT and C_GATE_GLA % LANES == GATE_SHIFT

VMEM_LIMIT = 56 * 1024 * 1024


def _cparams(sem):
    return pltpu.CompilerParams(dimension_semantics=sem, vmem_limit_bytes=VMEM_LIMIT)


def _silu(x):
    return x * jax.nn.sigmoid(x)


def _softplus(x):
    return jnp.maximum(x, 0.0) + jnp.log1p(jnp.exp(-jnp.abs(x)))


def _dot(a, b, precision=None):
    return jnp.dot(a, b, precision=precision, preferred_element_type=F32)


def _dot_nt(a, b, precision=None):
    return lax.dot_general(a, b, (((1,), (1,)), ((), ())), precision=precision, preferred_element_type=F32)


def _dot_tn(a, b, precision=None):
    return lax.dot_general(a, b, (((0,), (0,)), ((), ())), precision=precision, preferred_element_type=F32)


def _rmsnorm_kernel(x_ref, w_ref, o_ref):
    x = x_ref[...]
    ms = jnp.mean(x * x, axis=-1, keepdims=True)
    o_ref[...] = (x * lax.rsqrt(ms + EPS) * w_ref[...]).astype(o_ref.dtype)


def _rmsnorm(x, w, out_dtype, tm=512):
    M, D = x.shape
    tm = min(tm, M)
    return pl.pallas_call(
        _rmsnorm_kernel,
        grid=(M // tm,),
        in_specs=[pl.BlockSpec((tm, D), lambda i: (i, 0)), pl.BlockSpec((1, D), lambda i: (0, 0))],
        out_specs=pl.BlockSpec((tm, D), lambda i: (i, 0)),
        out_shape=jax.ShapeDtypeStruct((M, D), out_dtype),
        compiler_params=_cparams(("parallel",)),
        name="rmsnorm",
    )(x, w.reshape(1, D))


def _inproj_kernel(a_ref, wt_ref, o_ref, *, tail):
    j = pl.program_id(1)
    last = pl.num_programs(1) - 1

    @pl.when(jnp.logical_or(j < last, tail == 0))
    def _():
        hn = wt_ref.shape[0] // 2
        for c in (slice(0, hn), slice(hn, 2 * hn)):
            o_ref[:, c] = _dot_nt(a_ref[...], wt_ref[c, :].astype(BF16))

    if tail:
        @pl.when(j == last)
        def _():
            o_ref[:, :tail] = _dot_nt(a_ref[...], wt_ref[:tail, :].astype(BF16))


def _inproj(a, wt, tm=2048, tn=512):
    M, K = a.shape
    N = wt.shape[0]
    tm = min(tm, M)
    tail = -(-(N % tn) // LANES) * LANES
    return pl.pallas_call(
        functools.partial(_inproj_kernel, tail=tail),
        grid=(M // tm, pl.cdiv(N, tn)),
        in_specs=[pl.BlockSpec((tm, K), lambda i, j: (i, 0), pipeline_mode=pl.Buffered(1)),
                  pl.BlockSpec((tn, K), lambda i, j: (j, 0))],
        out_specs=pl.BlockSpec((tm, tn), lambda i, j: (i, j)),
        out_shape=jax.ShapeDtypeStruct((M, N), F32),
        compiler_params=_cparams(("parallel", "arbitrary")),
        name="inproj",
    )(a, wt)


def _lane_shift(tiles, shift):
    lane = lax.broadcasted_iota(jnp.int32, tiles[0].shape, 1)
    rolled = [pltpu.roll(t, LANES - shift, 1) for t in tiles]
    return [jnp.where(lane < LANES - shift, rolled[i], rolled[i + 1]) for i in range(len(tiles) - 1)]


def _l2norm(x):
    return x * lax.rsqrt(jnp.sum(x * x, axis=-1, keepdims=True) + EPS)


def _split3(x):
    hi = x.astype(BF16)
    r = x - hi.astype(F32)
    mid = r.astype(BF16)
    lo = (r - mid.astype(F32)).astype(BF16)
    return hi, mid, lo


def _split2(x):
    hi = x.astype(BF16)
    return hi, (x - hi.astype(F32)).astype(BF16)


DN_UNROLL = 8
DN_P_PIECES = 4
DN_P_PER_STAGE = 2


def _dn_kernel(q_ref, k_ref, v_ref, z_ref, small_ref, cwq_ref, cwk_ref, cwv_ref, alog_ref, dtb_ref, nw_ref,
               o_ref, gates_s, qs_s, k_s, v_s, beta_s, bc_s, lhs_s, add_s, ebl_s, *, seq):
    h = pl.program_id(1)
    C = CHUNK
    D = DN_D
    U = range(DN_UNROLL)
    n_groups = seq // (C * DN_UNROLL)
    row = lax.broadcasted_iota(jnp.int32, (seq, D), 0)

    @pl.when(h == 0)
    def _():
        small = small_ref[...]
        lane = lax.broadcasted_iota(jnp.int32, small.shape, 1)
        x = -jnp.exp(alog_ref[...]) * _softplus(small + dtb_ref[...])
        in_chunk = row % C
        s = 1
        while s < C:
            x = x + jnp.where(in_chunk >= s, pltpu.roll(x, s, 0), 0.0)
            s *= 2
        gates_s[...] = jnp.where(lane < S_A, jax.nn.sigmoid(small), x)

    sr = lax.broadcasted_iota(jnp.int32, (3 * LANES, 2 * LANES), 0) % LANES
    sc = lax.broadcasted_iota(jnp.int32, (3 * LANES, 2 * LANES), 1)
    sel = (sr == jnp.where(sc < LANES, h + S_B, h + S_A)).astype(BF16)
    G = C * DN_UNROLL

    def pass_p(g):
        rep = _dot(jnp.concatenate(_split3(gates_s[pl.ds(pl.multiple_of(g * G, G), G), :]), axis=1), sel)
        beta_s[pl.ds(pl.multiple_of(g * G, G), G), :] = rep[:, :LANES]
        bc_s[pl.ds(pl.multiple_of(g * G, G), G), :] = rep[:, LANES:]
        yield
        R = G // DN_P_PIECES
        for piece in range(DN_P_PIECES):
            r0 = g * G + piece * R
            rows = pl.ds(pl.multiple_of(r0, R), R)
            halo_rows = pl.ds(pl.multiple_of(jnp.maximum(r0 - HALO, 0), HALO), HALO)
            for x_ref, cw_ref, out_s, norm in ((q_ref, cwq_ref, qs_s, D ** -0.5), (k_ref, cwk_ref, k_s, 1.0),
                                               (v_ref, cwv_ref, v_s, None)):
                cw = cw_ref[...]
                halo = jnp.where(r0 == 0, 0.0, x_ref[halo_rows, :])
                x = x_ref[rows, :]
                ext = jnp.concatenate([halo, x], axis=0)
                y = x * cw[DN_CONV - 1:DN_CONV, :]
                for s in range(1, DN_CONV):
                    y = y + pltpu.roll(ext, s, 0)[HALO:, :] * cw[DN_CONV - 1 - s:DN_CONV - s, :]
                y = _silu(y)
                out_s[rows, :] = y if norm is None else _l2norm(y) * norm
                yield

    ri = lax.broadcasted_iota(jnp.int32, (C, C), 0)
    ci = lax.broadcasted_iota(jnp.int32, (C, C), 1)
    tril = ri >= ci
    strict = ri > ci
    eye = (ri == ci).astype(F32)
    nw = nw_ref[...]

    def pass_a(g):
        ns = [g * DN_UNROLL + u for u in U]
        sls = [pl.ds(pl.multiple_of(n * C, C), C) for n in ns]
        qc = [qs_s[s, :] for s in sls]
        kc = [k_s[s, :] for s in sls]
        vc = [v_s[s, :] for s in sls]
        bt = [beta_s[s, :] for s in sls]
        bc = [bc_s[s, :] for s in sls]
        kb = [kc[u] * bt[u] for u in U]
        gram = [_dot_nt(jnp.concatenate([kb[u], qc[u]], axis=0).astype(BF16), kc[u].astype(BF16)) for u in U]
        yield
        dm = [bc[u][:, :C] - jnp.transpose(bc[u])[:C, :] for u in U]
        decay = [jnp.where(tril, jnp.exp(jnp.minimum(dm[u], 0.0)), 0.0) for u in U]
        eb = [jnp.exp(bc[u]) for u in U]
        a = [jnp.where(strict, gram[u][:C] * decay[u], 0.0) for u in U]
        att = [(gram[u][C:] * decay[u]).astype(BF16) for u in U]
        p = [eye - a[u] for u in U]
        a_bf = [a[u].astype(BF16) for u in U]
        x = [_dot(a_bf[u], a_bf[u]) for u in U]
        yield
        for _ in range(4):
            px = [_dot(jnp.concatenate([p[u], x[u]], axis=0).astype(BF16), x[u].astype(BF16)) for u in U]
            p = [p[u] + px[u][:C] for u in U]
            x = [px[u][C:] for u in U]
            yield
        px = [_dot(p[u].astype(BF16), x[u].astype(BF16)) for u in U]
        p = [(p[u] + px[u]).astype(BF16) for u in U]
        yield
        rhs = [jnp.concatenate([kb[u] * eb[u], vc[u] * bt[u]], axis=1).astype(BF16) for u in U]
        wu = [_dot(p[u], rhs[u]).astype(BF16) for u in U]
        yield
        kd = [(kc[u] * jnp.exp(bc[u][C - 1:C, :] - bc[u])).astype(BF16) for u in U]
        aw = [_dot(att[u], wu[u]) for u in U]
        yield
        mn = [_dot_tn(kd[u], wu[u]) for u in U]
        for u in U:
            n = ns[u]
            lhs_s[n, 0:D, :] = mn[u][:, :D].astype(BF16)
            lhs_s[n, D:D + C, :] = (qc[u] * eb[u] - aw[u][:, :D]).astype(BF16)
            add_s[n, 0:D, :] = mn[u][:, D:]
            add_s[n, D:D + C, :] = aw[u][:, D:]
            ebl_s[n] = jnp.broadcast_to(eb[u][C - 1:C, :], (8, LANES))
        yield

    def finish(n, o):
        sl = pl.ds(pl.multiple_of(n * C, C), C)
        y = o * lax.rsqrt(jnp.mean(o * o, axis=-1, keepdims=True) + EPS) * nw * _silu(z_ref[sl, :])
        o_ref[sl, :] = y.astype(o_ref.dtype)

    def step_b(n, carry):
        S, o_prev = carry
        finish(jnp.maximum(n - 1, 0), o_prev)
        ps = _dot(lhs_s[n], S.astype(BF16))
        add = add_s[n]
        o = ps[D:] + add[D:]
        S = ebl_s[n][0:1, :] * S + (add[:D] - ps[:D])
        return S, o

    def interleave(g_a, g_b, g_p, carry):
        stages = pass_a(g_a) if g_a is not None else iter(())
        steps = [g_b * DN_UNROLL + u for u in U] if g_b is not None else []
        pieces = pass_p(g_p) if g_p is not None else iter(())
        next(stages, None)
        for _ in stages:
            if steps:
                carry = step_b(steps.pop(0), carry)
            for _ in range(DN_P_PER_STAGE):
                next(pieces, None)
        for n in steps:
            carry = step_b(n, carry)
        for _ in pieces:
            pass
        return carry

    assert n_groups >= 2
    carry = (jnp.zeros((D, D), F32), jnp.zeros((C, D), F32))
    for _ in pass_p(0):
        pass
    carry = interleave(0, None, 1, carry)
    carry = lax.fori_loop(1, n_groups - 1, lambda g, c: interleave(g, g - 1, g + 1, c), carry)
    carry = interleave(n_groups - 1, n_groups - 2, None, carry)
    _, o_last = interleave(None, n_groups - 1, None, carry)
    finish(seq // C - 1, o_last)


def _deltanet(proj, conv_w, a_log, dt_bias, norm_w, batch, seq):
    M = proj.shape[0]
    H = DN_HEADS
    n_chunks = seq // CHUNK
    assert seq % (CHUNK * DN_UNROLL) == 0
    blk = lambda c0: pl.BlockSpec((seq, DN_D), lambda b, h: (b, c0 // DN_D + h))
    cw = lambda c0: pl.BlockSpec((DN_CONV, DN_D), lambda b, h: (0, c0 // DN_D + h))
    row = pl.BlockSpec((1, LANES), lambda b, h: (0, 0))
    alog = jnp.zeros((1, LANES), F32).at[0, S_A:S_A + H].set(a_log)
    dtb = jnp.zeros((1, LANES), F32).at[0, S_A:S_A + H].set(dt_bias)
    seq_f32 = pltpu.VMEM((seq, DN_D), F32)
    return pl.pallas_call(
        functools.partial(_dn_kernel, seq=seq),
        grid=(batch, H),
        in_specs=[blk(C_DQ), blk(C_DK), blk(C_DV), blk(C_DZ),
                  pl.BlockSpec((seq, LANES), lambda b, h: (b, C_DGATE // LANES)),
                  cw(0), cw(2048), cw(4096), row, row, row],
        out_specs=pl.BlockSpec((seq, DN_D), lambda b, h: (b, h)),
        out_shape=jax.ShapeDtypeStruct((M, H * DN_D), BF16),
        scratch_shapes=[seq_f32] * 6 + [pltpu.VMEM((n_chunks, DN_D + CHUNK, DN_D), BF16),
                                        pltpu.VMEM((n_chunks, DN_D + CHUNK, DN_D), F32),
                                        pltpu.VMEM((n_chunks, 8, LANES), F32)],
        compiler_params=_cparams(("parallel", "arbitrary")),
        name="deltanet",
    )(proj, proj, proj, proj, proj, conv_w, conv_w, conv_w, alog, dtb, norm_w.reshape(1, DN_D))


GLA_LEVELS = (32, 16, 8, 4, 2, 1)
GLA_UNROLL = 4


def _gla_constants():
    C = CHUNK
    i = np.arange(C)[:, None]
    m = np.arange(C)[None, :]
    blocks = [(m <= i), (m > i)]
    upper_rows, masks = [], []
    for s in GLA_LEVELS:
        p = (i // (2 * s)) * (2 * s)
        m0 = p + s - 1
        up = (i - p) >= s
        blocks.append(np.where(up, (m > m0) & (m <= i), (m > i) & (m <= m0)))
        upper_rows.append(np.broadcast_to(up, (C, LANES)))
        j = m
        pj = (j // (2 * s)) * (2 * s)
        masks.append(up & ((j - pj) < s) & (p == pj))
    masks.append(i == m)
    mcat = np.concatenate(blocks, axis=0).astype(np.float32)
    upper = np.stack(upper_rows).astype(np.float32)
    mask = np.stack(masks).astype(np.float32)
    return mcat, upper, mask


N_QK_TILES = GLA_DK // LANES + 1
N_VR_TILES = GLA_DV // LANES + 1


def _gla_kernel(*refs, seq):
    q_t, refs = refs[:N_QK_TILES], refs[N_QK_TILES:]
    k_t, refs = refs[:N_QK_TILES], refs[N_QK_TILES:]
    v_t, refs = refs[:N_VR_TILES], refs[N_VR_TILES:]
    r_t, refs = refs[:N_VR_TILES], refs[N_VR_TILES:]
    lr_ref, wa_ref, ba_ref, nw_ref, mcat_ref, upper_ref, mask_ref, o_ref, gk_s, st_s, o_s = refs
    C = CHUNK

    def rows(tile_refs, s):
        return jnp.concatenate(_lane_shift([t[s, :] for t in tile_refs], GLA_SHIFT), axis=1)

    x_alpha = _dot(lr_ref[...], wa_ref[...], HI) + ba_ref[...]
    gk_s[...] = -_softplus(-x_alpha) / GLA_TAU
    st_s[...] = jnp.zeros_like(st_s)
    mcat = mcat_ref[...]
    nw = nw_ref[...]
    n_lvl = len(GLA_LEVELS)

    U = range(GLA_UNROLL)
    o_s[...] = jnp.zeros_like(o_s)

    def finish(g):
        for u in U:
            sl = pl.ds(pl.multiple_of((g * GLA_UNROLL + u) * C, C), C)
            o = o_s[u * C:(u + 1) * C, :]
            y = o * lax.rsqrt(jnp.mean(o * o, axis=-1, keepdims=True) + EPS) * nw * _silu(rows(r_t, sl))
            o_ref[sl, :] = y.astype(o_ref.dtype)

    def body(g, carry):
        finish(jnp.maximum(g - 1, 0))
        sls = [pl.ds(pl.multiple_of((g * GLA_UNROLL + u) * C, C), C) for u in U]
        qc = [rows(q_t, s) * (GLA_DK ** -0.5) for s in sls]
        kc = [rows(k_t, s) for s in sls]
        v_bf = [rows(v_t, s).astype(BF16) for s in sls]
        e = [_dot(mcat, jnp.concatenate(_split2(gk_s[s, :]), axis=1)) for s in sls]
        ex = [jnp.exp(e[u][:, :GLA_DK] + e[u][:, GLA_DK:]) for u in U]
        k_bf = [kc[u].astype(BF16) for u in U]
        att = [mask_ref[n_lvl] * _dot_nt(qc[u].astype(BF16), k_bf[u]) for u in U]
        for l in range(n_lvl):
            up = upper_ref[l] > 0.5
            zl = [(jnp.where(up, qc[u], kc[u]) * ex[u][(2 + l) * C:(3 + l) * C, :]).astype(BF16) for u in U]
            gl = [_dot_nt(zl[u], zl[u]) for u in U]
            att = [att[u] + mask_ref[l] * gl[u] for u in U]
        ds = [_dot_tn(v_bf[u], (kc[u] * ex[u][C:2 * C, :]).astype(BF16)) for u in U]
        ov = [_dot(att[u].astype(BF16), v_bf[u]) for u in U]
        st = st_s[...]
        for u in U:
            eb = ex[u][0:C, :]
            o_s[u * C:(u + 1) * C, :] = _dot_nt((qc[u] * eb).astype(BF16), st.astype(BF16)) + ov[u]
            st = st * eb[C - 1:C, :] + ds[u]
        st_s[...] = st
        return carry

    n_groups = seq // (C * GLA_UNROLL)
    lax.fori_loop(0, n_groups, body, 0)
    finish(n_groups - 1)


def _gla(proj, w_alpha2, b_alpha, norm_w, batch, seq):
    M = proj.shape[0]
    H = GLA_HEADS
    assert seq % (CHUNK * GLA_UNROLL) == 0
    mcat, upper, mask = _gla_constants()
    wa = jnp.zeros((LANES, H * GLA_DK), F32).at[S_LR:S_LR + GLA_RANK, :].set(w_alpha2)
    full = lambda a: pl.BlockSpec(a.shape, lambda b, h: (0,) * a.ndim)

    def tiles(c0, width, n):
        return [pl.BlockSpec((seq, LANES), lambda b, h, t=t: (b, c0 // LANES + h * (width // LANES) + t))
                for t in range(n)]

    n_in = 2 * N_QK_TILES + 2 * N_VR_TILES
    return pl.pallas_call(
        functools.partial(_gla_kernel, seq=seq),
        grid=(batch, H),
        in_specs=[*tiles(C_GQ, GLA_DK, N_QK_TILES), *tiles(C_GK, GLA_DK, N_QK_TILES),
                  *tiles(C_GV, GLA_DV, N_VR_TILES), *tiles(C_GR, GLA_DV, N_VR_TILES),
                  pl.BlockSpec((seq, LANES), lambda b, h: (b, C_LR // LANES)),
                  pl.BlockSpec((LANES, GLA_DK), lambda b, h: (0, h)),
                  pl.BlockSpec((1, GLA_DK), lambda b, h: (0, h)),
                  pl.BlockSpec((1, GLA_DV), lambda b, h: (0, 0)),
                  full(mcat), full(upper), full(mask)],
        out_specs=pl.BlockSpec((seq, GLA_DV), lambda b, h: (b, h)),
        out_shape=jax.ShapeDtypeStruct((M, H * GLA_DV), BF16),
        scratch_shapes=[pltpu.VMEM((seq, GLA_DK), F32), pltpu.VMEM((GLA_DV, GLA_DK), F32),
                        pltpu.VMEM((GLA_UNROLL * CHUNK, GLA_DV), F32)],
        compiler_params=_cparams(("parallel", "parallel")),
        name="gla",
    )(*([proj] * (n_in + 1)), wa, b_alpha.reshape(1, -1), norm_w.reshape(1, GLA_DV),
      jnp.asarray(mcat, BF16), jnp.asarray(upper), jnp.asarray(mask))


MERGE_ROW_BLOCKS = 2


def _merge_kernel(odn_ref, ogla_ref, pa_ref, pb_ref, gd_ref, gdn_ref, gg_ref, ggn_ref, o_ref):
    def gate(main_ref, next_ref, rows):
        n = main_ref.shape[1] // LANES
        tiles = [main_ref[rows, t * LANES:(t + 1) * LANES] for t in range(n)] + [next_ref[rows, :]]
        return jax.nn.sigmoid(jnp.concatenate(_lane_shift(tiles, GATE_SHIFT), axis=1))

    pa = pa_ref[...].astype(BF16)
    pb = None
    rb = o_ref.shape[0] // MERGE_ROW_BLOCKS
    for r in range(MERGE_ROW_BLOCKS):
        rows = slice(r * rb, (r + 1) * rb)
        ya = _dot(odn_ref[rows, :], pa)
        if pb is None:
            pb = pb_ref[...].astype(BF16)
        yb = _dot(ogla_ref[rows, :], pb)
        o_ref[rows, :] = (gate(gd_ref, gdn_ref, rows) * ya + gate(gg_ref, ggn_ref, rows) * yb).astype(o_ref.dtype)


def _merge(o_dn, o_gla, pa, pb, proj, tm=2048, tn=256):
    M, K = o_dn.shape
    N = pa.shape[1]
    tm = min(tm, M)
    lt = tn // LANES
    main = lambda c0: pl.BlockSpec((tm, tn), lambda i, j: (i, c0 // tn + j))
    nxt = lambda c0: pl.BlockSpec((tm, LANES), lambda i, j: (i, c0 // LANES + (j + 1) * lt))
    assert (C_GATE_DN - GATE_SHIFT) % tn == 0 and (C_GATE_GLA - GATE_SHIFT) % tn == 0
    assert tm % (MERGE_ROW_BLOCKS * 16) == 0
    return pl.pallas_call(
        _merge_kernel,
        grid=(M // tm, N // tn),
        in_specs=[pl.BlockSpec((tm, K), lambda i, j: (i, 0), pipeline_mode=pl.Buffered(1)),
                  pl.BlockSpec((tm, K), lambda i, j: (i, 0), pipeline_mode=pl.Buffered(1)),
                  pl.BlockSpec((K, tn), lambda i, j: (0, j)),
                  pl.BlockSpec((K, tn), lambda i, j: (0, j)),
                  main(C_GATE_DN), nxt(C_GATE_DN), main(C_GATE_GLA), nxt(C_GATE_GLA)],
        out_specs=pl.BlockSpec((tm, tn), lambda i, j: (i, j)),
        out_shape=jax.ShapeDtypeStruct((M, N), BF16),
        compiler_params=_cparams(("parallel", "arbitrary")),
        name="merge",
    )(o_dn, o_gla, pa, pb, proj, proj, proj, proj)


def _mm_res_kernel(a_ref, b_ref, r_ref, o_ref):
    o_ref[...] = r_ref[...] + _dot(a_ref[...], b_ref[...].astype(BF16))


def _matmul_residual(a, b, res, tm=1024, tn=512, name="outproj"):
    M, K = a.shape
    N = b.shape[1]
    tm = min(tm, M)
    return pl.pallas_call(
        _mm_res_kernel,
        grid=(M // tm, N // tn),
        in_specs=[pl.BlockSpec((tm, K), lambda i, j: (i, 0)),
                  pl.BlockSpec((K, tn), lambda i, j: (0, j)),
                  pl.BlockSpec((tm, tn), lambda i, j: (i, j))],
        out_specs=pl.BlockSpec((tm, tn), lambda i, j: (i, j)),
        out_shape=jax.ShapeDtypeStruct((M, N), F32),
        compiler_params=_cparams(("parallel", "arbitrary")),
        name=name,
    )(a, b, res)


FFN_ROW_BLOCKS = 4


def _ffn_in_kernel(h_ref, wg_ref, wu_ref, cw_ref, cb_ref, w2_ref, o_ref, w2_bf_ref):
    w2_bf_ref[...] = w2_ref[...].astype(BF16)
    wg = wg_ref[...].astype(BF16)
    wu = None
    cw = cw_ref[...]
    cb = cb_ref[...]
    rb = h_ref.shape[0] // FFN_ROW_BLOCKS
    tail = jnp.zeros((HALO, wg.shape[1]), F32)
    for r in range(FFN_ROW_BLOCKS):
        h = h_ref[r * rb:(r + 1) * rb, :]
        gate = _dot(h, wg)
        if wu is None:
            wu = wu_ref[...].astype(BF16)
        up = _dot(h, wu)
        ext = jnp.concatenate([tail, gate], axis=0)
        y = gate * cw[FFN_CONV - 1:FFN_CONV, :] + cb
        for s in range(1, FFN_CONV):
            y = y + pltpu.roll(ext, s, 0)[HALO:, :] * cw[FFN_CONV - 1 - s:FFN_CONV - s, :]
        o_ref[r * rb:(r + 1) * rb, :] = (_silu(y) * up).astype(o_ref.dtype)
        tail = gate[rb - HALO:, :]


def _ffn_in(h, w_in, conv_w, conv_b, w_out, seq, tn=256):
    M, K = h.shape
    N = w_in.shape[1] // 2
    nb = N // tn
    n_steps = (M // seq) * nb
    w2_rows = w_out.shape[0] // n_steps
    assert seq % (FFN_ROW_BLOCKS * HALO) == 0
    assert w2_rows * n_steps == w_out.shape[0] and w2_rows % 16 == 0
    w2_spec = pl.BlockSpec((w2_rows, w_out.shape[1]), lambda i, j: (i * nb + j, 0))
    return pl.pallas_call(
        _ffn_in_kernel,
        grid=(M // seq, nb),
        in_specs=[pl.BlockSpec((seq, K), lambda i, j: (i, 0), pipeline_mode=pl.Buffered(1)),
                  pl.BlockSpec((K, tn), lambda i, j: (0, j)),
                  pl.BlockSpec((K, tn), lambda i, j: (0, nb + j)),
                  pl.BlockSpec((FFN_CONV, tn), lambda i, j: (0, j)),
                  pl.BlockSpec((1, tn), lambda i, j: (0, j)),
                  w2_spec],
        out_specs=[pl.BlockSpec((seq, tn), lambda i, j: (i, j)), w2_spec],
        out_shape=[jax.ShapeDtypeStruct((M, N), BF16), jax.ShapeDtypeStruct(w_out.shape, BF16)],
        compiler_params=_cparams(("parallel", "arbitrary")),
        name="ffn_in",
    )(h, w_in, w_in, conv_w, conv_b.reshape(1, N), w_out)


def _ffn_out_kernel(a_ref, b_ref, r_ref, o_ref, acc_ref):
    kk = pl.program_id(2)

    @pl.when(kk == 0)
    def _():
        acc_ref[...] = r_ref[...]

    acc_ref[...] += _dot(a_ref[...], b_ref[...])

    @pl.when(kk == pl.num_programs(2) - 1)
    def _():
        o_ref[...] = acc_ref[...]


def _ffn_out(a, b, res, tm=512, tn=512, ksplit=1):
    M, K = a.shape
    N = b.shape[1]
    tm = min(tm, M)
    tk = K // ksplit
    return pl.pallas_call(
        _ffn_out_kernel,
        grid=(M // tm, N // tn, ksplit),
        in_specs=[pl.BlockSpec((tm, tk), lambda i, j, k: (i, k)),
                  pl.BlockSpec((tk, tn), lambda i, j, k: (k, j)),
                  pl.BlockSpec((tm, tn), lambda i, j, k: (i, j))],
        out_specs=pl.BlockSpec((tm, tn), lambda i, j, k: (i, j)),
        out_shape=jax.ShapeDtypeStruct((M, N), F32),
        scratch_shapes=[pltpu.VMEM((tm, tn), F32)],
        compiler_params=_cparams(("parallel", "arbitrary", "arbitrary")),
        name="ffn_out",
    )(a, b, res)


def _layer(x2d, batch, seq, norm_mix_w, w_in, dn_conv_w, dn_a_log, dn_dt_bias, dn_norm_w,
           gla_w_alpha2, gla_b_alpha, gla_norm_w, w_branch_dn, w_branch_gla, w_out,
           norm_ffn_w, w_ffn_in, ffn_conv_w, ffn_conv_b, w_ffn_out):
    h = _rmsnorm(x2d, norm_mix_w, BF16)
    proj = _inproj(h, w_in.T)
    o_dn = _deltanet(proj, dn_conv_w, dn_a_log, dn_dt_bias, dn_norm_w, batch, seq)
    o_gla = _gla(proj, gla_w_alpha2, gla_b_alpha, gla_norm_w, batch, seq)
    merged = _merge(o_dn, o_gla, w_branch_dn, w_branch_gla, proj)
    x1 = _matmul_residual(merged, w_out, x2d)

    h2 = _rmsnorm(x1, norm_ffn_w, BF16)
    act, w_ffn_out_bf = _ffn_in(h2, w_ffn_in, ffn_conv_w, ffn_conv_b, w_ffn_out, seq)
    return _ffn_out(act, w_ffn_out_bf, x1)


def kernel(x, norm_mix_w, w_in, dn_conv_w, dn_a_log, dn_dt_bias, dn_norm_w, gla_w_alpha2, gla_b_alpha, gla_norm_w, w_branch_dn, w_branch_gla, w_out, norm_ffn_w, w_ffn_in, ffn_conv_w, ffn_conv_b, w_ffn_out, norm_final_w):
    batch, seq, D = x.shape
    x2d = x.reshape(batch * seq, D)
    for l in range(norm_mix_w.shape[0]):
        x2d = _layer(x2d, batch, seq, norm_mix_w[l], w_in[l], dn_conv_w[l], dn_a_log[l], dn_dt_bias[l],
                     dn_norm_w[l], gla_w_alpha2[l], gla_b_alpha[l], gla_norm_w[l], w_branch_dn[l],
                     w_branch_gla[l], w_out[l], norm_ffn_w[l], w_ffn_in[l], ffn_conv_w[l], ffn_conv_b[l],
                     w_ffn_out[l])
    return _rmsnorm(x2d, norm_final_w, F32).reshape(batch, seq, D)
```

```python
import functools

import numpy as np
import jax
import jax.numpy as jnp
from jax import lax
from jax.experimental import pallas as pl
from jax.experimental.pallas import tpu as pltpu

F32 = jnp.float32
BF16 = jnp.bfloat16
HI = lax.Precision.HIGHEST

EPS = 1e-6
CHUNK = 64
DN_HEADS = 16
DN_D = 128
DN_CONV = 4
GLA_HEADS = 8
GLA_DK = 128
GLA_DV = 256
GLA_RANK = 16
GLA_TAU = 16.0
FFN_CONV = 3
LANES = 128
HALO = 8

C_DQ, C_DK, C_DV, C_DZ = 0, 2048, 4096, 6144
C_DGATE = 8192
C_GQ, C_GK, C_GV, C_GR = 8224, 9248, 10272, 12320
C_LR = 14368
C_GATE_DN, C_GATE_GLA = 14384, 18480
S_B, S_A = 0, 16
S_LR = C_LR % LANES
GLA_SHIFT = C_GQ % LANES
GATE_SHIFT = C_GATE_DN % LANES
assert C_GK % LANES == C_GV % LANES == C_GR % LANES == GLA_SHIFT and C_GATE_GLA % LANES == GATE_SHIFT

VMEM_LIMIT = 56 * 1024 * 1024


def _cparams(sem):
    return pltpu.CompilerParams(dimension_semantics=sem, vmem_limit_bytes=VMEM_LIMIT)


def _silu(x):
    return x * jax.nn.sigmoid(x)


def _softplus(x):
    return jnp.maximum(x, 0.0) + jnp.log1p(jnp.exp(-jnp.abs(x)))


def _dot(a, b, precision=None):
    return jnp.dot(a, b, precision=precision, preferred_element_type=F32)


def _dot_nt(a, b, precision=None):
    return lax.dot_general(a, b, (((1,), (1,)), ((), ())), precision=precision, preferred_element_type=F32)


def _dot_tn(a, b, precision=None):
    return lax.dot_general(a, b, (((0,), (0,)), ((), ())), precision=precision, preferred_element_type=F32)


def _rmsnorm_kernel(x_ref, w_ref, o_ref):
    x = x_ref[...]
    ms = jnp.mean(x * x, axis=-1, keepdims=True)
    o_ref[...] = (x * lax.rsqrt(ms + EPS) * w_ref[...]).astype(o_ref.dtype)


def _rmsnorm(x, w, out_dtype, tm=512):
    M, D = x.shape
    tm = min(tm, M)
    return pl.pallas_call(
        _rmsnorm_kernel,
        grid=(M // tm,),
        in_specs=[pl.BlockSpec((tm, D), lambda i: (i, 0)), pl.BlockSpec((1, D), lambda i: (0, 0))],
        out_specs=pl.BlockSpec((tm, D), lambda i: (i, 0)),
        out_shape=jax.ShapeDtypeStruct((M, D), out_dtype),
        compiler_params=_cparams(("parallel",)),
        name="rmsnorm",
    )(x, w.reshape(1, D))


def _inproj_kernel(a_ref, wt_ref, o_ref):
    hn = wt_ref.shape[0] // 2
    for c in (slice(0, hn), slice(hn, 2 * hn)):
        o_ref[:, c] = _dot_nt(a_ref[...], wt_ref[c, :].astype(BF16))


def _inproj(a, wt, tm=2048, tn=512):
    M, K = a.shape
    N = wt.shape[0]
    tm = min(tm, M)
    return pl.pallas_call(
        _inproj_kernel,
        grid=(M // tm, pl.cdiv(N, tn)),
        in_specs=[pl.BlockSpec((tm, K), lambda i, j: (i, 0), pipeline_mode=pl.Buffered(1)),
                  pl.BlockSpec((tn, K), lambda i, j: (j, 0))],
        out_specs=pl.BlockSpec((tm, tn), lambda i, j: (i, j)),
        out_shape=jax.ShapeDtypeStruct((M, N), F32),
        compiler_params=_cparams(("parallel", "arbitrary")),
        name="inproj",
    )(a, wt)


def _lane_shift(tiles, shift):
    lane = lax.broadcasted_iota(jnp.int32, tiles[0].shape, 1)
    rolled = [pltpu.roll(t, LANES - shift, 1) for t in tiles]
    return [jnp.where(lane < LANES - shift, rolled[i], rolled[i + 1]) for i in range(len(tiles) - 1)]


def _l2norm(x):
    return x * lax.rsqrt(jnp.sum(x * x, axis=-1, keepdims=True) + EPS)


def _split3(x):
    hi = x.astype(BF16)
    r = x - hi.astype(F32)
    mid = r.astype(BF16)
    lo = (r - mid.astype(F32)).astype(BF16)
    return hi, mid, lo


def _split2(x):
    hi = x.astype(BF16)
    return hi, (x - hi.astype(F32)).astype(BF16)


DN_UNROLL = 8
DN_P_PIECES = 4
DN_P_PER_STAGE = 2


def _dn_kernel(q_ref, k_ref, v_ref, z_ref, small_ref, cwq_ref, cwk_ref, cwv_ref, alog_ref, dtb_ref, nw_ref,
               o_ref, gates_s, qs_s, k_s, v_s, beta_s, bc_s, lhs_s, add_s, ebl_s, *, seq):
    h = pl.program_id(1)
    C = CHUNK
    D = DN_D
    U = range(DN_UNROLL)
    n_groups = seq // (C * DN_UNROLL)
    row = lax.broadcasted_iota(jnp.int32, (seq, D), 0)

    @pl.when(h == 0)
    def _():
        small = small_ref[...]
        lane = lax.broadcasted_iota(jnp.int32, small.shape, 1)
        x = -jnp.exp(alog_ref[...]) * _softplus(small + dtb_ref[...])
        in_chunk = row % C
        s = 1
        while s < C:
            x = x + jnp.where(in_chunk >= s, pltpu.roll(x, s, 0), 0.0)
            s *= 2
        gates_s[...] = jnp.where(lane < S_A, jax.nn.sigmoid(small), x)

    sr = lax.broadcasted_iota(jnp.int32, (3 * LANES, 2 * LANES), 0) % LANES
    sc = lax.broadcasted_iota(jnp.int32, (3 * LANES, 2 * LANES), 1)
    sel = (sr == jnp.where(sc < LANES, h + S_B, h + S_A)).astype(BF16)
    G = C * DN_UNROLL

    def pass_p(g):
        rep = _dot(jnp.concatenate(_split3(gates_s[pl.ds(pl.multiple_of(g * G, G), G), :]), axis=1), sel)
        beta_s[pl.ds(pl.multiple_of(g * G, G), G), :] = rep[:, :LANES]
        bc_s[pl.ds(pl.multiple_of(g * G, G), G), :] = rep[:, LANES:]
        yield
        R = G // DN_P_PIECES
        for piece in range(DN_P_PIECES):
            r0 = g * G + piece * R
            rows = pl.ds(pl.multiple_of(r0, R), R)
            halo_rows = pl.ds(pl.multiple_of(jnp.maximum(r0 - HALO, 0), HALO), HALO)
            for x_ref, cw_ref, out_s, norm in ((q_ref, cwq_ref, qs_s, D ** -0.5), (k_ref, cwk_ref, k_s, 1.0),
                                               (v_ref, cwv_ref, v_s, None)):
                cw = cw_ref[...]
                halo = jnp.where(r0 == 0, 0.0, x_ref[halo_rows, :])
                x = x_ref[rows, :]
                ext = jnp.concatenate([halo, x], axis=0)
                y = x * cw[DN_CONV - 1:DN_CONV, :]
                for s in range(1, DN_CONV):
                    y = y + pltpu.roll(ext, s, 0)[HALO:, :] * cw[DN_CONV - 1 - s:DN_CONV - s, :]
                y = _silu(y)
                out_s[rows, :] = y if norm is None else _l2norm(y) * norm
                yield

    ri = lax.broadcasted_iota(jnp.int32, (C, C), 0)
    ci = lax.broadcasted_iota(jnp.int32, (C, C), 1)
    tril = ri >= ci
    strict = ri > ci
    eye = (ri == ci).astype(F32)
    nw = nw_ref[...]

    def pass_a(g):
        ns = [g * DN_UNROLL + u for u in U]
        sls = [pl.ds(pl.multiple_of(n * C, C), C) for n in ns]
        qc = [qs_s[s, :] for s in sls]
        kc = [k_s[s, :] for s in sls]
        vc = [v_s[s, :] for s in sls]
        bt = [beta_s[s, :] for s in sls]
        bc = [bc_s[s, :] for s in sls]
        kb = [kc[u] * bt[u] for u in U]
        gram = [_dot_nt(jnp.concatenate([kb[u], qc[u]], axis=0).astype(BF16), kc[u].astype(BF16)) for u in U]
        yield
        dm = [bc[u][:, :C] - jnp.transpose(bc[u])[:C, :] for u in U]
        decay = [jnp.where(tril, jnp.exp(jnp.minimum(dm[u], 0.0)), 0.0) for u in U]
        eb = [jnp.exp(bc[u]) for u in U]
        a = [jnp.where(strict, gram[u][:C] * decay[u], 0.0) for u in U]
        att = [(gram[u][C:] * decay[u]).astype(BF16) for u in U]
        p = [eye - a[u] for u in U]
        a_bf = [a[u].astype(BF16) for u in U]
        x = [_dot(a_bf[u], a_bf[u]) for u in U]
        yield
        for _ in range(4):
            px = [_dot(jnp.concatenate([p[u], x[u]], axis=0).astype(BF16), x[u].astype(BF16)) for u in U]
            p = [p[u] + px[u][:C] for u in U]
            x = [px[u][C:] for u in U]
            yield
        px = [_dot(p[u].astype(BF16), x[u].astype(BF16)) for u in U]
        p = [(p[u] + px[u]).astype(BF16) for u in U]
        yield
        rhs = [jnp.concatenate([kb[u] * eb[u], vc[u] * bt[u]], axis=1).astype(BF16) for u in U]
        wu = [_dot(p[u], rhs[u]).astype(BF16) for u in U]
        yield
        kd = [(kc[u] * jnp.exp(bc[u][C - 1:C, :] - bc[u])).astype(BF16) for u in U]
        aw = [_dot(att[u], wu[u]) for u in U]
        yield
        mn = [_dot_tn(kd[u], wu[u]) for u in U]
        for u in U:
            n = ns[u]
            lhs_s[n, 0:D, :] = mn[u][:, :D].astype(BF16)
            lhs_s[n, D:D + C, :] = (qc[u] * eb[u] - aw[u][:, :D]).astype(BF16)
            add_s[n, 0:D, :] = mn[u][:, D:]
            add_s[n, D:D + C, :] = aw[u][:, D:]
            ebl_s[n] = jnp.broadcast_to(eb[u][C - 1:C, :], (8, LANES))
        yield

    def finish(n, o):
        sl = pl.ds(pl.multiple_of(n * C, C), C)
        y = o * lax.rsqrt(jnp.mean(o * o, axis=-1, keepdims=True) + EPS) * nw * _silu(z_ref[sl, :])
        o_ref[sl, :] = y.astype(o_ref.dtype)

    def step_b(n, carry):
        S, o_prev = carry
        finish(jnp.maximum(n - 1, 0), o_prev)
        ps = _dot(lhs_s[n], S.astype(BF16))
        add = add_s[n]
        o = ps[D:] + add[D:]
        S = ebl_s[n][0:1, :] * S + (add[:D] - ps[:D])
        return S, o

    def interleave(g_a, g_b, g_p, carry):
        stages = pass_a(g_a) if g_a is not None else iter(())
        steps = [g_b * DN_UNROLL + u for u in U] if g_b is not None else []
        pieces = pass_p(g_p) if g_p is not None else iter(())
        next(stages, None)
        for _ in stages:
            if steps:
                carry = step_b(steps.pop(0), carry)
            for _ in range(DN_P_PER_STAGE):
                next(pieces, None)
        for n in steps:
            carry = step_b(n, carry)
        for _ in pieces:
            pass
        return carry

    assert n_groups >= 2
    carry = (jnp.zeros((D, D), F32), jnp.zeros((C, D), F32))
    for _ in pass_p(0):
        pass
    carry = interleave(0, None, 1, carry)
    carry = lax.fori_loop(1, n_groups - 1, lambda g, c: interleave(g, g - 1, g + 1, c), carry)
    carry = interleave(n_groups - 1, n_groups - 2, None, carry)
    _, o_last = interleave(None, n_groups - 1, None, carry)
    finish(seq // C - 1, o_last)


def _deltanet(proj, conv_w, a_log, dt_bias, norm_w, batch, seq):
    M = proj.shape[0]
    H = DN_HEADS
    n_chunks = seq // CHUNK
    assert seq % (CHUNK * DN_UNROLL) == 0
    blk = lambda c0: pl.BlockSpec((seq, DN_D), lambda b, h: (b, c0 // DN_D + h))
    cw = lambda c0: pl.BlockSpec((DN_CONV, DN_D), lambda b, h: (0, c0 // DN_D + h))
    row = pl.BlockSpec((1, LANES), lambda b, h: (0, 0))
    alog = jnp.zeros((1, LANES), F32).at[0, S_A:S_A + H].set(a_log)
    dtb = jnp.zeros((1, LANES), F32).at[0, S_A:S_A + H].set(dt_bias)
    seq_f32 = pltpu.VMEM((seq, DN_D), F32)
    return pl.pallas_call(
        functools.partial(_dn_kernel, seq=seq),
        grid=(batch, H),
        in_specs=[blk(C_DQ), blk(C_DK), blk(C_DV), blk(C_DZ),
                  pl.BlockSpec((seq, LANES), lambda b, h: (b, C_DGATE // LANES)),
                  cw(0), cw(2048), cw(4096), row, row, row],
        out_specs=pl.BlockSpec((seq, DN_D), lambda b, h: (b, h)),
        out_shape=jax.ShapeDtypeStruct((M, H * DN_D), BF16),
        scratch_shapes=[seq_f32] * 6 + [pltpu.VMEM((n_chunks, DN_D + CHUNK, DN_D), BF16),
                                        pltpu.VMEM((n_chunks, DN_D + CHUNK, DN_D), F32),
                                        pltpu.VMEM((n_chunks, 8, LANES), F32)],
        compiler_params=_cparams(("parallel", "arbitrary")),
        name="deltanet",
    )(proj, proj, proj, proj, proj, conv_w, conv_w, conv_w, alog, dtb, norm_w.reshape(1, DN_D))


GLA_LEVELS = (32, 16, 8, 4, 2, 1)
GLA_UNROLL = 4


def _gla_constants():
    C = CHUNK
    i = np.arange(C)[:, None]
    m = np.arange(C)[None, :]
    blocks = [(m <= i), (m > i)]
    upper_rows, masks = [], []
    for s in GLA_LEVELS:
        p = (i // (2 * s)) * (2 * s)
        m0 = p + s - 1
        up = (i - p) >= s
        blocks.append(np.where(up, (m > m0) & (m <= i), (m > i) & (m <= m0)))
        upper_rows.append(np.broadcast_to(up, (C, LANES)))
        j = m
        pj = (j // (2 * s)) * (2 * s)
        masks.append(up & ((j - pj) < s) & (p == pj))
    masks.append(i == m)
    mcat = np.concatenate(blocks, axis=0).astype(np.float32)
    upper = np.stack(upper_rows).astype(np.float32)
    mask = np.stack(masks).astype(np.float32)
    return mcat, upper, mask


N_QK_TILES = GLA_DK // LANES + 1
N_VR_TILES = GLA_DV // LANES + 1


def _gla_kernel(*refs, seq):
    q_t, refs = refs[:N_QK_TILES], refs[N_QK_TILES:]
    k_t, refs = refs[:N_QK_TILES], refs[N_QK_TILES:]
    v_t, refs = refs[:N_VR_TILES], refs[N_VR_TILES:]
    r_t, refs = refs[:N_VR_TILES], refs[N_VR_TILES:]
    lr_ref, wa_ref, ba_ref, nw_ref, mcat_ref, upper_ref, mask_ref, o_ref, gk_s, st_s, o_s = refs
    C = CHUNK

    def rows(tile_refs, s):
        return jnp.concatenate(_lane_shift([t[s, :] for t in tile_refs], GLA_SHIFT), axis=1)

    x_alpha = _dot(lr_ref[...], wa_ref[...], HI) + ba_ref[...]
    gk_s[...] = -_softplus(-x_alpha) / GLA_TAU
    st_s[...] = jnp.zeros_like(st_s)
    mcat = mcat_ref[...]
    nw = nw_ref[...]
    n_lvl = len(GLA_LEVELS)

    U = range(GLA_UNROLL)
    o_s[...] = jnp.zeros_like(o_s)

    def finish(g):
        for u in U:
            sl = pl.ds(pl.multiple_of((g * GLA_UNROLL + u) * C, C), C)
            o = o_s[u * C:(u + 1) * C, :]
            y = o * lax.rsqrt(jnp.mean(o * o, axis=-1, keepdims=True) + EPS) * nw * _silu(rows(r_t, sl))
            o_ref[sl, :] = y.astype(o_ref.dtype)

    def body(g, carry):
        finish(jnp.maximum(g - 1, 0))
        sls = [pl.ds(pl.multiple_of((g * GLA_UNROLL + u) * C, C), C) for u in U]
        qc = [rows(q_t, s) * (GLA_DK ** -0.5) for s in sls]
        kc = [rows(k_t, s) for s in sls]
        v_bf = [rows(v_t, s).astype(BF16) for s in sls]
        e = [_dot(mcat, jnp.concatenate(_split2(gk_s[s, :]), axis=1)) for s in sls]
        ex = [jnp.exp(e[u][:, :GLA_DK] + e[u][:, GLA_DK:]) for u in U]
        k_bf = [kc[u].astype(BF16) for u in U]
        att = [mask_ref[n_lvl] * _dot_nt(qc[u].astype(BF16), k_bf[u]) for u in U]
        for l in range(n_lvl):
            up = upper_ref[l] > 0.5
            zl = [(jnp.where(up, qc[u], kc[u]) * ex[u][(2 + l) * C:(3 + l) * C, :]).astype(BF16) for u in U]
            gl = [_dot_nt(zl[u], zl[u]) for u in U]
            att = [att[u] + mask_ref[l] * gl[u] for u in U]
        ds = [_dot_tn(v_bf[u], (kc[u] * ex[u][C:2 * C, :]).astype(BF16)) for u in U]
        ov = [_dot(att[u].astype(BF16), v_bf[u]) for u in U]
        st = st_s[...]
        for u in U:
            eb = ex[u][0:C, :]
            o_s[u * C:(u + 1) * C, :] = _dot_nt((qc[u] * eb).astype(BF16), st.astype(BF16)) + ov[u]
            st = st * eb[C - 1:C, :] + ds[u]
        st_s[...] = st
        return carry

    n_groups = seq // (C * GLA_UNROLL)
    lax.fori_loop(0, n_groups, body, 0)
    finish(n_groups - 1)


def _gla(proj, w_alpha2, b_alpha, norm_w, batch, seq):
    M = proj.shape[0]
    H = GLA_HEADS
    assert seq % (CHUNK * GLA_UNROLL) == 0
    mcat, upper, mask = _gla_constants()
    wa = jnp.zeros((LANES, H * GLA_DK), F32).at[S_LR:S_LR + GLA_RANK, :].set(w_alpha2)
    full = lambda a: pl.BlockSpec(a.shape, lambda b, h: (0,) * a.ndim)

    def tiles(c0, width, n):
        return [pl.BlockSpec((seq, LANES), lambda b, h, t=t: (b, c0 // LANES + h * (width // LANES) + t))
                for t in range(n)]

    n_in = 2 * N_QK_TILES + 2 * N_VR_TILES
    return pl.pallas_call(
        functools.partial(_gla_kernel, seq=seq),
        grid=(batch, H),
        in_specs=[*tiles(C_GQ, GLA_DK, N_QK_TILES), *tiles(C_GK, GLA_DK, N_QK_TILES),
                  *tiles(C_GV, GLA_DV, N_VR_TILES), *tiles(C_GR, GLA_DV, N_VR_TILES),
                  pl.BlockSpec((seq, LANES), lambda b, h: (b, C_LR // LANES)),
                  pl.BlockSpec((LANES, GLA_DK), lambda b, h: (0, h)),
                  pl.BlockSpec((1, GLA_DK), lambda b, h: (0, h)),
                  pl.BlockSpec((1, GLA_DV), lambda b, h: (0, 0)),
                  full(mcat), full(upper), full(mask)],
        out_specs=pl.BlockSpec((seq, GLA_DV), lambda b, h: (b, h)),
        out_shape=jax.ShapeDtypeStruct((M, H * GLA_DV), BF16),
        scratch_shapes=[pltpu.VMEM((seq, GLA_DK), F32), pltpu.VMEM((GLA_DV, GLA_DK), F32),
                        pltpu.VMEM((GLA_UNROLL * CHUNK, GLA_DV), F32)],
        compiler_params=_cparams(("parallel", "parallel")),
        name="gla",
    )(*([proj] * (n_in + 1)), wa, b_alpha.reshape(1, -1), norm_w.reshape(1, GLA_DV),
      jnp.asarray(mcat, BF16), jnp.asarray(upper), jnp.asarray(mask))


MERGE_ROW_BLOCKS = 2


def _merge_kernel(odn_ref, ogla_ref, pa_ref, pb_ref, gd_ref, gdn_ref, gg_ref, ggn_ref, o_ref):
    def gate(main_ref, next_ref, rows):
        n = main_ref.shape[1] // LANES
        tiles = [main_ref[rows, t * LANES:(t + 1) * LANES] for t in range(n)] + [next_ref[rows, :]]
        return jax.nn.sigmoid(jnp.concatenate(_lane_shift(tiles, GATE_SHIFT), axis=1))

    pa = pa_ref[...].astype(BF16)
    pb = None
    rb = o_ref.shape[0] // MERGE_ROW_BLOCKS
    for r in range(MERGE_ROW_BLOCKS):
        rows = slice(r * rb, (r + 1) * rb)
        ya = _dot(odn_ref[rows, :], pa)
        if pb is None:
            pb = pb_ref[...].astype(BF16)
        yb = _dot(ogla_ref[rows, :], pb)
        o_ref[rows, :] = (gate(gd_ref, gdn_ref, rows) * ya + gate(gg_ref, ggn_ref, rows) * yb).astype(o_ref.dtype)


def _merge(o_dn, o_gla, pa, pb, proj, tm=2048, tn=256):
    M, K = o_dn.shape
    N = pa.shape[1]
    tm = min(tm, M)
    lt = tn // LANES
    main = lambda c0: pl.BlockSpec((tm, tn), lambda i, j: (i, c0 // tn + j))
    nxt = lambda c0: pl.BlockSpec((tm, LANES), lambda i, j: (i, c0 // LANES + (j + 1) * lt))
    assert (C_GATE_DN - GATE_SHIFT) % tn == 0 and (C_GATE_GLA - GATE_SHIFT) % tn == 0
    assert tm % (MERGE_ROW_BLOCKS * 16) == 0
    return pl.pallas_call(
        _merge_kernel,
        grid=(M // tm, N // tn),
        in_specs=[pl.BlockSpec((tm, K), lambda i, j: (i, 0), pipeline_mode=pl.Buffered(1)),
                  pl.BlockSpec((tm, K), lambda i, j: (i, 0), pipeline_mode=pl.Buffered(1)),
                  pl.BlockSpec((K, tn), lambda i, j: (0, j)),
                  pl.BlockSpec((K, tn), lambda i, j: (0, j)),
                  main(C_GATE_DN), nxt(C_GATE_DN), main(C_GATE_GLA), nxt(C_GATE_GLA)],
        out_specs=pl.BlockSpec((tm, tn), lambda i, j: (i, j)),
        out_shape=jax.ShapeDtypeStruct((M, N), BF16),
        compiler_params=_cparams(("parallel", "arbitrary")),
        name="merge",
    )(o_dn, o_gla, pa, pb, proj, proj, proj, proj)


def _mm_res_kernel(a_ref, b_ref, r_ref, o_ref):
    o_ref[...] = r_ref[...] + _dot(a_ref[...], b_ref[...].astype(BF16))


def _matmul_residual(a, b, res, tm=1024, tn=512, name="outproj"):
    M, K = a.shape
    N = b.shape[1]
    tm = min(tm, M)
    return pl.pallas_call(
        _mm_res_kernel,
        grid=(M // tm, N // tn),
        in_specs=[pl.BlockSpec((tm, K), lambda i, j: (i, 0)),
                  pl.BlockSpec((K, tn), lambda i, j: (0, j)),
                  pl.BlockSpec((tm, tn), lambda i, j: (i, j))],
        out_specs=pl.BlockSpec((tm, tn), lambda i, j: (i, j)),
        out_shape=jax.ShapeDtypeStruct((M, N), F32),
        compiler_params=_cparams(("parallel", "arbitrary")),
        name=name,
    )(a, b, res)


FFN_ROW_BLOCKS = 2


def _ffn_in_kernel(h_ref, wg_ref, wu_ref, cw_ref, cb_ref, w2_ref, o_ref, w2_bf_ref):
    w2_bf_ref[...] = w2_ref[...].astype(BF16)
    wg = wg_ref[...].astype(BF16)
    wu = None
    cw = cw_ref[...]
    cb = cb_ref[...]
    rb = h_ref.shape[0] // FFN_ROW_BLOCKS
    tail = jnp.zeros((HALO, wg.shape[1]), F32)
    for r in range(FFN_ROW_BLOCKS):
        h = h_ref[r * rb:(r + 1) * rb, :]
        gate = _dot(h, wg)
        if wu is None:
            wu = wu_ref[...].astype(BF16)
        up = _dot(h, wu)
        ext = jnp.concatenate([tail, gate], axis=0)
        y = gate * cw[FFN_CONV - 1:FFN_CONV, :] + cb
        for s in range(1, FFN_CONV):
            y = y + pltpu.roll(ext, s, 0)[HALO:, :] * cw[FFN_CONV - 1 - s:FFN_CONV - s, :]
        o_ref[r * rb:(r + 1) * rb, :] = (_silu(y) * up).astype(o_ref.dtype)
        tail = gate[rb - HALO:, :]


def _ffn_in(h, w_in, conv_w, conv_b, w_out, seq, tn=256):
    M, K = h.shape
    N = w_in.shape[1] // 2
    nb = N // tn
    n_steps = (M // seq) * nb
    w2_rows = w_out.shape[0] // n_steps
    assert seq % (FFN_ROW_BLOCKS * HALO) == 0
    assert w2_rows * n_steps == w_out.shape[0] and w2_rows % 16 == 0
    w2_spec = pl.BlockSpec((w2_rows, w_out.shape[1]), lambda i, j: (i * nb + j, 0))
    return pl.pallas_call(
        _ffn_in_kernel,
        grid=(M // seq, nb),
        in_specs=[pl.BlockSpec((seq, K), lambda i, j: (i, 0), pipeline_mode=pl.Buffered(1)),
                  pl.BlockSpec((K, tn), lambda i, j: (0, j)),
                  pl.BlockSpec((K, tn), lambda i, j: (0, nb + j)),
                  pl.BlockSpec((FFN_CONV, tn), lambda i, j: (0, j)),
                  pl.BlockSpec((1, tn), lambda i, j: (0, j)),
                  w2_spec],
        out_specs=[pl.BlockSpec((seq, tn), lambda i, j: (i, j)), w2_spec],
        out_shape=[jax.ShapeDtypeStruct((M, N), BF16), jax.ShapeDtypeStruct(w_out.shape, BF16)],
        compiler_params=_cparams(("parallel", "arbitrary")),
        name="ffn_in",
    )(h, w_in, w_in, conv_w, conv_b.reshape(1, N), w_out)


def _ffn_out_kernel(a_ref, b_ref, r_ref, o_ref, acc_ref):
    kk = pl.program_id(2)

    @pl.when(kk == 0)
    def _():
        acc_ref[...] = r_ref[...]

    acc_ref[...] += _dot(a_ref[...], b_ref[...])

    @pl.when(kk == pl.num_programs(2) - 1)
    def _():
        o_ref[...] = acc_ref[...]


def _ffn_out(a, b, res, tm=512, tn=512, ksplit=1):
    M, K = a.shape
    N = b.shape[1]
    tm = min(tm, M)
    tk = K // ksplit
    return pl.pallas_call(
        _ffn_out_kernel,
        grid=(M // tm, N // tn, ksplit),
        in_specs=[pl.BlockSpec((tm, tk), lambda i, j, k: (i, k)),
                  pl.BlockSpec((tk, tn), lambda i, j, k: (k, j)),
                  pl.BlockSpec((tm, tn), lambda i, j, k: (i, j))],
        out_specs=pl.BlockSpec((tm, tn), lambda i, j, k: (i, j)),
        out_shape=jax.ShapeDtypeStruct((M, N), F32),
        scratch_shapes=[pltpu.VMEM((tm, tn), F32)],
        compiler_params=_cparams(("parallel", "arbitrary", "arbitrary")),
        name="ffn_out",
    )(a, b, res)


def _layer(x2d, batch, seq, norm_mix_w, w_in, dn_conv_w, dn_a_log, dn_dt_bias, dn_norm_w,
           gla_w_alpha2, gla_b_alpha, gla_norm_w, w_branch_dn, w_branch_gla, w_out,
           norm_ffn_w, w_ffn_in, ffn_conv_w, ffn_conv_b, w_ffn_out):
    h = _rmsnorm(x2d, norm_mix_w, BF16)
    proj = _inproj(h, w_in.T)
    o_dn = _deltanet(proj, dn_conv_w, dn_a_log, dn_dt_bias, dn_norm_w, batch, seq)
    o_gla = _gla(proj, gla_w_alpha2, gla_b_alpha, gla_norm_w, batch, seq)
    merged = _merge(o_dn, o_gla, w_branch_dn, w_branch_gla, proj)
    x1 = _matmul_residual(merged, w_out, x2d)

    h2 = _rmsnorm(x1, norm_ffn_w, BF16)
    act, w_ffn_out_bf = _ffn_in(h2, w_ffn_in, ffn_conv_w, ffn_conv_b, w_ffn_out, seq)
    return _ffn_out(act, w_ffn_out_bf, x1)


def kernel(x, norm_mix_w, w_in, dn_conv_w, dn_a_log, dn_dt_bias, dn_norm_w, gla_w_alpha2, gla_b_alpha, gla_norm_w, w_branch_dn, w_branch_gla, w_out, norm_ffn_w, w_ffn_in, ffn_conv_w, ffn_conv_b, w_ffn_out, norm_final_w):
    batch, seq, D = x.shape
    x2d = x.reshape(batch * seq, D)
    for l in range(norm_mix_w.shape[0]):
        x2d = _layer(x2d, batch, seq, norm_mix_w[l], w_in[l], dn_conv_w[l], dn_a_log[l], dn_dt_bias[l],
                     dn_norm_w[l], gla_w_alpha2[l], gla_b_alpha[l], gla_norm_w[l], w_branch_dn[l],
                     w_branch_gla[l], w_out[l], norm_ffn_w[l], w_ffn_in[l], ffn_conv_w[l], ffn_conv_b[l],
                     w_ffn_out[l])
    return _rmsnorm(x2d, norm_final_w, F32).reshape(batch, seq, D)
```

```python
import functools

import numpy as np
import jax
import jax.numpy as jnp
from jax import lax
from jax.experimental import pallas as pl
from jax.experimental.pallas import tpu as pltpu

F32 = jnp.float32
BF16 = jnp.bfloat16
HI = lax.Precision.HIGHEST

EPS = 1e-6
CHUNK = 64
DN_HEADS = 16
DN_D = 128
DN_CONV = 4
GLA_HEADS = 8
GLA_DK = 128
GLA_DV = 256
GLA_RANK = 16
GLA_TAU = 16.0
FFN_CONV = 3
LANES = 128
HALO = 8

C_DQ, C_DK, C_DV, C_DZ = 0, 2048, 4096, 6144
C_DGATE = 8192
C_GQ, C_GK, C_GV, C_GR = 8224, 9248, 10272, 12320
C_LR = 14368
C_GATE_DN, C_GATE_GLA = 14384, 18480
S_B, S_A = 0, 16
S_LR = C_LR % LANES
GLA_SHIFT = C_GQ % LANES
GATE_SHIFT = C_GATE_DN % LANES
assert C_GK % LANES == C_GV % LANES == C_GR % LANES == GLA_SHIFT and C_GATE_GLA % LANES == GATE_SHIFT

VMEM_LIMIT = 56 * 1024 * 1024


def _cparams(sem):
    return pltpu.CompilerParams(dimension_semantics=sem, vmem_limit_bytes=VMEM_LIMIT)


def _sigmoid(x):
    return 0.5 * jnp.tanh(0.5 * x) + 0.5


def _silu(x):
    return x * _sigmoid(x)


def _softplus(x):
    return jnp.maximum(x, 0.0) + jnp.log1p(jnp.exp(-jnp.abs(x)))


def _dot(a, b, precision=None):
    return jnp.dot(a, b, precision=precision, preferred_element_type=F32)


def _dot_nt(a, b, precision=None):
    return lax.dot_general(a, b, (((1,), (1,)), ((), ())), precision=precision, preferred_element_type=F32)


def _dot_tn(a, b, precision=None):
    return lax.dot_general(a, b, (((0,), (0,)), ((), ())), precision=precision, preferred_element_type=F32)


def _rmsnorm_kernel(x_ref, w_ref, o_ref):
    x = x_ref[...]
    ms = jnp.mean(x * x, axis=-1, keepdims=True)
    o_ref[...] = (x * lax.rsqrt(ms + EPS) * w_ref[...]).astype(o_ref.dtype)


def _rmsnorm(x, w, out_dtype, tm=512):
    M, D = x.shape
    tm = min(tm, M)
    return pl.pallas_call(
        _rmsnorm_kernel,
        grid=(M // tm,),
        in_specs=[pl.BlockSpec((tm, D), lambda i: (i, 0)), pl.BlockSpec((1, D), lambda i: (0, 0))],
        out_specs=pl.BlockSpec((tm, D), lambda i: (i, 0)),
        out_shape=jax.ShapeDtypeStruct((M, D), out_dtype),
        compiler_params=_cparams(("parallel",)),
        name="rmsnorm",
    )(x, w.reshape(1, D))


def _inproj_kernel(a_ref, wt_ref, o_ref, *, tail):
    j = pl.program_id(1)
    last = pl.num_programs(1) - 1

    @pl.when(jnp.logical_or(j < last, tail == 0))
    def _():
        hn = wt_ref.shape[0] // 2
        for c in (slice(0, hn), slice(hn, 2 * hn)):
            o_ref[:, c] = _dot_nt(a_ref[...], wt_ref[c, :].astype(BF16))

    if tail:
        @pl.when(j == last)
        def _():
            o_ref[:, :tail] = _dot_nt(a_ref[...], wt_ref[:tail, :].astype(BF16))


def _inproj(a, wt, tm=2048, tn=512):
    M, K = a.shape
    N = wt.shape[0]
    tm = min(tm, M)
    tail = -(-(N % tn) // LANES) * LANES
    return pl.pallas_call(
        functools.partial(_inproj_kernel, tail=tail),
        grid=(M // tm, pl.cdiv(N, tn)),
        in_specs=[pl.BlockSpec((tm, K), lambda i, j: (i, 0), pipeline_mode=pl.Buffered(1)),
                  pl.BlockSpec((tn, K), lambda i, j: (j, 0))],
        out_specs=pl.BlockSpec((tm, tn), lambda i, j: (i, j)),
        out_shape=jax.ShapeDtypeStruct((M, N), F32),
        compiler_params=_cparams(("parallel", "arbitrary")),
        name="inproj",
    )(a, wt)


def _lane_shift(tiles, shift):
    lane = lax.broadcasted_iota(jnp.int32, tiles[0].shape, 1)
    rolled = [pltpu.roll(t, LANES - shift, 1) for t in tiles]
    return [jnp.where(lane < LANES - shift, rolled[i], rolled[i + 1]) for i in range(len(tiles) - 1)]


def _l2norm(x):
    return x * lax.rsqrt(jnp.sum(x * x, axis=-1, keepdims=True) + EPS)


def _split3(x):
    hi = x.astype(BF16)
    r = x - hi.astype(F32)
    mid = r.astype(BF16)
    lo = (r - mid.astype(F32)).astype(BF16)
    return hi, mid, lo


def _split2(x):
    hi = x.astype(BF16)
    return hi, (x - hi.astype(F32)).astype(BF16)


DN_UNROLL = 8
DN_P_PIECES = 4
DN_P_PER_STAGE = 2


def _dn_kernel(q_ref, k_ref, v_ref, z_ref, small_ref, cwq_ref, cwk_ref, cwv_ref, alog_ref, dtb_ref, nw_ref,
               o_ref, gates_s, qs_s, k_s, v_s, beta_s, bc_s, lhs_s, add_s, ebl_s, *, seq):
    h = pl.program_id(1)
    C = CHUNK
    D = DN_D
    U = range(DN_UNROLL)
    n_groups = seq // (C * DN_UNROLL)
    row = lax.broadcasted_iota(jnp.int32, (seq, D), 0)

    @pl.when(h == 0)
    def _():
        small = small_ref[...]
        lane = lax.broadcasted_iota(jnp.int32, small.shape, 1)
        x = -jnp.exp(alog_ref[...]) * _softplus(small + dtb_ref[...])
        in_chunk = row % C
        s = 1
        while s < C:
            x = x + jnp.where(in_chunk >= s, pltpu.roll(x, s, 0), 0.0)
            s *= 2
        gates_s[...] = jnp.where(lane < S_A, _sigmoid(small), x)

    sr = lax.broadcasted_iota(jnp.int32, (3 * LANES, 2 * LANES), 0) % LANES
    sc = lax.broadcasted_iota(jnp.int32, (3 * LANES, 2 * LANES), 1)
    sel = (sr == jnp.where(sc < LANES, h + S_B, h + S_A)).astype(BF16)
    G = C * DN_UNROLL

    def pass_p(g):
        rep = _dot(jnp.concatenate(_split3(gates_s[pl.ds(pl.multiple_of(g * G, G), G), :]), axis=1), sel)
        beta_s[pl.ds(pl.multiple_of(g * G, G), G), :] = rep[:, :LANES]
        bc_s[pl.ds(pl.multiple_of(g * G, G), G), :] = rep[:, LANES:]
        yield
        R = G // DN_P_PIECES
        for piece in range(DN_P_PIECES):
            r0 = g * G + piece * R
            rows = pl.ds(pl.multiple_of(r0, R), R)
            halo_rows = pl.ds(pl.multiple_of(jnp.maximum(r0 - HALO, 0), HALO), HALO)
            for x_ref, cw_ref, out_s, norm in ((q_ref, cwq_ref, qs_s, D ** -0.5), (k_ref, cwk_ref, k_s, 1.0),
                                               (v_ref, cwv_ref, v_s, None)):
                cw = cw_ref[...]
                halo = jnp.where(r0 == 0, 0.0, x_ref[halo_rows, :])
                x = x_ref[rows, :]
                ext = jnp.concatenate([halo, x], axis=0)
                y = x * cw[DN_CONV - 1:DN_CONV, :]
                for s in range(1, DN_CONV):
                    y = y + pltpu.roll(ext, s, 0)[HALO:, :] * cw[DN_CONV - 1 - s:DN_CONV - s, :]
                y = _silu(y)
                out_s[rows, :] = y if norm is None else _l2norm(y) * norm
                yield

    ri = lax.broadcasted_iota(jnp.int32, (C, C), 0)
    ci = lax.broadcasted_iota(jnp.int32, (C, C), 1)
    tril = ri >= ci
    strict = ri > ci
    eye = (ri == ci).astype(F32)
    nw = nw_ref[...]

    def pass_a(g):
        ns = [g * DN_UNROLL + u for u in U]
        sls = [pl.ds(pl.multiple_of(n * C, C), C) for n in ns]
        qc = [qs_s[s, :] for s in sls]
        kc = [k_s[s, :] for s in sls]
        vc = [v_s[s, :] for s in sls]
        bt = [beta_s[s, :] for s in sls]
        bc = [bc_s[s, :] for s in sls]
        kb = [kc[u] * bt[u] for u in U]
        gram = [_dot_nt(jnp.concatenate([kb[u], qc[u]], axis=0).astype(BF16), kc[u].astype(BF16)) for u in U]
        yield
        dm = [bc[u][:, :C] - jnp.transpose(bc[u])[:C, :] for u in U]
        decay = [jnp.where(tril, jnp.exp(jnp.minimum(dm[u], 0.0)), 0.0) for u in U]
        eb = [jnp.exp(bc[u]) for u in U]
        a = [jnp.where(strict, gram[u][:C] * decay[u], 0.0) for u in U]
        att = [(gram[u][C:] * decay[u]).astype(BF16) for u in U]
        p = [eye - a[u] for u in U]
        a_bf = [a[u].astype(BF16) for u in U]
        x = [_dot(a_bf[u], a_bf[u]) for u in U]
        yield
        for _ in range(4):
            px = [_dot(jnp.concatenate([p[u], x[u]], axis=0).astype(BF16), x[u].astype(BF16)) for u in U]
            p = [p[u] + px[u][:C] for u in U]
            x = [px[u][C:] for u in U]
            yield
        px = [_dot(p[u].astype(BF16), x[u].astype(BF16)) for u in U]
        p = [(p[u] + px[u]).astype(BF16) for u in U]
        yield
        rhs = [jnp.concatenate([kb[u] * eb[u], vc[u] * bt[u]], axis=1).astype(BF16) for u in U]
        wu = [_dot(p[u], rhs[u]).astype(BF16) for u in U]
        yield
        kd = [(kc[u] * jnp.exp(bc[u][C - 1:C, :] - bc[u])).astype(BF16) for u in U]
        aw = [_dot(att[u], wu[u]) for u in U]
        yield
        mn = [_dot_tn(kd[u], wu[u]) for u in U]
        for u in U:
            n = ns[u]
            lhs_s[n, 0:D, :] = mn[u][:, :D].astype(BF16)
            lhs_s[n, D:D + C, :] = (qc[u] * eb[u] - aw[u][:, :D]).astype(BF16)
            add_s[n, 0:D, :] = mn[u][:, D:]
            add_s[n, D:D + C, :] = aw[u][:, D:]
            ebl_s[n] = jnp.broadcast_to(eb[u][C - 1:C, :], (8, LANES))
        yield

    def finish(n, o):
        sl = pl.ds(pl.multiple_of(n * C, C), C)
        y = o * lax.rsqrt(jnp.mean(o * o, axis=-1, keepdims=True) + EPS) * nw * _silu(z_ref[sl, :])
        o_ref[sl, :] = y.astype(o_ref.dtype)

    def step_b(n, carry):
        S, o_prev = carry
        finish(jnp.maximum(n - 1, 0), o_prev)
        ps = _dot(lhs_s[n], S.astype(BF16))
        add = add_s[n]
        o = ps[D:] + add[D:]
        S = ebl_s[n][0:1, :] * S + (add[:D] - ps[:D])
        return S, o

    def interleave(g_a, g_b, g_p, carry):
        stages = pass_a(g_a) if g_a is not None else iter(())
        steps = [g_b * DN_UNROLL + u for u in U] if g_b is not None else []
        pieces = pass_p(g_p) if g_p is not None else iter(())
        next(stages, None)
        for _ in stages:
            if steps:
                carry = step_b(steps.pop(0), carry)
            for _ in range(DN_P_PER_STAGE):
                next(pieces, None)
        for n in steps:
            carry = step_b(n, carry)
        for _ in pieces:
            pass
        return carry

    assert n_groups >= 2
    carry = (jnp.zeros((D, D), F32), jnp.zeros((C, D), F32))
    for _ in pass_p(0):
        pass
    carry = interleave(0, None, 1, carry)
    carry = lax.fori_loop(1, n_groups - 1, lambda g, c: interleave(g, g - 1, g + 1, c), carry)
    carry = interleave(n_groups - 1, n_groups - 2, None, carry)
    _, o_last = interleave(None, n_groups - 1, None, carry)
    finish(seq // C - 1, o_last)


def _deltanet(proj, conv_w, a_log, dt_bias, norm_w, batch, seq):
    M = proj.shape[0]
    H = DN_HEADS
    n_chunks = seq // CHUNK
    assert seq % (CHUNK * DN_UNROLL) == 0
    blk = lambda c0: pl.BlockSpec((seq, DN_D), lambda b, h: (b, c0 // DN_D + h))
    cw = lambda c0: pl.BlockSpec((DN_CONV, DN_D), lambda b, h: (0, c0 // DN_D + h))
    row = pl.BlockSpec((1, LANES), lambda b, h: (0, 0))
    alog = jnp.zeros((1, LANES), F32).at[0, S_A:S_A + H].set(a_log)
    dtb = jnp.zeros((1, LANES), F32).at[0, S_A:S_A + H].set(dt_bias)
    seq_f32 = pltpu.VMEM((seq, DN_D), F32)
    return pl.pallas_call(
        functools.partial(_dn_kernel, seq=seq),
        grid=(batch, H),
        in_specs=[blk(C_DQ), blk(C_DK), blk(C_DV), blk(C_DZ),
                  pl.BlockSpec((seq, LANES), lambda b, h: (b, C_DGATE // LANES)),
                  cw(0), cw(2048), cw(4096), row, row, row],
        out_specs=pl.BlockSpec((seq, DN_D), lambda b, h: (b, h)),
        out_shape=jax.ShapeDtypeStruct((M, H * DN_D), BF16),
        scratch_shapes=[seq_f32] * 6 + [pltpu.VMEM((n_chunks, DN_D + CHUNK, DN_D), BF16),
                                        pltpu.VMEM((n_chunks, DN_D + CHUNK, DN_D), F32),
                                        pltpu.VMEM((n_chunks, 8, LANES), F32)],
        compiler_params=_cparams(("parallel", "arbitrary")),
        name="deltanet",
    )(proj, proj, proj, proj, proj, conv_w, conv_w, conv_w, alog, dtb, norm_w.reshape(1, DN_D))


GLA_LEVELS = (32, 16, 8, 4, 2, 1)
GLA_UNROLL = 4


def _gla_constants():
    C = CHUNK
    i = np.arange(C)[:, None]
    m = np.arange(C)[None, :]
    blocks = [(m <= i), (m > i)]
    upper_rows, masks = [], []
    for s in GLA_LEVELS:
        p = (i // (2 * s)) * (2 * s)
        m0 = p + s - 1
        up = (i - p) >= s
        blocks.append(np.where(up, (m > m0) & (m <= i), (m > i) & (m <= m0)))
        upper_rows.append(np.broadcast_to(up, (C, LANES)))
        j = m
        pj = (j // (2 * s)) * (2 * s)
        masks.append(up & ((j - pj) < s) & (p == pj))
    masks.append(i == m)
    mcat = np.concatenate(blocks, axis=0).astype(np.float32)
    upper = np.stack(upper_rows).astype(np.float32)
    mask = np.stack(masks).astype(np.float32)
    return mcat, upper, mask


N_QK_TILES = GLA_DK // LANES + 1
N_VR_TILES = GLA_DV // LANES + 1


def _gla_kernel(*refs, seq):
    q_t, refs = refs[:N_QK_TILES], refs[N_QK_TILES:]
    k_t, refs = refs[:N_QK_TILES], refs[N_QK_TILES:]
    v_t, refs = refs[:N_VR_TILES], refs[N_VR_TILES:]
    r_t, refs = refs[:N_VR_TILES], refs[N_VR_TILES:]
    lr_ref, wa_ref, ba_ref, nw_ref, mcat_ref, upper_ref, mask_ref, o_ref, gk_s, st_s, o_s = refs
    C = CHUNK

    def rows(tile_refs, s):
        return jnp.concatenate(_lane_shift([t[s, :] for t in tile_refs], GLA_SHIFT), axis=1)

    x_alpha = _dot(lr_ref[...], wa_ref[...], HI) + ba_ref[...]
    gk_s[...] = -_softplus(-x_alpha) / GLA_TAU
    st_s[...] = jnp.zeros_like(st_s)
    mcat = mcat_ref[...]
    nw = nw_ref[...]
    n_lvl = len(GLA_LEVELS)

    U = range(GLA_UNROLL)
    o_s[...] = jnp.zeros_like(o_s)

    def finish(g):
        for u in U:
            sl = pl.ds(pl.multiple_of((g * GLA_UNROLL + u) * C, C), C)
            o = o_s[u * C:(u + 1) * C, :]
            y = o * lax.rsqrt(jnp.mean(o * o, axis=-1, keepdims=True) + EPS) * nw * _silu(rows(r_t, sl))
            o_ref[sl, :] = y.astype(o_ref.dtype)

    def body(g, carry):
        finish(jnp.maximum(g - 1, 0))
        sls = [pl.ds(pl.multiple_of((g * GLA_UNROLL + u) * C, C), C) for u in U]
        qc = [rows(q_t, s) * (GLA_DK ** -0.5) for s in sls]
        kc = [rows(k_t, s) for s in sls]
        v_bf = [rows(v_t, s).astype(BF16) for s in sls]
        e = [_dot(mcat, jnp.concatenate(_split2(gk_s[s, :]), axis=1)) for s in sls]
        ex = [jnp.exp(e[u][:, :GLA_DK] + e[u][:, GLA_DK:]) for u in U]
        k_bf = [kc[u].astype(BF16) for u in U]
        att = [mask_ref[n_lvl] * _dot_nt(qc[u].astype(BF16), k_bf[u]) for u in U]
        for l in range(n_lvl):
            up = upper_ref[l] > 0.5
            zl = [(jnp.where(up, qc[u], kc[u]) * ex[u][(2 + l) * C:(3 + l) * C, :]).astype(BF16) for u in U]
            gl = [_dot_nt(zl[u], zl[u]) for u in U]
            att = [att[u] + mask_ref[l] * gl[u] for u in U]
        ds = [_dot_tn(v_bf[u], (kc[u] * ex[u][C:2 * C, :]).astype(BF16)) for u in U]
        ov = [_dot(att[u].astype(BF16), v_bf[u]) for u in U]
        st = st_s[...]
        for u in U:
            eb = ex[u][0:C, :]
            o_s[u * C:(u + 1) * C, :] = _dot_nt((qc[u] * eb).astype(BF16), st.astype(BF16)) + ov[u]
            st = st * eb[C - 1:C, :] + ds[u]
        st_s[...] = st
        return carry

    n_groups = seq // (C * GLA_UNROLL)
    lax.fori_loop(0, n_groups, body, 0)
    finish(n_groups - 1)


def _gla(proj, w_alpha2, b_alpha, norm_w, batch, seq):
    M = proj.shape[0]
    H = GLA_HEADS
    assert seq % (CHUNK * GLA_UNROLL) == 0
    mcat, upper, mask = _gla_constants()
    wa = jnp.zeros((LANES, H * GLA_DK), F32).at[S_LR:S_LR + GLA_RANK, :].set(w_alpha2)
    full = lambda a: pl.BlockSpec(a.shape, lambda b, h: (0,) * a.ndim)

    def tiles(c0, width, n):
        return [pl.BlockSpec((seq, LANES), lambda b, h, t=t: (b, c0 // LANES + h * (width // LANES) + t))
                for t in range(n)]

    n_in = 2 * N_QK_TILES + 2 * N_VR_TILES
    return pl.pallas_call(
        functools.partial(_gla_kernel, seq=seq),
        grid=(batch, H),
        in_specs=[*tiles(C_GQ, GLA_DK, N_QK_TILES), *tiles(C_GK, GLA_DK, N_QK_TILES),
                  *tiles(C_GV, GLA_DV, N_VR_TILES), *tiles(C_GR, GLA_DV, N_VR_TILES),
                  pl.BlockSpec((seq, LANES), lambda b, h: (b, C_LR // LANES)),
                  pl.BlockSpec((LANES, GLA_DK), lambda b, h: (0, h)),
                  pl.BlockSpec((1, GLA_DK), lambda b, h: (0, h)),
                  pl.BlockSpec((1, GLA_DV), lambda b, h: (0, 0)),
                  full(mcat), full(upper), full(mask)],
        out_specs=pl.BlockSpec((seq, GLA_DV), lambda b, h: (b, h)),
        out_shape=jax.ShapeDtypeStruct((M, H * GLA_DV), BF16),
        scratch_shapes=[pltpu.VMEM((seq, GLA_DK), F32), pltpu.VMEM((GLA_DV, GLA_DK), F32),
                        pltpu.VMEM((GLA_UNROLL * CHUNK, GLA_DV), F32)],
        compiler_params=_cparams(("parallel", "parallel")),
        name="gla",
    )(*([proj] * (n_in + 1)), wa, b_alpha.reshape(1, -1), norm_w.reshape(1, GLA_DV),
      jnp.asarray(mcat, BF16), jnp.asarray(upper), jnp.asarray(mask))


MERGE_ROW_BLOCKS = 2


def _merge_kernel(odn_ref, ogla_ref, pa_ref, pb_ref, gd_ref, gdn_ref, gg_ref, ggn_ref, o_ref):
    def gate(main_ref, next_ref, rows):
        n = main_ref.shape[1] // LANES
        tiles = [main_ref[rows, t * LANES:(t + 1) * LANES] for t in range(n)] + [next_ref[rows, :]]
        return _sigmoid(jnp.concatenate(_lane_shift(tiles, GATE_SHIFT), axis=1))

    pa = pa_ref[...].astype(BF16)
    pb = None
    rb = o_ref.shape[0] // MERGE_ROW_BLOCKS
    for r in range(MERGE_ROW_BLOCKS):
        rows = slice(r * rb, (r + 1) * rb)
        ya = _dot(odn_ref[rows, :], pa)
        if pb is None:
            pb = pb_ref[...].astype(BF16)
        yb = _dot(ogla_ref[rows, :], pb)
        o_ref[rows, :] = (gate(gd_ref, gdn_ref, rows) * ya + gate(gg_ref, ggn_ref, rows) * yb).astype(o_ref.dtype)


def _merge(o_dn, o_gla, pa, pb, proj, tm=2048, tn=256):
    M, K = o_dn.shape
    N = pa.shape[1]
    tm = min(tm, M)
    lt = tn // LANES
    main = lambda c0: pl.BlockSpec((tm, tn), lambda i, j: (i, c0 // tn + j))
    nxt = lambda c0: pl.BlockSpec((tm, LANES), lambda i, j: (i, c0 // LANES + (j + 1) * lt))
    assert (C_GATE_DN - GATE_SHIFT) % tn == 0 and (C_GATE_GLA - GATE_SHIFT) % tn == 0
    assert tm % (MERGE_ROW_BLOCKS * 16) == 0
    return pl.pallas_call(
        _merge_kernel,
        grid=(M // tm, N // tn),
        in_specs=[pl.BlockSpec((tm, K), lambda i, j: (i, 0), pipeline_mode=pl.Buffered(1)),
                  pl.BlockSpec((tm, K), lambda i, j: (i, 0), pipeline_mode=pl.Buffered(1)),
                  pl.BlockSpec((K, tn), lambda i, j: (0, j)),
                  pl.BlockSpec((K, tn), lambda i, j: (0, j)),
                  main(C_GATE_DN), nxt(C_GATE_DN), main(C_GATE_GLA), nxt(C_GATE_GLA)],
        out_specs=pl.BlockSpec((tm, tn), lambda i, j: (i, j)),
        out_shape=jax.ShapeDtypeStruct((M, N), BF16),
        compiler_params=_cparams(("parallel", "arbitrary")),
        name="merge",
    )(o_dn, o_gla, pa, pb, proj, proj, proj, proj)


def _mm_res_kernel(a_ref, b_ref, r_ref, o_ref):
    o_ref[...] = r_ref[...] + _dot(a_ref[...], b_ref[...].astype(BF16))


def _matmul_residual(a, b, res, tm=1024, tn=512, name="outproj"):
    M, K = a.shape
    N = b.shape[1]
    tm = min(tm, M)
    return pl.pallas_call(
        _mm_res_kernel,
        grid=(M // tm, N // tn),
        in_specs=[pl.BlockSpec((tm, K), lambda i, j: (i, 0)),
                  pl.BlockSpec((K, tn), lambda i, j: (0, j)),
                  pl.BlockSpec((tm, tn), lambda i, j: (i, j))],
        out_specs=pl.BlockSpec((tm, tn), lambda i, j: (i, j)),
        out_shape=jax.ShapeDtypeStruct((M, N), F32),
        compiler_params=_cparams(("parallel", "arbitrary")),
        name=name,
    )(a, b, res)


FFN_ROW_BLOCKS = 4


def _ffn_in_kernel(h_ref, wg_ref, wu_ref, cw_ref, cb_ref, w2_ref, o_ref, w2_bf_ref):
    w2_bf_ref[...] = w2_ref[...].astype(BF16)
    wg = wg_ref[...].astype(BF16)
    wu = None
    cw = cw_ref[...]
    cb = cb_ref[...]
    rb = h_ref.shape[0] // FFN_ROW_BLOCKS
    tail = jnp.zeros((HALO, wg.shape[1]), F32)
    for r in range(FFN_ROW_BLOCKS):
        h = h_ref[r * rb:(r + 1) * rb, :]
        gate = _dot(h, wg)
        if wu is None:
            wu = wu_ref[...].astype(BF16)
        up = _dot(h, wu)
        ext = jnp.concatenate([tail, gate], axis=0)
        y = gate * cw[FFN_CONV - 1:FFN_CONV, :] + cb
        for s in range(1, FFN_CONV):
            y = y + pltpu.roll(ext, s, 0)[HALO:, :] * cw[FFN_CONV - 1 - s:FFN_CONV - s, :]
        o_ref[r * rb:(r + 1) * rb, :] = (_silu(y) * up).astype(o_ref.dtype)
        tail = gate[rb - HALO:, :]


def _ffn_in(h, w_in, conv_w, conv_b, w_out, seq, tn=256):
    M, K = h.shape
    N = w_in.shape[1] // 2
    nb = N // tn
    n_steps = (M // seq) * nb
    w2_rows = w_out.shape[0] // n_steps
    assert seq % (FFN_ROW_BLOCKS * HALO) == 0
    assert w2_rows * n_steps == w_out.shape[0] and w2_rows % 16 == 0
    w2_spec = pl.BlockSpec((w2_rows, w_out.shape[1]), lambda i, j: (i * nb + j, 0))
    return pl.pallas_call(
        _ffn_in_kernel,
        grid=(M // seq, nb),
        in_specs=[pl.BlockSpec((seq, K), lambda i, j: (i, 0), pipeline_mode=pl.Buffered(1)),
                  pl.BlockSpec((K, tn), lambda i, j: (0, j)),
                  pl.BlockSpec((K, tn), lambda i, j: (0, nb + j)),
                  pl.BlockSpec((FFN_CONV, tn), lambda i, j: (0, j)),
                  pl.BlockSpec((1, tn), lambda i, j: (0, j)),
                  w2_spec],
        out_specs=[pl.BlockSpec((seq, tn), lambda i, j: (i, j)), w2_spec],
        out_shape=[jax.ShapeDtypeStruct((M, N), BF16), jax.ShapeDtypeStruct(w_out.shape, BF16)],
        compiler_params=_cparams(("parallel", "arbitrary")),
        name="ffn_in",
    )(h, w_in, w_in, conv_w, conv_b.reshape(1, N), w_out)


def _ffn_out_kernel(a_ref, b_ref, r_ref, o_ref, acc_ref):
    kk = pl.program_id(2)

    @pl.when(kk == 0)
    def _():
        acc_ref[...] = r_ref[...]

    acc_ref[...] += _dot(a_ref[...], b_ref[...])

    @pl.when(kk == pl.num_programs(2) - 1)
    def _():
        o_ref[...] = acc_ref[...]


def _ffn_out(a, b, res, tm=512, tn=512, ksplit=1):
    M, K = a.shape
    N = b.shape[1]
    tm = min(tm, M)
    tk = K // ksplit
    return pl.pallas_call(
        _ffn_out_kernel,
        grid=(M // tm, N // tn, ksplit),
        in_specs=[pl.BlockSpec((tm, tk), lambda i, j, k: (i, k)),
                  pl.BlockSpec((tk, tn), lambda i, j, k: (k, j)),
                  pl.BlockSpec((tm, tn), lambda i, j, k: (i, j))],
        out_specs=pl.BlockSpec((tm, tn), lambda i, j, k: (i, j)),
        out_shape=jax.ShapeDtypeStruct((M, N), F32),
        scratch_shapes=[pltpu.VMEM((tm, tn), F32)],
        compiler_params=_cparams(("parallel", "arbitrary", "arbitrary")),
        name="ffn_out",
    )(a, b, res)


def _layer(x2d, batch, seq, norm_mix_w, w_in, dn_conv_w, dn_a_log, dn_dt_bias, dn_norm_w,
           gla_w_alpha2, gla_b_alpha, gla_norm_w, w_branch_dn, w_branch_gla, w_out,
           norm_ffn_w, w_ffn_in, ffn_conv_w, ffn_conv_b, w_ffn_out):
    h = _rmsnorm(x2d, norm_mix_w, BF16)
    proj = _inproj(h, w_in.T)
    o_dn = _deltanet(proj, dn_conv_w, dn_a_log, dn_dt_bias, dn_norm_w, batch, seq)
    o_gla = _gla(proj, gla_w_alpha2, gla_b_alpha, gla_norm_w, batch, seq)
    merged = _merge(o_dn, o_gla, w_branch_dn, w_branch_gla, proj)
    x1 = _matmul_residual(merged, w_out, x2d)

    h2 = _rmsnorm(x1, norm_ffn_w, BF16)
    act, w_ffn_out_bf = _ffn_in(h2, w_ffn_in, ffn_conv_w, ffn_conv_b, w_ffn_out, seq)
    return _ffn_out(act, w_ffn_out_bf, x1)


def kernel(x, norm_mix_w, w_in, dn_conv_w, dn_a_log, dn_dt_bias, dn_norm_w, gla_w_alpha2, gla_b_alpha, gla_norm_w, w_branch_dn, w_branch_gla, w_out, norm_ffn_w, w_ffn_in, ffn_conv_w, ffn_conv_b, w_ffn_out, norm_final_w):
    batch, seq, D = x.shape
    x2d = x.reshape(batch * seq, D)
    for l in range(norm_mix_w.shape[0]):
        x2d = _layer(x2d, batch, seq, norm_mix_w[l], w_in[l], dn_conv_w[l], dn_a_log[l], dn_dt_bias[l],
                     dn_norm_w[l], gla_w_alpha2[l], gla_b_alpha[l], gla_norm_w[l], w_branch_dn[l],
                     w_branch_gla[l], w_out[l], norm_ffn_w[l], w_ffn_in[l], ffn_conv_w[l], ffn_conv_b[l],
                     w_ffn_out[l])
    return _rmsnorm(x2d, norm_final_w, F32).reshape(batch, seq, D)
```
